```python
import jax, jax.numpy as jnp
from jax import lax
import numpy as np

D_MODEL = 1024
BATCH = 4
SEQ = 4096
DEPTH = 2

CHUNK = 64

D_A = 512
CONV_A_WIDTH = 31
D_B = 512
POOL_WINDOWS = (2, 4, 8, 16)
POOL_GROUPS = len(POOL_WINDOWS)
POOL_GROUP_DIM = D_B // POOL_GROUPS
D_MIX_EVEN = D_A + D_B
IN_EVEN = 2 * D_A + D_B

D_C = 512
CONV_C_WIDTH = 3
D_D = 512
SGU_BLOCK = 128
SGU_HEADS = 4
SGU_HEAD_DIM = D_D // SGU_HEADS
D_MIX_ODD = D_C + D_D
IN_ODD = 3 * D_C + 2 * D_D

D_FF_DENSE = 2816
N_EXPERTS = 8
TOP_K = 2
D_FF_EXPERT = 3584

ALPHA = (2 * DEPTH) ** 0.25
BETA = (8 * DEPTH) ** -0.25
LN_EPS = 1e-5

N_EVEN = (DEPTH + 1) // 2
N_ODD = DEPTH // 2

kernel_name = "hybrid_conv_pool_shortconv_sgu_moe_deepnorm"


def layer_norm(x, g, b):
    xf = x.astype(jnp.float32)
    mu = jnp.mean(xf, axis=-1, keepdims=True)
    var = jnp.mean(jnp.square(xf - mu), axis=-1, keepdims=True)
    y = (xf - mu) * lax.rsqrt(var + LN_EPS)
    return (y * g.astype(jnp.float32) + b.astype(jnp.float32)).astype(x.dtype)


def causal_depthwise_conv(x, w):
    k = w.shape[0]
    return lax.conv_general_dilated(
        x, w[:, None, :], window_strides=(1,), padding=((k - 1, 0),),
        dimension_numbers=("NWC", "WIO", "NWC"), feature_group_count=x.shape[-1])


def swiglu(x, w_gate, w_up, w_down):
    return (jax.nn.silu(x @ w_gate) * (x @ w_up)) @ w_down


def conformer_conv(a_in, conv_w, conv_b, norm_g, norm_b):
    val, gate = jnp.split(a_in, 2, axis=-1)
    h = val * jax.nn.sigmoid(gate)
    h = causal_depthwise_conv(h, conv_w) + conv_b
    h = layer_norm(h, norm_g, norm_b)
    return jax.nn.silu(h)


def multiscale_pool(b_in, group_w, scale):
    bsz, s, _ = b_in.shape
    xg = b_in.astype(jnp.float32).reshape(bsz, s, POOL_GROUPS, POOL_GROUP_DIM)
    csum = jnp.cumsum(xg, axis=1)
    pos = jnp.arange(s)
    outs = []
    for g, win in enumerate(POOL_WINDOWS):
        cs = csum[:, :, g]
        lag = jnp.pad(cs, ((0, 0), (win, 0), (0, 0)))[:, :s]
        cnt = jnp.minimum(pos + 1, win).astype(jnp.float32)[None, :, None]
        outs.append((cs - lag) / cnt - xg[:, :, g])
    pooled = jnp.stack(outs, axis=2).astype(b_in.dtype)
    mixed = jnp.einsum("bsgc,gcd->bsgd", pooled, group_w).reshape(bsz, s, D_B)
    return mixed * scale


def short_gated_conv(c_in, conv_w):
    gate_b, gate_c, v = jnp.split(c_in, 3, axis=-1)
    return gate_b * causal_depthwise_conv(gate_c * v, conv_w)


def spatial_gating(d_in, norm_g, norm_b, w_s, b_s):
    z = jax.nn.gelu(d_in)
    u, v = jnp.split(z, 2, axis=-1)
    v = layer_norm(v, norm_g, norm_b)
    bsz, s, _ = v.shape
    nb = s // SGU_BLOCK
    v = v.reshape(bsz, nb, SGU_BLOCK, SGU_HEADS, SGU_HEAD_DIM)
    mask = jnp.tril(jnp.ones((SGU_BLOCK, SGU_BLOCK), dtype=bool))
    w = jnp.where(mask[None], w_s, jnp.zeros_like(w_s))
    mixed = jnp.einsum("hij,bnjhc->bnihc", w, v) + b_s.T[None, None, :, :, None]
    return u * mixed.reshape(bsz, s, D_D)


def moe_swiglu(x, router, w_gate, w_up, w_down):
    bsz, s, d = x.shape
    t = x.reshape(-1, d)
    logits = (t @ router).astype(jnp.float32)
    top_logits, top_idx = lax.top_k(logits, TOP_K)
    top_w = jax.nn.softmax(top_logits, axis=-1)
    gates = jnp.sum(jax.nn.one_hot(top_idx, N_EXPERTS, dtype=jnp.float32) * top_w[..., None], axis=1)
    gates = gates.astype(t.dtype)
    out = jnp.zeros_like(t)
    for e in range(N_EXPERTS):
        out = out + gates[:, e:e + 1] * swiglu(t, w_gate[e], w_up[e], w_down[e])
    return out.reshape(bsz, s, d)


def even_layer(x, w_in, conv_a_w, conv_a_b, norm_a_g, norm_a_b, pool_w, pool_scale, w_out,
               ln1_g, ln1_b, ffn_w_gate, ffn_w_up, ffn_w_down, ln2_g, ln2_b):
    h = x @ w_in
    y_a = conformer_conv(h[..., :2 * D_A], conv_a_w, conv_a_b, norm_a_g, norm_a_b)
    y_b = multiscale_pool(h[..., 2 * D_A:], pool_w, pool_scale)
    mix = jnp.concatenate([y_a, y_b], axis=-1) @ w_out
    x = layer_norm(ALPHA * x + mix, ln1_g, ln1_b)
    x = layer_norm(ALPHA * x + swiglu(x, ffn_w_gate, ffn_w_up, ffn_w_down), ln2_g, ln2_b)
    return x


def odd_layer(x, w_in, conv_c_w, sgu_norm_g, sgu_norm_b, sgu_w, sgu_b, w_out,
              ln1_g, ln1_b, router, moe_w_gate, moe_w_up, moe_w_down, ln2_g, ln2_b):
    h = x @ w_in
    y_c = short_gated_conv(h[..., :3 * D_C], conv_c_w)
    y_d = spatial_gating(h[..., 3 * D_C:], sgu_norm_g, sgu_norm_b, sgu_w, sgu_b)
    mix = jnp.concatenate([y_c, y_d], axis=-1) @ w_out
    x = layer_norm(ALPHA * x + mix, ln1_g, ln1_b)
    x = layer_norm(ALPHA * x + moe_swiglu(x, router, moe_w_gate, moe_w_up, moe_w_down), ln2_g, ln2_b)
    return x


def setup_inputs(seed: int = 0) -> dict:
    key = jax.random.key(seed)
    ks = jax.random.split(key, 32)
    f32 = jnp.float32

    def nrm(k, shape, scale):
        return jax.random.normal(k, shape, f32) * scale

    def gain(k, shape):
        return 1.0 + 0.01 * jax.random.normal(k, shape, f32)

    d = D_MODEL
    return {
        "x": nrm(ks[0], (BATCH, SEQ, d), 1.0),
        "even_w_in": nrm(ks[1], (N_EVEN, d, IN_EVEN), d ** -0.5),
        "even_conv_a_w": nrm(ks[2], (N_EVEN, CONV_A_WIDTH, D_A), CONV_A_WIDTH ** -0.5),
        "even_conv_a_b": nrm(ks[3], (N_EVEN, D_A), 0.01),
        "even_norm_a_g": gain(ks[4], (N_EVEN, D_A)),
        "even_norm_a_b": nrm(ks[5], (N_EVEN, D_A), 0.01),
        "even_pool_w": nrm(ks[6], (N_EVEN, POOL_GROUPS, POOL_GROUP_DIM, POOL_GROUP_DIM), POOL_GROUP_DIM ** -0.5),
        "even_pool_scale": 1.0 + 0.05 * jax.random.normal(ks[7], (N_EVEN, D_B), f32),
        "even_w_out": nrm(ks[8], (N_EVEN, D_MIX_EVEN, d), D_MIX_EVEN ** -0.5 * BETA),
        "even_ln1_g": gain(ks[9], (N_EVEN, d)),
        "even_ln1_b": nrm(ks[10], (N_EVEN, d), 0.01),
        "even_ffn_w_gate": nrm(ks[11], (N_EVEN, d, D_FF_DENSE), d ** -0.5),
        "even_ffn_w_up": nrm(ks[12], (N_EVEN, d, D_FF_DENSE), d ** -0.5),
        "even_ffn_w_down": nrm(ks[13], (N_EVEN, D_FF_DENSE, d), D_FF_DENSE ** -0.5 * BETA),
        "even_ln2_g": gain(ks[14], (N_EVEN, d)),
        "even_ln2_b": nrm(ks[15], (N_EVEN, d), 0.01),
        "odd_w_in": nrm(ks[16], (N_ODD, d, IN_ODD), d ** -0.5),
        "odd_conv_c_w": nrm(ks[17], (N_ODD, CONV_C_WIDTH, D_C), CONV_C_WIDTH ** -0.5),
        "odd_sgu_norm_g": gain(ks[18], (N_ODD, D_D)),
        "odd_sgu_norm_b": nrm(ks[19], (N_ODD, D_D), 0.01),
        "odd_sgu_w": nrm(ks[20], (N_ODD, SGU_HEADS, SGU_BLOCK, SGU_BLOCK), SGU_BLOCK ** -0.5),
        "odd_sgu_b": gain(ks[21], (N_ODD, SGU_HEADS, SGU_BLOCK)),
        "odd_w_out": nrm(ks[22], (N_ODD, D_MIX_ODD, d), D_MIX_ODD ** -0.5 * BETA),
        "odd_ln1_g": gain(ks[23], (N_ODD, d)),
        "odd_ln1_b": nrm(ks[24], (N_ODD, d), 0.01),
        "odd_router": nrm(ks[25], (N_ODD, d, N_EXPERTS), d ** -0.5),
        "odd_moe_w_gate": nrm(ks[26], (N_ODD, N_EXPERTS, d, D_FF_EXPERT), d ** -0.5),
        "odd_moe_w_up": nrm(ks[27], (N_ODD, N_EXPERTS, d, D_FF_EXPERT), d ** -0.5),
        "odd_moe_w_down": nrm(ks[28], (N_ODD, N_EXPERTS, D_FF_EXPERT, d), D_FF_EXPERT ** -0.5 * BETA),
        "odd_ln2_g": gain(ks[29], (N_ODD, d)),
        "odd_ln2_b": nrm(ks[30], (N_ODD, d), 0.01),
    }


def reference(x, even_w_in, even_conv_a_w, even_conv_a_b, even_norm_a_g, even_norm_a_b,
              even_pool_w, even_pool_scale, even_w_out, even_ln1_g, even_ln1_b,
              even_ffn_w_gate, even_ffn_w_up, even_ffn_w_down, even_ln2_g, even_ln2_b,
              odd_w_in, odd_conv_c_w, odd_sgu_norm_g, odd_sgu_norm_b, odd_sgu_w, odd_sgu_b,
              odd_w_out, odd_ln1_g, odd_ln1_b, odd_router, odd_moe_w_gate, odd_moe_w_up,
              odd_moe_w_down, odd_ln2_g, odd_ln2_b):
    for layer in range(DEPTH):
        i = layer // 2
        if layer % 2 == 0:
            x = even_layer(x, even_w_in[i], even_conv_a_w[i], even_conv_a_b[i], even_norm_a_g[i],
                           even_norm_a_b[i], even_pool_w[i], even_pool_scale[i], even_w_out[i],
                           even_ln1_g[i], even_ln1_b[i], even_ffn_w_gate[i], even_ffn_w_up[i],
                           even_ffn_w_down[i], even_ln2_g[i], even_ln2_b[i])
        else:
            x = odd_layer(x, odd_w_in[i], odd_conv_c_w[i], odd_sgu_norm_g[i], odd_sgu_norm_b[i],
                          odd_sgu_w[i], odd_sgu_b[i], odd_w_out[i], odd_ln1_g[i], odd_ln1_b[i],
                          odd_router[i], odd_moe_w_gate[i], odd_moe_w_up[i], odd_moe_w_down[i],
                          odd_ln2_g[i], odd_ln2_b[i])
    return x
```

```python
import functools

import jax
import jax.numpy as jnp
from jax import lax
from jax.experimental import pallas as pl
from jax.experimental.pallas import tpu as pltpu

F32 = jnp.float32
BF16 = jnp.bfloat16

D_MODEL = 1024
BATCH = 4
SEQ = 4096
TOKENS = BATCH * SEQ
DEPTH = 2

D_A = 512
CONV_A_WIDTH = 31
D_B = 512
POOL_WINDOWS = (2, 4, 8, 16)
POOL_GROUP_DIM = 128
IN_EVEN = 2 * D_A + D_B

D_C = 512
CONV_C_WIDTH = 3
D_D = 512
SGU_BLOCK = 128
SGU_HEADS = 4
SGU_HEAD_DIM = 128
IN_ODD = 3 * D_C + 2 * D_D

D_FF_DENSE = 2816
N_EXPERTS = 8
D_FF_EXPERT = 3584

ALPHA = (2 * DEPTH) ** 0.25
LN_EPS = 1e-5

SUBLANES = 8
LANES = 128
VMEM_LIMIT_BYTES = 56 * 1024 * 1024

TS_MIX = 512
CONV_ROWS = 32
HALO_A = 32
HALO_B = 16
HALO_C = 8
TM_FFN = 512
FF_CHUNK = 512
TM_MOE = 1024
SUB_MOE = 512
TF_MOE = 512
TD_ROWS = 512
N_TILES_MOE = 2 * TOKENS // TM_MOE + N_EXPERTS - 1
ROWS_SORTED = N_TILES_MOE * TM_MOE


def _layer_norm(x, g, b):
    mu = jnp.mean(x, axis=-1, keepdims=True)
    xc = x - mu
    var = jnp.mean(xc * xc, axis=-1, keepdims=True)
    return xc * lax.rsqrt(var + LN_EPS) * g + b


def _silu(x):
    return x * jax.nn.sigmoid(x)


def _dot(a, b):
    return jnp.dot(a, b, preferred_element_type=F32)


def _even_mixer_kernel(x_ref, win_ref, cw_ref, cb_ref, nag_ref, nab_ref, pw_ref, ps_ref,
                       wout_ref, g1_ref, b1_ref, o_ref, exta_ref, sh_ref, extb_ref, mixin_ref):
    s = pl.program_id(1)
    ts = TS_MIX

    @pl.when(s == 0)
    def _():
        exta_ref[0:HALO_A, :] = jnp.zeros((HALO_A, D_A), F32)
        extb_ref[0:HALO_B, :] = jnp.zeros((HALO_B, D_B), F32)

    x = x_ref[...]
    h = _dot(x.astype(BF16), win_ref[...])
    exta_ref[HALO_A:HALO_A + ts, :] = h[:, :D_A] * jax.nn.sigmoid(h[:, D_A:2 * D_A])
    b_in = h[:, 2 * D_A:]
    extb_ref[HALO_B:HALO_B + ts, :] = b_in

    sh_len = ts + HALO_A - SUBLANES
    for r in range(1, SUBLANES):
        sh_ref[r - 1] = exta_ref[r:r + sh_len, :]

    def conv_chunk(c, carry):
        base = pl.multiple_of(c * CONV_ROWS, CONV_ROWS)
        acc = jnp.broadcast_to(cb_ref[...], (CONV_ROWS, D_A))
        for k in range(CONV_A_WIDTH):
            q, r = divmod(HALO_A - (CONV_A_WIDTH - 1) + k, SUBLANES)
            if r == 0:
                slab = exta_ref[pl.ds(base + SUBLANES * q, CONV_ROWS), :]
            else:
                slab = sh_ref[r - 1, pl.ds(base + SUBLANES * q, CONV_ROWS), :]
            acc = acc + cw_ref[k:k + 1, :] * slab
        y = _silu(_layer_norm(acc, nag_ref[...], nab_ref[...]))
        mixin_ref[pl.ds(base, CONV_ROWS), 0:D_A] = y.astype(BF16)
        return carry

    lax.fori_loop(0, ts // CONV_ROWS, conv_chunk, 0)

    row_pos = s * ts + lax.broadcasted_iota(jnp.int32, (ts, 1), 0)
    for g, win in enumerate(POOL_WINDOWS):
        lo, hi = g * POOL_GROUP_DIM, (g + 1) * POOL_GROUP_DIM
        wsum = extb_ref[HALO_B:HALO_B + ts, lo:hi]
        for i in range(1, win):
            wsum = wsum + extb_ref[HALO_B - i:HALO_B - i + ts, lo:hi]
        cnt = jnp.minimum(row_pos + 1, win).astype(F32)
        pooled = wsum / cnt - b_in[:, lo:hi]
        mixed = _dot(pooled.astype(BF16), pw_ref[g]) * ps_ref[:, lo:hi]
        mixin_ref[:, D_A + lo:D_A + hi] = mixed.astype(BF16)

    mix = _dot(mixin_ref[...], wout_ref[...])
    o_ref[...] = _layer_norm(ALPHA * x + mix, g1_ref[...], b1_ref[...])

    exta_ref[0:HALO_A, :] = exta_ref[ts:ts + HALO_A, :]
    extb_ref[0:HALO_B, :] = extb_ref[ts:ts + HALO_B, :]


def _const_spec(shape):
    return pl.BlockSpec(shape, lambda *_: (0,) * len(shape))


def _even_mixer(x, w_in, conv_w, conv_b, norm_g, norm_b, pool_w, pool_scale, w_out, ln_g, ln_b):
    ts = TS_MIX
    x_spec = pl.BlockSpec((None, ts, D_MODEL), lambda b, s: (b, s, 0))
    return pl.pallas_call(
        _even_mixer_kernel,
        grid=(BATCH, SEQ // ts),
        in_specs=[
            x_spec,
            _const_spec((D_MODEL, IN_EVEN)),
            _const_spec((CONV_A_WIDTH, D_A)),
            _const_spec((1, D_A)),
            _const_spec((1, D_A)),
            _const_spec((1, D_A)),
            _const_spec((len(POOL_WINDOWS), POOL_GROUP_DIM, POOL_GROUP_DIM)),
            _const_spec((1, D_B)),
            _const_spec((D_A + D_B, D_MODEL)),
            _const_spec((1, D_MODEL)),
            _const_spec((1, D_MODEL)),
        ],
        out_specs=x_spec,
        out_shape=jax.ShapeDtypeStruct((BATCH, SEQ, D_MODEL), F32),
        scratch_shapes=[
            pltpu.VMEM((HALO_A + ts, D_A), F32),
            pltpu.VMEM((SUBLANES - 1, ts + HALO_A - SUBLANES, D_A), F32),
            pltpu.VMEM((HALO_B + ts, D_B), F32),
            pltpu.VMEM((ts, D_A + D_B), BF16),
        ],
        compiler_params=pltpu.CompilerParams(
            dimension_semantics=("arbitrary", "arbitrary"), vmem_limit_bytes=VMEM_LIMIT_BYTES),
        name="even_mixer",
    )(x, w_in.astype(BF16), conv_w, conv_b[None], norm_g[None], norm_b[None],
      pool_w.astype(BF16), pool_scale[None], w_out.astype(BF16), ln_g[None], ln_b[None])


def _ff_chunks(total, chunk):
    bounds = list(range(0, total, chunk)) + [total]
    return list(zip(bounds[:-1], bounds[1:]))


def _dense_ffn_kernel(x_ref, wg_ref, wu_ref, wd_ref, g_ref, b_ref, o_ref, acc_ref):
    x = x_ref[...]
    xb = x.astype(BF16)
    for n, (lo, hi) in enumerate(_ff_chunks(D_FF_DENSE, FF_CHUNK)):
        a = _silu(_dot(xb, wg_ref[:, lo:hi])) * _dot(xb, wu_ref[:, lo:hi])
        part = _dot(a.astype(BF16), wd_ref[lo:hi, :])
        if n == 0:
            acc_ref[...] = part
        else:
            acc_ref[...] += part
    o_ref[...] = _layer_norm(ALPHA * x + acc_ref[...], g_ref[...], b_ref[...])


def _dense_ffn(x, w_gate, w_up, w_down, ln_g, ln_b):
    tm = TM_FFN
    row_spec = pl.BlockSpec((tm, D_MODEL), lambda i: (i, 0))
    resident = pl.BlockSpec(memory_space=pltpu.VMEM)
    return pl.pallas_call(
        _dense_ffn_kernel,
        grid=(TOKENS // tm,),
        in_specs=[row_spec, resident, resident, resident,
                  _const_spec((1, D_MODEL)), _const_spec((1, D_MODEL))],
        out_specs=row_spec,
        out_shape=jax.ShapeDtypeStruct((TOKENS, D_MODEL), F32),
        scratch_shapes=[pltpu.VMEM((tm, D_MODEL), F32)],
        compiler_params=pltpu.CompilerParams(
            dimension_semantics=("arbitrary",), vmem_limit_bytes=VMEM_LIMIT_BYTES),
        name="dense_ffn",
    )(x, w_gate.astype(BF16), w_up.astype(BF16), w_down.astype(BF16), ln_g[None], ln_b[None])


META_E1, META_E2, META_W1, META_W2, META_R1, META_R2 = range(6)


def _odd_mixer_kernel(x_ref, win_ref, ccw_ref, sng_ref, snb_ref, sw_ref, sb_ref, wout_ref,
                      g1_ref, b1_ref, rhi_ref, rlo_ref, o_ref, meta_ref, cnt_ref,
                      extc_ref, mixin_ref, count_ref):
    b = pl.program_id(0)
    s = pl.program_id(1)
    ts = TS_MIX

    @pl.when(s == 0)
    def _():
        extc_ref[0:HALO_C, :] = jnp.zeros((HALO_C, D_C), F32)

    @pl.when(jnp.logical_and(b == 0, s == 0))
    def _():
        count_ref[...] = jnp.zeros((1, LANES), F32)

    x = x_ref[...]
    h = _dot(x.astype(BF16), win_ref[...])

    extc_ref[HALO_C:HALO_C + ts, :] = h[:, D_C:2 * D_C] * h[:, 2 * D_C:3 * D_C]
    conv = ccw_ref[0:1, :] * extc_ref[HALO_C - 2:HALO_C - 2 + ts, :]
    conv = conv + ccw_ref[1:2, :] * extc_ref[HALO_C - 1:HALO_C - 1 + ts, :]
    conv = conv + ccw_ref[2:3, :] * extc_ref[HALO_C:HALO_C + ts, :]
    mixin_ref[:, 0:D_C] = (h[:, 0:D_C] * conv).astype(BF16)

    z = jax.nn.gelu(h[:, 3 * D_C:])
    u = z[:, :D_D]
    v = _layer_norm(z[:, D_D:], sng_ref[...], snb_ref[...]).astype(BF16)
    tri = (lax.broadcasted_iota(jnp.int32, (SGU_BLOCK, SGU_BLOCK), 0)
           >= lax.broadcasted_iota(jnp.int32, (SGU_BLOCK, SGU_BLOCK), 1))
    for hd in range(SGU_HEADS):
        lo, hi = hd * SGU_HEAD_DIM, (hd + 1) * SGU_HEAD_DIM
        w = jnp.where(tri, sw_ref[hd], 0.0).astype(BF16)
        bias = sb_ref[:, hd:hd + 1]
        for blk in range(ts // SGU_BLOCK):
            r0, r1 = blk * SGU_BLOCK, (blk + 1) * SGU_BLOCK
            mixed = _dot(w, v[r0:r1, lo:hi]) + bias
            mixin_ref[r0:r1, D_C + lo:D_C + hi] = (u[r0:r1, lo:hi] * mixed).astype(BF16)

    mix = _dot(mixin_ref[...], wout_ref[...])
    x1 = _layer_norm(ALPHA * x + mix, g1_ref[...], b1_ref[...])
    o_ref[...] = x1

    extc_ref[0:HALO_C, :] = extc_ref[ts:ts + HALO_C, :]

    x_hi = x1.astype(BF16)
    x_lo = (x1 - x_hi.astype(F32)).astype(BF16)
    logits = _dot(x_hi, rhi_ref[...]) + _dot(x_lo, rhi_ref[...]) + _dot(x_hi, rlo_ref[...])

    lane = lax.broadcasted_iota(jnp.int32, (ts, LANES), 1).astype(F32)
    neg_inf = jnp.float32(-jnp.inf)
    lg = jnp.where(lane < N_EXPERTS, logits, neg_inf)
    m1 = jnp.max(lg, axis=1, keepdims=True)
    e1 = jnp.min(jnp.where(lg == m1, lane, float(LANES)), axis=1, keepdims=True)
    lg2 = jnp.where(lane == e1, neg_inf, lg)
    m2 = jnp.max(lg2, axis=1, keepdims=True)
    e2 = jnp.min(jnp.where(lg2 == m2, lane, float(LANES)), axis=1, keepdims=True)
    p2 = jnp.exp(m2 - m1)
    den = 1.0 + p2
    w1 = 1.0 / den
    w2 = p2 / den

    onehot = jnp.logical_or(lane == e1, lane == e2)
    before = (lax.broadcasted_iota(jnp.int32, (ts, ts), 0)
              > lax.broadcasted_iota(jnp.int32, (ts, ts), 1))
    prefix = _dot(jnp.where(before, 1.0, 0.0).astype(BF16), jnp.where(onehot, 1.0, 0.0).astype(BF16))
    rank = prefix + count_ref[...]
    r1 = jnp.sum(jnp.where(lane == e1, rank, 0.0), axis=1, keepdims=True)
    r2 = jnp.sum(jnp.where(lane == e2, rank, 0.0), axis=1, keepdims=True)
    count_ref[...] += jnp.sum(jnp.where(onehot, 1.0, 0.0), axis=0, keepdims=True)

    meta = jnp.zeros((ts, LANES), F32)
    for col, val in ((META_E1, e1), (META_E2, e2), (META_W1, w1), (META_W2, w2),
                     (META_R1, r1), (META_R2, r2)):
        meta = jnp.where(lane == col, val, meta)
    meta_ref[...] = meta
    cnt_ref[...] = jnp.broadcast_to(count_ref[...], (SUBLANES, LANES))


def _odd_mixer(x, w_in, conv_w, sgu_norm_g, sgu_norm_b, sgu_w, sgu_b, w_out, ln_g, ln_b, router):
    ts = TS_MIX
    x_spec = pl.BlockSpec((None, ts, D_MODEL), lambda b, s: (b, s, 0))
    router_pad = jnp.zeros((D_MODEL, LANES), F32).at[:, :N_EXPERTS].set(router)
    router_hi = router_pad.astype(BF16)
    router_lo = (router_pad - router_hi.astype(F32)).astype(BF16)
    return pl.pallas_call(
        _odd_mixer_kernel,
        grid=(BATCH, SEQ // ts),
        in_specs=[
            x_spec,
            _const_spec((D_MODEL, IN_ODD)),
            _const_spec((CONV_C_WIDTH, D_C)),
            _const_spec((1, D_D)),
            _const_spec((1, D_D)),
            _const_spec((SGU_HEADS, SGU_BLOCK, SGU_BLOCK)),
            _const_spec((SGU_BLOCK, SGU_HEADS)),
            _const_spec((D_C + D_D, D_MODEL)),
            _const_spec((1, D_MODEL)),
            _const_spec((1, D_MODEL)),
            _const_spec((D_MODEL, LANES)),
            _const_spec((D_MODEL, LANES)),
        ],
        out_specs=[
            x_spec,
            pl.BlockSpec((ts, LANES), lambda b, s: (b * (SEQ // ts) + s, 0)),
            _const_spec((SUBLANES, LANES)),
        ],
        out_shape=[
            jax.ShapeDtypeStruct((BATCH, SEQ, D_MODEL), F32),
            jax.ShapeDtypeStruct((TOKENS, LANES), F32),
            jax.ShapeDtypeStruct((SUBLANES, LANES), F32),
        ],
        scratch_shapes=[
            pltpu.VMEM((HALO_C + ts, D_C), F32),
            pltpu.VMEM((ts, D_C + D_D), BF16),
            pltpu.VMEM((1, LANES), F32),
        ],
        compiler_params=pltpu.CompilerParams(
            dimension_semantics=("arbitrary", "arbitrary"), vmem_limit_bytes=VMEM_LIMIT_BYTES),
        name="odd_mixer",
    )(x, w_in.astype(BF16), conv_w, sgu_norm_g[None], sgu_norm_b[None], sgu_w, sgu_b.T,
      w_out.astype(BF16), ln_g[None], ln_b[None], router_hi, router_lo)


def _row_copy(src_ref, src_row, dst_ref, dst_row, sem):
    return pltpu.make_async_copy(src_ref.at[pl.ds(src_row, 1)], dst_ref.at[pl.ds(dst_row, 1)], sem)


def _dispatch_kernel(pos_ref, pad_tile_ref, x_ref, xs_ref, zero_ref, zero_sem, row_sem):
    i = pl.program_id(0)

    @pl.when(i == 0)
    def _():
        zero_ref[...] = jnp.zeros(zero_ref.shape, F32)

        def clear_tile(t):
            start = pl.multiple_of(t * TM_MOE, TM_MOE)
            cp = pltpu.make_async_copy(zero_ref, xs_ref.at[pl.ds(start, TM_MOE)], zero_sem)
            cp.start()
            cp.wait()

        for e in range(N_EXPERTS):
            @pl.when(pad_tile_ref[e] >= 0)
            def _():
                clear_tile(pad_tile_ref[e])

        def clear_tail(t, carry):
            clear_tile(t)
            return carry

        lax.fori_loop(pad_tile_ref[N_EXPERTS], N_TILES_MOE, clear_tail, 0)

    base = i * TD_ROWS

    def issue(r, carry):
        _row_copy(x_ref, r, xs_ref, pos_ref[base + r], row_sem).start()
        _row_copy(x_ref, r, xs_ref, pos_ref[TOKENS + base + r], row_sem).start()
        return carry

    lax.fori_loop(0, TD_ROWS, issue, 0)

    def drain(r, carry):
        _row_copy(x_ref, r, xs_ref, pos_ref[base + r], row_sem).wait()
        _row_copy(x_ref, r, xs_ref, pos_ref[TOKENS + base + r], row_sem).wait()
        return carry

    lax.fori_loop(0, TD_ROWS, drain, 0)


def _dispatch(x, pos, pad_tile):
    return pl.pallas_call(
        _dispatch_kernel,
        grid_spec=pltpu.PrefetchScalarGridSpec(
            num_scalar_prefetch=2,
            grid=(TOKENS // TD_ROWS,),
            in_specs=[pl.BlockSpec((TD_ROWS, D_MODEL), lambda i, *_: (i, 0))],
            out_specs=pl.BlockSpec(memory_space=pl.ANY),
            scratch_shapes=[
                pltpu.VMEM((TM_MOE, D_MODEL), F32),
                pltpu.SemaphoreType.DMA(()),
                pltpu.SemaphoreType.DMA(()),
            ],
        ),
        out_shape=jax.ShapeDtypeStruct((ROWS_SORTED, D_MODEL), F32),
        compiler_params=pltpu.CompilerParams(
            dimension_semantics=("arbitrary",), vmem_limit_bytes=VMEM_LIMIT_BYTES),
        name="moe_dispatch",
    )(pos, pad_tile, x)


def _grouped_ffn_kernel(tile_expert_ref, tile_rows_ref, n_active_ref, xs_ref, wg_ref, wu_ref, wd_ref,
                        ys_ref, xb_ref):
    i = pl.program_id(0)
    j = pl.program_id(1)

    @pl.when(jnp.logical_and(i >= n_active_ref[0], j == 0))
    def _():
        ys_ref[...] = jnp.zeros((TM_MOE, D_MODEL), F32)

    @pl.when(i < n_active_ref[0])
    def _():
        @pl.when(j == 0)
        def _():
            xb_ref[...] = xs_ref[...].astype(BF16)

        wg = wg_ref[...].astype(BF16)
        wu = wu_ref[...].astype(BF16)
        wd = wd_ref[...].astype(BF16)
        for sb in range(TM_MOE // SUB_MOE):
            rows = slice(sb * SUB_MOE, (sb + 1) * SUB_MOE)

            @pl.when(tile_rows_ref[i] > sb * SUB_MOE)
            def _():
                xb = xb_ref[rows, :]
                a = _silu(_dot(xb, wg)) * _dot(xb, wu)
                part = _dot(a.astype(BF16), wd)

                @pl.when(j == 0)
                def _():
                    ys_ref[rows, :] = part

                @pl.when(j > 0)
                def _():
                    ys_ref[rows, :] += part

            @pl.when(jnp.logical_and(tile_rows_ref[i] <= sb * SUB_MOE, j == 0))
            def _():
                ys_ref[rows, :] = jnp.zeros((SUB_MOE, D_MODEL), F32)


def _grouped_ffn(xs, w_gate, w_up, w_down, tile_expert, tile_rows, n_active):
    nj = D_FF_EXPERT // TF_MOE

    def row_map(i, j, te, tr, na):
        return (jnp.minimum(i, na[0] - 1), 0)

    def frozen_j(i, j, na):
        return jnp.where(i < na[0], j, nj - 1)

    def up_map(i, j, te, tr, na):
        return (te[i], 0, frozen_j(i, j, na))

    def down_map(i, j, te, tr, na):
        return (te[i], frozen_j(i, j, na), 0)

    return pl.pallas_call(
        _grouped_ffn_kernel,
        grid_spec=pltpu.PrefetchScalarGridSpec(
            num_scalar_prefetch=3,
            grid=(N_TILES_MOE, nj),
            in_specs=[
                pl.BlockSpec((TM_MOE, D_MODEL), row_map),
                pl.BlockSpec((None, D_MODEL, TF_MOE), up_map),
                pl.BlockSpec((None, D_MODEL, TF_MOE), up_map),
                pl.BlockSpec((None, TF_MOE, D_MODEL), down_map),
            ],
            out_specs=pl.BlockSpec((TM_MOE, D_MODEL), lambda i, j, *_: (i, 0)),
            scratch_shapes=[pltpu.VMEM((TM_MOE, D_MODEL), BF16)],
        ),
        out_shape=jax.ShapeDtypeStruct((ROWS_SORTED, D_MODEL), F32),
        compiler_params=pltpu.CompilerParams(
            dimension_semantics=("arbitrary", "arbitrary"), vmem_limit_bytes=VMEM_LIMIT_BYTES),
        name="moe_grouped_ffn",
    )(tile_expert, tile_rows, n_active, xs, w_gate, w_up, w_down)


def _combine_kernel(pos_ref, x_ref, meta_ref, ys_ref, g_ref, b_ref, o_ref, y1_ref, y2_ref, sem):
    i = pl.program_id(0)
    base = i * TD_ROWS

    def issue(r, carry):
        _row_copy(ys_ref, pos_ref[base + r], y1_ref, r, sem).start()
        _row_copy(ys_ref, pos_ref[TOKENS + base + r], y2_ref, r, sem).start()
        return carry

    lax.fori_loop(0, TD_ROWS, issue, 0)

    def drain(r, carry):
        _row_copy(ys_ref, pos_ref[base + r], y1_ref, r, sem).wait()
        _row_copy(ys_ref, pos_ref[TOKENS + base + r], y2_ref, r, sem).wait()
        return carry

    lax.fori_loop(0, TD_ROWS, drain, 0)

    meta = meta_ref[...]
    w1 = meta[:, META_W1:META_W1 + 1]
    w2 = meta[:, META_W2:META_W2 + 1]
    moe = w1 * y1_ref[...] + w2 * y2_ref[...]
    o_ref[...] = _layer_norm(ALPHA * x_ref[...] + moe, g_ref[...], b_ref[...])


def _combine(x, meta, ys, pos, ln_g, ln_b):
    row_spec = pl.BlockSpec((TD_ROWS, D_MODEL), lambda i, *_: (i, 0))
    return pl.pallas_call(
        _combine_kernel,
        grid_spec=pltpu.PrefetchScalarGridSpec(
            num_scalar_prefetch=1,
            grid=(TOKENS // TD_ROWS,),
            in_specs=[
                row_spec,
                pl.BlockSpec((TD_ROWS, LANES), lambda i, *_: (i, 0)),
                pl.BlockSpec(memory_space=pl.ANY),
                pl.BlockSpec((1, D_MODEL), lambda i, *_: (0, 0)),
                pl.BlockSpec((1, D_MODEL), lambda i, *_: (0, 0)),
            ],
            out_specs=row_spec,
            scratch_shapes=[
                pltpu.VMEM((TD_ROWS, D_MODEL), F32),
                pltpu.VMEM((TD_ROWS, D_MODEL), F32),
                pltpu.SemaphoreType.DMA(()),
            ],
        ),
        out_shape=jax.ShapeDtypeStruct((TOKENS, D_MODEL), F32),
        compiler_params=pltpu.CompilerParams(
            dimension_semantics=("arbitrary",), vmem_limit_bytes=VMEM_LIMIT_BYTES),
        name="moe_combine",
    )(pos, x, meta, ys, ln_g[None], ln_b[None])


def _routing_tables(meta, counts_f):
    counts = counts_f[0, :N_EXPERTS].astype(jnp.int32)
    tiles = (counts + TM_MOE - 1) // TM_MOE
    tile_end = jnp.cumsum(tiles)
    tile_start = tile_end - tiles
    offset = tile_start * TM_MOE
    e1 = meta[:, META_E1].astype(jnp.int32)
    e2 = meta[:, META_E2].astype(jnp.int32)
    pos1 = offset[e1] + meta[:, META_R1].astype(jnp.int32)
    pos2 = offset[e2] + meta[:, META_R2].astype(jnp.int32)
    pos = jnp.concatenate([pos1, pos2])
    n_active = tile_end[-1:]
    tile_id = jnp.minimum(jnp.arange(N_TILES_MOE, dtype=jnp.int32), n_active[0] - 1)
    tile_expert = jnp.sum(tile_id[:, None] >= tile_end[None, :], axis=1).astype(jnp.int32)
    tile_rows = jnp.clip(counts[tile_expert] - (tile_id - tile_start[tile_expert]) * TM_MOE, 0, TM_MOE)
    pad_tile = jnp.concatenate([jnp.where(tiles > 0, tile_end - 1, -1), n_active]).astype(jnp.int32)
    return pos, pad_tile, tile_expert, tile_rows.astype(jnp.int32), n_active.astype(jnp.int32)


def _even_layer(x, w_in, conv_a_w, conv_a_b, norm_a_g, norm_a_b, pool_w, pool_scale, w_out,
                ln1_g, ln1_b, ffn_w_gate, ffn_w_up, ffn_w_down, ln2_g, ln2_b):
    x = _even_mixer(x, w_in, conv_a_w, conv_a_b, norm_a_g, norm_a_b, pool_w, pool_scale, w_out,
                    ln1_g, ln1_b)
    x = _dense_ffn(x.reshape(TOKENS, D_MODEL), ffn_w_gate, ffn_w_up, ffn_w_down, ln2_g, ln2_b)
    return x.reshape(BATCH, SEQ, D_MODEL)


def _odd_layer(x, w_in, conv_c_w, sgu_norm_g, sgu_norm_b, sgu_w, sgu_b, w_out, ln1_g, ln1_b,
               router, moe_w_gate, moe_w_up, moe_w_down, ln2_g, ln2_b):
    x, meta, counts = _odd_mixer(x, w_in, conv_c_w, sgu_norm_g, sgu_norm_b, sgu_w, sgu_b, w_out,
                                 ln1_g, ln1_b, router)
    x = x.reshape(TOKENS, D_MODEL)
    pos, pad_tile, tile_expert, tile_rows, n_active = _routing_tables(meta, counts)
    xs = _dispatch(x, pos, pad_tile)
    ys = _grouped_ffn(xs, moe_w_gate, moe_w_up, moe_w_down, tile_expert, tile_rows, n_active)
    x = _combine(x, meta, ys, pos, ln2_g, ln2_b)
    return x.reshape(BATCH, SEQ, D_MODEL)


def kernel(x, even_w_in, even_conv_a_w, even_conv_a_b, even_norm_a_g, even_norm_a_b, even_pool_w, even_pool_scale, even_w_out, even_ln1_g, even_ln1_b, even_ffn_w_gate, even_ffn_w_up, even_ffn_w_down, even_ln2_g, even_ln2_b, odd_w_in, odd_conv_c_w, odd_sgu_norm_g, odd_sgu_norm_b, odd_sgu_w, odd_sgu_b, odd_w_out, odd_ln1_g, odd_ln1_b, odd_router, odd_moe_w_gate, odd_moe_w_up, odd_moe_w_down, odd_ln2_g, odd_ln2_b):
    for layer in range(DEPTH):
        i = layer // 2
        if layer % 2 == 0:
            x = _even_layer(x, even_w_in[i], even_conv_a_w[i], even_conv_a_b[i], even_norm_a_g[i],
                            even_norm_a_b[i], even_pool_w[i], even_pool_scale[i], even_w_out[i],
                            even_ln1_g[i], even_ln1_b[i], even_ffn_w_gate[i], even_ffn_w_up[i],
                            even_ffn_w_down[i], even_ln2_g[i], even_ln2_b[i])
        else:
            x = _odd_layer(x, odd_w_in[i], odd_conv_c_w[i], odd_sgu_norm_g[i], odd_sgu_norm_b[i],
                           odd_sgu_w[i], odd_sgu_b[i], odd_w_out[i], odd_ln1_g[i], odd_ln1_b[i],
                           odd_router[i], odd_moe_w_gate[i], odd_moe_w_up[i], odd_moe_w_down[i],
                           odd_ln2_g[i], odd_ln2_b[i])
    return x
```

```python
import functools

import jax
import jax.numpy as jnp
from jax import lax
from jax.experimental import pallas as pl
from jax.experimental.pallas import tpu as pltpu

F32 = jnp.float32
BF16 = jnp.bfloat16

D_MODEL = 1024
BATCH = 4
SEQ = 4096
TOKENS = BATCH * SEQ
DEPTH = 2

D_A = 512
CONV_A_WIDTH = 31
D_B = 512
POOL_WINDOWS = (2, 4, 8, 16)
POOL_GROUP_DIM = 128
IN_EVEN = 2 * D_A + D_B

D_C = 512
CONV_C_WIDTH = 3
D_D = 512
SGU_BLOCK = 128
SGU_HEADS = 4
SGU_HEAD_DIM = 128
IN_ODD = 3 * D_C + 2 * D_D

D_FF_DENSE = 2816
N_EXPERTS = 8
D_FF_EXPERT = 3584

ALPHA = (2 * DEPTH) ** 0.25
LN_EPS = 1e-5

SUBLANES = 8
LANES = 128
VMEM_LIMIT_BYTES = 56 * 1024 * 1024

TS_MIX = 512
CONV_ROWS = 32
HALO_A = 32
HALO_B = 16
HALO_C = 8
TM_FFN = 512
FF_CHUNK = 512
TM_MOE = 1024
SUB_MOE = 512
TF_MOE = 512
N_TOKEN_TILES = TOKENS // TS_MIX
N_SEGMENTS = N_TOKEN_TILES * N_EXPERTS
SLOT_ROWS = -(-(2 * TS_MIX + N_EXPERTS * (SUBLANES - 1)) // 16) * 16
MAX_SORTED_ROWS = 2 * TOKENS + N_SEGMENTS * (SUBLANES - 1)
N_TILES_MOE = (MAX_SORTED_ROWS + N_EXPERTS * (TM_MOE - 1)) // TM_MOE
ROWS_SORTED = N_TILES_MOE * TM_MOE


def _layer_norm(x, g, b):
    mu = jnp.mean(x, axis=-1, keepdims=True)
    xc = x - mu
    var = jnp.mean(xc * xc, axis=-1, keepdims=True)
    return xc * lax.rsqrt(var + LN_EPS) * g + b


def _silu(x):
    return x * jax.nn.sigmoid(x)


def _dot(a, b):
    return jnp.dot(a, b, preferred_element_type=F32)


def _even_mixer_kernel(x_ref, win_ref, cw_ref, cb_ref, nag_ref, nab_ref, pw_ref, ps_ref,
                       wout_ref, g1_ref, b1_ref, o_ref, exta_ref, sh_ref, extb_ref, mixin_ref):
    s = pl.program_id(1)
    ts = TS_MIX

    @pl.when(s == 0)
    def _():
        exta_ref[0:HALO_A, :] = jnp.zeros((HALO_A, D_A), F32)
        extb_ref[0:HALO_B, :] = jnp.zeros((HALO_B, D_B), F32)

    x = x_ref[...]
    h = _dot(x.astype(BF16), win_ref[...])
    exta_ref[HALO_A:HALO_A + ts, :] = h[:, :D_A] * jax.nn.sigmoid(h[:, D_A:2 * D_A])
    b_in = h[:, 2 * D_A:]
    extb_ref[HALO_B:HALO_B + ts, :] = b_in

    sh_len = ts + HALO_A - SUBLANES
    for r in range(1, SUBLANES):
        sh_ref[r - 1] = exta_ref[r:r + sh_len, :]

    def conv_chunk(c, carry):
        base = pl.multiple_of(c * CONV_ROWS, CONV_ROWS)
        acc = jnp.broadcast_to(cb_ref[...], (CONV_ROWS, D_A))
        for k in range(CONV_A_WIDTH):
            q, r = divmod(HALO_A - (CONV_A_WIDTH - 1) + k, SUBLANES)
            if r == 0:
                slab = exta_ref[pl.ds(base + SUBLANES * q, CONV_ROWS), :]
            else:
                slab = sh_ref[r - 1, pl.ds(base + SUBLANES * q, CONV_ROWS), :]
            acc = acc + cw_ref[k:k + 1, :] * slab
        y = _silu(_layer_norm(acc, nag_ref[...], nab_ref[...]))
        mixin_ref[pl.ds(base, CONV_ROWS), 0:D_A] = y.astype(BF16)
        return carry

    lax.fori_loop(0, ts // CONV_ROWS, conv_chunk, 0)

    row_pos = s * ts + lax.broadcasted_iota(jnp.int32, (ts, 1), 0)
    for g, win in enumerate(POOL_WINDOWS):
        lo, hi = g * POOL_GROUP_DIM, (g + 1) * POOL_GROUP_DIM
        wsum = extb_ref[HALO_B:HALO_B + ts, lo:hi]
        for i in range(1, win):
            wsum = wsum + extb_ref[HALO_B - i:HALO_B - i + ts, lo:hi]
        cnt = jnp.minimum(row_pos + 1, win).astype(F32)
        pooled = wsum / cnt - b_in[:, lo:hi]
        mixed = _dot(pooled.astype(BF16), pw_ref[g]) * ps_ref[:, lo:hi]
        mixin_ref[:, D_A + lo:D_A + hi] = mixed.astype(BF16)

    mix = _dot(mixin_ref[...], wout_ref[...])
    o_ref[...] = _layer_norm(ALPHA * x + mix, g1_ref[...], b1_ref[...])

    exta_ref[0:HALO_A, :] = exta_ref[ts:ts + HALO_A, :]
    extb_ref[0:HALO_B, :] = extb_ref[ts:ts + HALO_B, :]


def _const_spec(shape):
    return pl.BlockSpec(shape, lambda *_: (0,) * len(shape))


def _even_mixer(x, w_in, conv_w, conv_b, norm_g, norm_b, pool_w, pool_scale, w_out, ln_g, ln_b):
    ts = TS_MIX
    x_spec = pl.BlockSpec((None, ts, D_MODEL), lambda b, s: (b, s, 0))
    return pl.pallas_call(
        _even_mixer_kernel,
        grid=(BATCH, SEQ // ts),
        in_specs=[
            x_spec,
            _const_spec((D_MODEL, IN_EVEN)),
            _const_spec((CONV_A_WIDTH, D_A)),
            _const_spec((1, D_A)),
            _const_spec((1, D_A)),
            _const_spec((1, D_A)),
            _const_spec((len(POOL_WINDOWS), POOL_GROUP_DIM, POOL_GROUP_DIM)),
            _const_spec((1, D_B)),
            _const_spec((D_A + D_B, D_MODEL)),
            _const_spec((1, D_MODEL)),
            _const_spec((1, D_MODEL)),
        ],
        out_specs=x_spec,
        out_shape=jax.ShapeDtypeStruct((BATCH, SEQ, D_MODEL), F32),
        scratch_shapes=[
            pltpu.VMEM((HALO_A + ts, D_A), F32),
            pltpu.VMEM((SUBLANES - 1, ts + HALO_A - SUBLANES, D_A), F32),
            pltpu.VMEM((HALO_B + ts, D_B), F32),
            pltpu.VMEM((ts, D_A + D_B), BF16),
        ],
        compiler_params=pltpu.CompilerParams(
            dimension_semantics=("arbitrary", "arbitrary"), vmem_limit_bytes=VMEM_LIMIT_BYTES),
        name="even_mixer",
    )(x, w_in.astype(BF16), conv_w, conv_b[None], norm_g[None], norm_b[None],
      pool_w.astype(BF16), pool_scale[None], w_out.astype(BF16), ln_g[None], ln_b[None])


def _ff_chunks(total, chunk):
    bounds = list(range(0, total, chunk)) + [total]
    return list(zip(bounds[:-1], bounds[1:]))


def _dense_ffn_kernel(x_ref, wg_ref, wu_ref, wd_ref, g_ref, b_ref, o_ref, acc_ref):
    x = x_ref[...]
    xb = x.astype(BF16)
    for n, (lo, hi) in enumerate(_ff_chunks(D_FF_DENSE, FF_CHUNK)):
        a = _silu(_dot(xb, wg_ref[:, lo:hi])) * _dot(xb, wu_ref[:, lo:hi])
        part = _dot(a.astype(BF16), wd_ref[lo:hi, :])
        if n == 0:
            acc_ref[...] = part
        else:
            acc_ref[...] += part
    o_ref[...] = _layer_norm(ALPHA * x + acc_ref[...], g_ref[...], b_ref[...])


def _dense_ffn(x, w_gate, w_up, w_down, ln_g, ln_b):
    tm = TM_FFN
    row_spec = pl.BlockSpec((tm, D_MODEL), lambda i: (i, 0))
    resident = pl.BlockSpec(memory_space=pltpu.VMEM)
    return pl.pallas_call(
        _dense_ffn_kernel,
        grid=(TOKENS // tm,),
        in_specs=[row_spec, resident, resident, resident,
                  _const_spec((1, D_MODEL)), _const_spec((1, D_MODEL))],
        out_specs=row_spec,
        out_shape=jax.ShapeDtypeStruct((TOKENS, D_MODEL), F32),
        scratch_shapes=[pltpu.VMEM((tm, D_MODEL), F32)],
        compiler_params=pltpu.CompilerParams(
            dimension_semantics=("arbitrary",), vmem_limit_bytes=VMEM_LIMIT_BYTES),
        name="dense_ffn",
    )(x, w_gate.astype(BF16), w_up.astype(BF16), w_down.astype(BF16), ln_g[None], ln_b[None])


META_E1, META_E2, META_W1, META_W2, META_P1, META_P2 = range(6)


def _odd_mixer_kernel(x_ref, win_ref, ccw_ref, sng_ref, snb_ref, sw_ref, sb_ref, wout_ref,
                      g1_ref, b1_ref, rhi_ref, rlo_ref, o_ref, meta_ref, cnt_ref,
                      extc_ref, mixin_ref):
    s = pl.program_id(1)
    ts = TS_MIX

    @pl.when(s == 0)
    def _():
        extc_ref[0:HALO_C, :] = jnp.zeros((HALO_C, D_C), F32)

    x = x_ref[...]
    h = _dot(x.astype(BF16), win_ref[...])

    extc_ref[HALO_C:HALO_C + ts, :] = h[:, D_C:2 * D_C] * h[:, 2 * D_C:3 * D_C]
    conv = ccw_ref[0:1, :] * extc_ref[HALO_C - 2:HALO_C - 2 + ts, :]
    conv = conv + ccw_ref[1:2, :] * extc_ref[HALO_C - 1:HALO_C - 1 + ts, :]
    conv = conv + ccw_ref[2:3, :] * extc_ref[HALO_C:HALO_C + ts, :]
    mixin_ref[:, 0:D_C] = (h[:, 0:D_C] * conv).astype(BF16)

    z = jax.nn.gelu(h[:, 3 * D_C:])
    u = z[:, :D_D]
    v = _layer_norm(z[:, D_D:], sng_ref[...], snb_ref[...]).astype(BF16)
    tri = (lax.broadcasted_iota(jnp.int32, (SGU_BLOCK, SGU_BLOCK), 0)
           >= lax.broadcasted_iota(jnp.int32, (SGU_BLOCK, SGU_BLOCK), 1))
    for hd in range(SGU_HEADS):
        lo, hi = hd * SGU_HEAD_DIM, (hd + 1) * SGU_HEAD_DIM
        w = jnp.where(tri, sw_ref[hd], 0.0).astype(BF16)
        bias = sb_ref[:, hd:hd + 1]
        for blk in range(ts // SGU_BLOCK):
            r0, r1 = blk * SGU_BLOCK, (blk + 1) * SGU_BLOCK
            mixed = _dot(w, v[r0:r1, lo:hi]) + bias
            mixin_ref[r0:r1, D_C + lo:D_C + hi] = (u[r0:r1, lo:hi] * mixed).astype(BF16)

    mix = _dot(mixin_ref[...], wout_ref[...])
    x1 = _layer_norm(ALPHA * x + mix, g1_ref[...], b1_ref[...])
    o_ref[...] = x1

    extc_ref[0:HALO_C, :] = extc_ref[ts:ts + HALO_C, :]

    x_hi = x1.astype(BF16)
    x_lo = (x1 - x_hi.astype(F32)).astype(BF16)
    logits = _dot(x_hi, rhi_ref[...]) + _dot(x_lo, rhi_ref[...]) + _dot(x_hi, rlo_ref[...])

    lane = lax.broadcasted_iota(jnp.int32, (ts, LANES), 1).astype(F32)
    neg_inf = jnp.float32(-jnp.inf)
    lg = jnp.where(lane < N_EXPERTS, logits, neg_inf)
    m1 = jnp.max(lg, axis=1, keepdims=True)
    e1 = jnp.min(jnp.where(lg == m1, lane, float(LANES)), axis=1, keepdims=True)
    lg2 = jnp.where(lane == e1, neg_inf, lg)
    m2 = jnp.max(lg2, axis=1, keepdims=True)
    e2 = jnp.min(jnp.where(lg2 == m2, lane, float(LANES)), axis=1, keepdims=True)
    ex = jnp.exp(m2 - m1)
    den = 1.0 + ex
    w1 = 1.0 / den
    w2 = ex / den

    onehot = jnp.where(jnp.logical_or(lane == e1, lane == e2), 1.0, 0.0)
    onehot_b = onehot.astype(BF16)
    before = (lax.broadcasted_iota(jnp.int32, (ts, ts), 0)
              > lax.broadcasted_iota(jnp.int32, (ts, ts), 1))
    lower = (lax.broadcasted_iota(jnp.int32, (LANES, LANES), 0)
             < lax.broadcasted_iota(jnp.int32, (LANES, LANES), 1))
    earlier = _dot(jnp.where(before, 1.0, 0.0).astype(BF16), onehot_b)
    run = jnp.sum(onehot, axis=0, keepdims=True)
    run = jnp.floor((run + (SUBLANES - 1)) * (1.0 / SUBLANES)) * SUBLANES
    run_rows = jnp.broadcast_to(run, (SUBLANES, LANES)).astype(BF16)
    lower_rows = _dot(run_rows, jnp.where(lower, 1.0, 0.0).astype(BF16))[0:1, :]
    place = earlier + lower_rows
    p1 = jnp.sum(jnp.where(lane == e1, place, 0.0), axis=1, keepdims=True)
    p2 = jnp.sum(jnp.where(lane == e2, place, 0.0), axis=1, keepdims=True)

    meta = jnp.zeros((ts, LANES), F32)
    for col, val in ((META_E1, e1), (META_E2, e2), (META_W1, w1), (META_W2, w2),
                     (META_P1, p1), (META_P2, p2)):
        meta = jnp.where(lane == col, val, meta)
    meta_ref[...] = meta
    cnt_ref[...] = jnp.broadcast_to(run, (SUBLANES, LANES))


def _odd_mixer(x, w_in, conv_w, sgu_norm_g, sgu_norm_b, sgu_w, sgu_b, w_out, ln_g, ln_b, router):
    ts = TS_MIX
    x_spec = pl.BlockSpec((None, ts, D_MODEL), lambda b, s: (b, s, 0))
    router_pad = jnp.zeros((D_MODEL, LANES), F32).at[:, :N_EXPERTS].set(router)
    router_hi = router_pad.astype(BF16)
    router_lo = (router_pad - router_hi.astype(F32)).astype(BF16)
    return pl.pallas_call(
        _odd_mixer_kernel,
        grid=(BATCH, SEQ // ts),
        in_specs=[
            x_spec,
            _const_spec((D_MODEL, IN_ODD)),
            _const_spec((CONV_C_WIDTH, D_C)),
            _const_spec((1, D_D)),
            _const_spec((1, D_D)),
            _const_spec((SGU_HEADS, SGU_BLOCK, SGU_BLOCK)),
            _const_spec((SGU_BLOCK, SGU_HEADS)),
            _const_spec((D_C + D_D, D_MODEL)),
            _const_spec((1, D_MODEL)),
            _const_spec((1, D_MODEL)),
            _const_spec((D_MODEL, LANES)),
            _const_spec((D_MODEL, LANES)),
        ],
        out_specs=[
            x_spec,
            pl.BlockSpec((ts, LANES), lambda b, s: (b * (SEQ // ts) + s, 0)),
            pl.BlockSpec((None, SUBLANES, LANES), lambda b, s: (b * (SEQ // ts) + s, 0, 0)),
        ],
        out_shape=[
            jax.ShapeDtypeStruct((BATCH, SEQ, D_MODEL), F32),
            jax.ShapeDtypeStruct((TOKENS, LANES), F32),
            jax.ShapeDtypeStruct((N_TOKEN_TILES, SUBLANES, LANES), F32),
        ],
        scratch_shapes=[
            pltpu.VMEM((HALO_C + ts, D_C), F32),
            pltpu.VMEM((ts, D_C + D_D), BF16),
        ],
        compiler_params=pltpu.CompilerParams(
            dimension_semantics=("arbitrary", "arbitrary"), vmem_limit_bytes=VMEM_LIMIT_BYTES),
        name="odd_mixer",
    )(x, w_in.astype(BF16), conv_w, sgu_norm_g[None], sgu_norm_b[None], sgu_w, sgu_b.T,
      w_out.astype(BF16), ln_g[None], ln_b[None], router_hi, router_lo)


SEGMENT_BITS = range(SUBLANES.bit_length() - 1, TS_MIX.bit_length())


def _for_each_segment_chunk(seg_ref, tile, visit):
    for e in range(N_EXPERTS):
        k = tile * N_EXPERTS + e
        n = seg_ref[k]
        local = seg_ref[N_SEGMENTS + k]
        sorted_row = seg_ref[2 * N_SEGMENTS + k]
        for bit in reversed(SEGMENT_BITS):
            @pl.when(((n >> bit) & 1) == 1)
            def _():
                done = (n >> (bit + 1)) << (bit + 1)
                visit(pl.multiple_of(local + done, SUBLANES), pl.multiple_of(sorted_row + done, SUBLANES),
                      1 << bit)


def _dispatch_kernel(seg_ref, pad_tile_ref, x_ref, meta_ref, xs_ref, zero_ref, perm_ref, zero_sem, perm_sem):
    i = pl.program_id(0)
    slot = i % 2

    def chunk_copy(which):
        def make(local, sorted_row, size):
            return pltpu.make_async_copy(perm_ref.at[which, pl.ds(local, size)],
                                         xs_ref.at[pl.ds(sorted_row, size)], perm_sem.at[which])
        return make

    def wait_tile(tile, which):
        _for_each_segment_chunk(seg_ref, tile, lambda *chunk: chunk_copy(which)(*chunk).wait())

    @pl.when(i == 0)
    def _():
        zero_ref[...] = jnp.zeros(zero_ref.shape, F32)

        def clear_tile(t):
            start = pl.multiple_of(t * TM_MOE, TM_MOE)
            cp = pltpu.make_async_copy(zero_ref, xs_ref.at[pl.ds(start, TM_MOE)], zero_sem)
            cp.start()
            cp.wait()

        for e in range(N_EXPERTS):
            @pl.when(pad_tile_ref[e] >= 0)
            def _():
                clear_tile(pad_tile_ref[e])

        def clear_tail(t, carry):
            clear_tile(t)
            return carry

        lax.fori_loop(pad_tile_ref[N_EXPERTS], N_TILES_MOE, clear_tail, 0)

    meta = meta_ref[...]
    col = lax.broadcasted_iota(jnp.int32, (TS_MIX, SLOT_ROWS), 1).astype(F32)
    hit = jnp.logical_or(col == meta[:, META_P1:META_P1 + 1], col == meta[:, META_P2:META_P2 + 1])
    onehot = jnp.where(hit, 1.0, 0.0).astype(BF16)
    permuted = lax.dot_general(onehot, x_ref[...].astype(BF16), (((0,), (0,)), ((), ())),
                               preferred_element_type=F32)

    @pl.when(i >= 2)
    def _():
        wait_tile(i - 2, slot)

    perm_ref[slot] = permuted
    _for_each_segment_chunk(seg_ref, i, lambda *chunk: chunk_copy(slot)(*chunk).start())

    @pl.when(i == N_TOKEN_TILES - 1)
    def _():
        wait_tile(i - 1, 1 - slot)
        wait_tile(i, slot)


def _dispatch(x, meta, seg, pad_tile):
    return pl.pallas_call(
        _dispatch_kernel,
        grid_spec=pltpu.PrefetchScalarGridSpec(
            num_scalar_prefetch=2,
            grid=(N_TOKEN_TILES,),
            in_specs=[pl.BlockSpec((TS_MIX, D_MODEL), lambda i, *_: (i, 0)),
                      pl.BlockSpec((TS_MIX, LANES), lambda i, *_: (i, 0))],
            out_specs=pl.BlockSpec(memory_space=pl.ANY),
            scratch_shapes=[
                pltpu.VMEM((TM_MOE, D_MODEL), F32),
                pltpu.VMEM((2, SLOT_ROWS, D_MODEL), F32),
                pltpu.SemaphoreType.DMA(()),
                pltpu.SemaphoreType.DMA((2,)),
            ],
        ),
        out_shape=jax.ShapeDtypeStruct((ROWS_SORTED, D_MODEL), F32),
        compiler_params=pltpu.CompilerParams(
            dimension_semantics=("arbitrary",), vmem_limit_bytes=VMEM_LIMIT_BYTES),
        name="moe_dispatch",
    )(seg, pad_tile, x, meta)


def _grouped_ffn_kernel(tile_expert_ref, tile_rows_ref, n_active_ref, xs_ref, wg_ref, wu_ref, wd_ref,
                        ys_ref, xb_ref):
    i = pl.program_id(0)
    j = pl.program_id(1)

    @pl.when(jnp.logical_and(i >= n_active_ref[0], j == 0))
    def _():
        ys_ref[...] = jnp.zeros((TM_MOE, D_MODEL), F32)

    @pl.when(i < n_active_ref[0])
    def _():
        @pl.when(j == 0)
        def _():
            xb_ref[...] = xs_ref[...].astype(BF16)

        wg = wg_ref[...].astype(BF16)
        wu = wu_ref[...].astype(BF16)
        wd = wd_ref[...].astype(BF16)
        for sb in range(TM_MOE // SUB_MOE):
            rows = slice(sb * SUB_MOE, (sb + 1) * SUB_MOE)

            @pl.when(tile_rows_ref[i] > sb * SUB_MOE)
            def _():
                xb = xb_ref[rows, :]
                a = _silu(_dot(xb, wg)) * _dot(xb, wu)
                part = _dot(a.astype(BF16), wd)

                @pl.when(j == 0)
                def _():
                    ys_ref[rows, :] = part

                @pl.when(j > 0)
                def _():
                    ys_ref[rows, :] += part

            @pl.when(jnp.logical_and(tile_rows_ref[i] <= sb * SUB_MOE, j == 0))
            def _():
                ys_ref[rows, :] = jnp.zeros((SUB_MOE, D_MODEL), F32)


def _grouped_ffn(xs, w_gate, w_up, w_down, tile_expert, tile_rows, n_active):
    nj = D_FF_EXPERT // TF_MOE

    def row_map(i, j, te, tr, na):
        return (jnp.minimum(i, na[0] - 1), 0)

    def frozen_j(i, j, na):
        return jnp.where(i < na[0], j, nj - 1)

    def up_map(i, j, te, tr, na):
        return (te[i], 0, frozen_j(i, j, na))

    def down_map(i, j, te, tr, na):
        return (te[i], frozen_j(i, j, na), 0)

    return pl.pallas_call(
        _grouped_ffn_kernel,
        grid_spec=pltpu.PrefetchScalarGridSpec(
            num_scalar_prefetch=3,
            grid=(N_TILES_MOE, nj),
            in_specs=[
                pl.BlockSpec((TM_MOE, D_MODEL), row_map),
                pl.BlockSpec((None, D_MODEL, TF_MOE), up_map),
                pl.BlockSpec((None, D_MODEL, TF_MOE), up_map),
                pl.BlockSpec((None, TF_MOE, D_MODEL), down_map),
            ],
            out_specs=pl.BlockSpec((TM_MOE, D_MODEL), lambda i, j, *_: (i, 0)),
            scratch_shapes=[pltpu.VMEM((TM_MOE, D_MODEL), BF16)],
        ),
        out_shape=jax.ShapeDtypeStruct((ROWS_SORTED, D_MODEL), F32),
        compiler_params=pltpu.CompilerParams(
            dimension_semantics=("arbitrary", "arbitrary"), vmem_limit_bytes=VMEM_LIMIT_BYTES),
        name="moe_grouped_ffn",
    )(tile_expert, tile_rows, n_active, xs, w_gate, w_up, w_down)


def _combine_kernel(seg_ref, x_ref, meta_ref, ys_ref, g_ref, b_ref, o_ref, win_ref, sem):
    i = pl.program_id(0)
    slot = i % 2

    def chunk_copy(which):
        def make(local, sorted_row, size):
            return pltpu.make_async_copy(ys_ref.at[pl.ds(sorted_row, size)],
                                         win_ref.at[which, pl.ds(local, size)], sem.at[which])
        return make

    def fetch(tile, which):
        _for_each_segment_chunk(seg_ref, tile, lambda *chunk: chunk_copy(which)(*chunk).start())

    @pl.when(i == 0)
    def _():
        win_ref[...] = jnp.zeros(win_ref.shape, F32)
        fetch(0, 0)

    @pl.when(i + 1 < N_TOKEN_TILES)
    def _():
        fetch(i + 1, 1 - slot)

    _for_each_segment_chunk(seg_ref, i, lambda *chunk: chunk_copy(slot)(*chunk).wait())

    meta = meta_ref[...]
    col = lax.broadcasted_iota(jnp.int32, (TS_MIX, SLOT_ROWS), 1).astype(F32)
    gate = (jnp.where(col == meta[:, META_P1:META_P1 + 1], meta[:, META_W1:META_W1 + 1], 0.0)
            + jnp.where(col == meta[:, META_P2:META_P2 + 1], meta[:, META_W2:META_W2 + 1], 0.0))
    moe = _dot(gate.astype(BF16), win_ref[slot].astype(BF16))
    o_ref[...] = _layer_norm(ALPHA * x_ref[...] + moe, g_ref[...], b_ref[...])


def _combine(x, meta, ys, seg, ln_g, ln_b):
    row_spec = pl.BlockSpec((TS_MIX, D_MODEL), lambda i, *_: (i, 0))
    return pl.pallas_call(
        _combine_kernel,
        grid_spec=pltpu.PrefetchScalarGridSpec(
            num_scalar_prefetch=1,
            grid=(N_TOKEN_TILES,),
            in_specs=[
                row_spec,
                pl.BlockSpec((TS_MIX, LANES), lambda i, *_: (i, 0)),
                pl.BlockSpec(memory_space=pl.ANY),
                pl.BlockSpec((1, D_MODEL), lambda i, *_: (0, 0)),
                pl.BlockSpec((1, D_MODEL), lambda i, *_: (0, 0)),
            ],
            out_specs=row_spec,
            scratch_shapes=[
                pltpu.VMEM((2, SLOT_ROWS, D_MODEL), F32),
                pltpu.SemaphoreType.DMA((2,)),
            ],
        ),
        out_shape=jax.ShapeDtypeStruct((TOKENS, D_MODEL), F32),
        compiler_params=pltpu.CompilerParams(
            dimension_semantics=("arbitrary",), vmem_limit_bytes=VMEM_LIMIT_BYTES),
        name="moe_combine",
    )(seg, x, meta, ys, ln_g[None], ln_b[None])


def _routing_tables(tile_counts):
    n = tile_counts[:, 0, :N_EXPERTS].astype(jnp.int32)
    counts = jnp.sum(n, axis=0)
    tiles = (counts + TM_MOE - 1) // TM_MOE
    tile_end = jnp.cumsum(tiles)
    tile_start = tile_end - tiles
    offset = tile_start * TM_MOE
    local_start = jnp.cumsum(n, axis=1) - n
    sorted_start = offset[None, :] + jnp.cumsum(n, axis=0) - n
    seg = jnp.concatenate([n.reshape(-1), local_start.reshape(-1), sorted_start.reshape(-1)])
    n_active = tile_end[-1:]
    tile_id = jnp.minimum(jnp.arange(N_TILES_MOE, dtype=jnp.int32), n_active[0] - 1)
    tile_expert = jnp.sum(tile_id[:, None] >= tile_end[None, :], axis=1).astype(jnp.int32)
    tile_rows = jnp.clip(counts[tile_expert] - (tile_id - tile_start[tile_expert]) * TM_MOE, 0, TM_MOE)
    pad_tile = jnp.concatenate([jnp.where(tiles > 0, tile_end - 1, -1), n_active]).astype(jnp.int32)
    return (seg.astype(jnp.int32), pad_tile, tile_expert, tile_rows.astype(jnp.int32),
            n_active.astype(jnp.int32))


def _even_layer(x, w_in, conv_a_w, conv_a_b, norm_a_g, norm_a_b, pool_w, pool_scale, w_out,
                ln1_g, ln1_b, ffn_w_gate, ffn_w_up, ffn_w_down, ln2_g, ln2_b):
    x = _even_mixer(x, w_in, conv_a_w, conv_a_b, norm_a_g, norm_a_b, pool_w, pool_scale, w_out,
                    ln1_g, ln1_b)
    x = _dense_ffn(x.reshape(TOKENS, D_MODEL), ffn_w_gate, ffn_w_up, ffn_w_down, ln2_g, ln2_b)
    return x.reshape(BATCH, SEQ, D_MODEL)


def _odd_layer(x, w_in, conv_c_w, sgu_norm_g, sgu_norm_b, sgu_w, sgu_b, w_out, ln1_g, ln1_b,
               router, moe_w_gate, moe_w_up, moe_w_down, ln2_g, ln2_b):
    x, meta, tile_counts = _odd_mixer(x, w_in, conv_c_w, sgu_norm_g, sgu_norm_b, sgu_w, sgu_b, w_out,
                                      ln1_g, ln1_b, router)
    x = x.reshape(TOKENS, D_MODEL)
    seg, pad_tile, tile_expert, tile_rows, n_active = _routing_tables(tile_counts)
    xs = _dispatch(x, meta, seg, pad_tile)
    ys = _grouped_ffn(xs, moe_w_gate, moe_w_up, moe_w_down, tile_expert, tile_rows, n_active)
    x = _combine(x, meta, ys, seg, ln2_g, ln2_b)
    return x.reshape(BATCH, SEQ, D_MODEL)


def kernel(x, even_w_in, even_conv_a_w, even_conv_a_b, even_norm_a_g, even_norm_a_b, even_pool_w, even_pool_scale, even_w_out, even_ln1_g, even_ln1_b, even_ffn_w_gate, even_ffn_w_up, even_ffn_w_down, even_ln2_g, even_ln2_b, odd_w_in, odd_conv_c_w, odd_sgu_norm_g, odd_sgu_norm_b, odd_sgu_w, odd_sgu_b, odd_w_out, odd_ln1_g, odd_ln1_b, odd_router, odd_moe_w_gate, odd_moe_w_up, odd_moe_w_down, odd_ln2_g, odd_ln2_b):
    for layer in range(DEPTH):
        i = layer // 2
        if layer % 2 == 0:
            x = _even_layer(x, even_w_in[i], even_conv_a_w[i], even_conv_a_b[i], even_norm_a_g[i],
                            even_norm_a_b[i], even_pool_w[i], even_pool_scale[i], even_w_out[i],
                            even_ln1_g[i], even_ln1_b[i], even_ffn_w_gate[i], even_ffn_w_up[i],
                            even_ffn_w_down[i], even_ln2_g[i], even_ln2_b[i])
        else:
            x = _odd_layer(x, odd_w_in[i], odd_conv_c_w[i], odd_sgu_norm_g[i], odd_sgu_norm_b[i],
                           odd_sgu_w[i], odd_sgu_b[i], odd_w_out[i], odd_ln1_g[i], odd_ln1_b[i],
                           odd_router[i], odd_moe_w_gate[i], odd_moe_w_up[i], odd_moe_w_down[i],
                           odd_ln2_g[i], odd_ln2_b[i])
    return x
```

```python
import functools

import jax
import jax.numpy as jnp
from jax import lax
from jax.experimental import pallas as pl
from jax.experimental.pallas import tpu as pltpu

F32 = jnp.float32
BF16 = jnp.bfloat16

D_MODEL = 1024
BATCH = 4
SEQ = 4096
TOKENS = BATCH * SEQ
DEPTH = 2

D_A = 512
CONV_A_WIDTH = 31
D_B = 512
POOL_WINDOWS = (2, 4, 8, 16)
POOL_GROUP_DIM = 128
IN_EVEN = 2 * D_A + D_B

D_C = 512
CONV_C_WIDTH = 3
D_D = 512
SGU_BLOCK = 128
SGU_HEADS = 4
SGU_HEAD_DIM = 128
IN_ODD = 3 * D_C + 2 * D_D

D_FF_DENSE = 2816
N_EXPERTS = 8
D_FF_EXPERT = 3584

ALPHA = (2 * DEPTH) ** 0.25
LN_EPS = 1e-5

SUBLANES = 8
LANES = 128
VMEM_LIMIT_BYTES = 56 * 1024 * 1024

TS_MIX = 512
ROW_SET_STRIDE = 4
CONV_SETS = 8
HALO_A = 32
HALO_B = 16
HALO_C = 8
TM_FFN = 512
FF_CHUNK = 512
TM_MOE = 1024
SUB_MOE = 512
TF_MOE = 512
N_TOKEN_TILES = TOKENS // TS_MIX
N_SEGMENTS = N_TOKEN_TILES * N_EXPERTS
SLOT_ROWS = -(-(2 * TS_MIX + N_EXPERTS * (SUBLANES - 1)) // 16) * 16
MAX_SORTED_ROWS = 2 * TOKENS + N_SEGMENTS * (SUBLANES - 1)
N_TILES_MOE = (MAX_SORTED_ROWS + N_EXPERTS * (TM_MOE - 1)) // TM_MOE
ROWS_SORTED = N_TILES_MOE * TM_MOE


def _layer_norm(x, g, b):
    mu = jnp.mean(x, axis=-1, keepdims=True)
    xc = x - mu
    var = jnp.mean(xc * xc, axis=-1, keepdims=True)
    return xc * lax.rsqrt(var + LN_EPS) * g + b


def _silu(x):
    return x * jax.nn.sigmoid(x)


def _dot(a, b):
    return jnp.dot(a, b, preferred_element_type=F32)


def _even_mixer_kernel(x_ref, win_ref, cw_ref, cb_ref, nag_ref, nab_ref, pw_ref, ps_ref,
                       wout_ref, g1_ref, b1_ref, o_ref, exta_ref, extb_ref, mixf_ref, mixin_ref):
    s = pl.program_id(1)
    ts = TS_MIX

    n_a, n_b = D_A // LANES, D_B // LANES

    @pl.when(s == 0)
    def _():
        exta_ref[:, 0:HALO_A, :] = jnp.zeros((n_a, HALO_A, LANES), F32)
        extb_ref[:, 0:HALO_B, :] = jnp.zeros((n_b, HALO_B, LANES), F32)

    x = x_ref[...]
    h = _dot(x.astype(BF16), win_ref[...])
    glu = h[:, :D_A] * jax.nn.sigmoid(h[:, D_A:2 * D_A])
    for cb in range(n_a):
        exta_ref[cb, HALO_A:HALO_A + ts, :] = glu[:, cb * LANES:(cb + 1) * LANES]
    for cb in range(n_b):
        extb_ref[cb, HALO_B:HALO_B + ts, :] = h[:, 2 * D_A + cb * LANES:2 * D_A + (cb + 1) * LANES]

    def row_set(first):
        return pl.ds(first, SUBLANES, stride=ROW_SET_STRIDE)

    set_firsts = [blk * SUBLANES * ROW_SET_STRIDE + r
                  for blk in range(ts // (SUBLANES * ROW_SET_STRIDE)) for r in range(ROW_SET_STRIDE)]
    for cb in range(n_a):
        lanes = slice(cb * LANES, (cb + 1) * LANES)
        for group in range(0, len(set_firsts), CONV_SETS):
            firsts = set_firsts[group:group + CONV_SETS]
            accs = [jnp.broadcast_to(cb_ref[:, lanes], (SUBLANES, LANES))] * len(firsts)
            for k in range(CONV_A_WIDTH):
                wk = jnp.broadcast_to(cw_ref[k:k + 1, lanes], (SUBLANES, LANES))
                back = CONV_A_WIDTH - 1 - k
                for n, first in enumerate(firsts):
                    accs[n] = accs[n] + wk * exta_ref[cb, row_set(HALO_A + first - back), :]
            for n, first in enumerate(firsts):
                mixf_ref[cb, row_set(first), :] = accs[n]

    set_step = ROW_SET_STRIDE * lax.broadcasted_iota(jnp.int32, (SUBLANES, 1), 0)
    for first in set_firsts:
        row_pos = s * ts + first + set_step
        for g, win in enumerate(POOL_WINDOWS):
            frame = extb_ref[g, row_set(HALO_B + first), :]
            wsum = frame
            for i in range(1, win):
                wsum = wsum + extb_ref[g, row_set(HALO_B + first - i), :]
            if first >= win - 1:
                mean = wsum * (1.0 / win)
            else:
                mean = wsum / jnp.minimum(row_pos + 1, win).astype(F32)
            mixf_ref[n_a + g, row_set(first), :] = mean - frame

    norm_rows = 128
    for r0 in range(0, ts, norm_rows):
        conv = jnp.concatenate([mixf_ref[cb, r0:r0 + norm_rows, :] for cb in range(n_a)], axis=1)
        y = _silu(_layer_norm(conv, nag_ref[...], nab_ref[...]))
        mixin_ref[r0:r0 + norm_rows, 0:D_A] = y.astype(BF16)
    for g in range(n_b):
        lo, hi = g * POOL_GROUP_DIM, (g + 1) * POOL_GROUP_DIM
        mixed = _dot(mixf_ref[n_a + g].astype(BF16), pw_ref[g]) * ps_ref[:, lo:hi]
        mixin_ref[:, D_A + lo:D_A + hi] = mixed.astype(BF16)

    mix = _dot(mixin_ref[...], wout_ref[...])
    o_ref[...] = _layer_norm(ALPHA * x + mix, g1_ref[...], b1_ref[...])

    exta_ref[:, 0:HALO_A, :] = exta_ref[:, ts:ts + HALO_A, :]
    extb_ref[:, 0:HALO_B, :] = extb_ref[:, ts:ts + HALO_B, :]


def _const_spec(shape):
    return pl.BlockSpec(shape, lambda *_: (0,) * len(shape))


def _even_mixer(x, w_in, conv_w, conv_b, norm_g, norm_b, pool_w, pool_scale, w_out, ln_g, ln_b):
    ts = TS_MIX
    x_spec = pl.BlockSpec((None, ts, D_MODEL), lambda b, s: (b, s, 0))
    return pl.pallas_call(
        _even_mixer_kernel,
        grid=(BATCH, SEQ // ts),
        in_specs=[
            x_spec,
            _const_spec((D_MODEL, IN_EVEN)),
            _const_spec((CONV_A_WIDTH, D_A)),
            _const_spec((1, D_A)),
            _const_spec((1, D_A)),
            _const_spec((1, D_A)),
            _const_spec((len(POOL_WINDOWS), POOL_GROUP_DIM, POOL_GROUP_DIM)),
            _const_spec((1, D_B)),
            _const_spec((D_A + D_B, D_MODEL)),
            _const_spec((1, D_MODEL)),
            _const_spec((1, D_MODEL)),
        ],
        out_specs=x_spec,
        out_shape=jax.ShapeDtypeStruct((BATCH, SEQ, D_MODEL), F32),
        scratch_shapes=[
            pltpu.VMEM((D_A // LANES, HALO_A + ts, LANES), F32),
            pltpu.VMEM((D_B // LANES, HALO_B + ts, LANES), F32),
            pltpu.VMEM(((D_A + D_B) // LANES, ts, LANES), F32),
            pltpu.VMEM((ts, D_A + D_B), BF16),
        ],
        compiler_params=pltpu.CompilerParams(
            dimension_semantics=("arbitrary", "arbitrary"), vmem_limit_bytes=VMEM_LIMIT_BYTES),
        name="even_mixer",
    )(x, w_in.astype(BF16), conv_w, conv_b[None], norm_g[None], norm_b[None],
      pool_w.astype(BF16), pool_scale[None], w_out.astype(BF16), ln_g[None], ln_b[None])


def _ff_chunks(total, chunk):
    bounds = list(range(0, total, chunk)) + [total]
    return list(zip(bounds[:-1], bounds[1:]))


def _dense_ffn_kernel(x_ref, wg_ref, wu_ref, wd_ref, g_ref, b_ref, o_ref, acc_ref):
    x = x_ref[...]
    xb = x.astype(BF16)
    for n, (lo, hi) in enumerate(_ff_chunks(D_FF_DENSE, FF_CHUNK)):
        a = _silu(_dot(xb, wg_ref[:, lo:hi])) * _dot(xb, wu_ref[:, lo:hi])
        part = _dot(a.astype(BF16), wd_ref[lo:hi, :])
        if n == 0:
            acc_ref[...] = part
        else:
            acc_ref[...] += part
    o_ref[...] = _layer_norm(ALPHA * x + acc_ref[...], g_ref[...], b_ref[...])


def _dense_ffn(x, w_gate, w_up, w_down, ln_g, ln_b):
    tm = TM_FFN
    row_spec = pl.BlockSpec((tm, D_MODEL), lambda i: (i, 0))
    resident = pl.BlockSpec(memory_space=pltpu.VMEM)
    return pl.pallas_call(
        _dense_ffn_kernel,
        grid=(TOKENS // tm,),
        in_specs=[row_spec, resident, resident, resident,
                  _const_spec((1, D_MODEL)), _const_spec((1, D_MODEL))],
        out_specs=row_spec,
        out_shape=jax.ShapeDtypeStruct((TOKENS, D_MODEL), F32),
        scratch_shapes=[pltpu.VMEM((tm, D_MODEL), F32)],
        compiler_params=pltpu.CompilerParams(
            dimension_semantics=("arbitrary",), vmem_limit_bytes=VMEM_LIMIT_BYTES),
        name="dense_ffn",
    )(x, w_gate.astype(BF16), w_up.astype(BF16), w_down.astype(BF16), ln_g[None], ln_b[None])


META_E1, META_E2, META_W1, META_W2, META_P1, META_P2 = range(6)


def _odd_mixer_kernel(x_ref, win_ref, ccw_ref, sng_ref, snb_ref, sw_ref, sb_ref, wout_ref,
                      g1_ref, b1_ref, rhi_ref, rlo_ref, o_ref, meta_ref, cnt_ref,
                      extc_ref, mixin_ref):
    s = pl.program_id(1)
    ts = TS_MIX

    @pl.when(s == 0)
    def _():
        extc_ref[0:HALO_C, :] = jnp.zeros((HALO_C, D_C), F32)

    x = x_ref[...]
    h = _dot(x.astype(BF16), win_ref[...])

    extc_ref[HALO_C:HALO_C + ts, :] = h[:, D_C:2 * D_C] * h[:, 2 * D_C:3 * D_C]
    conv = ccw_ref[0:1, :] * extc_ref[HALO_C - 2:HALO_C - 2 + ts, :]
    conv = conv + ccw_ref[1:2, :] * extc_ref[HALO_C - 1:HALO_C - 1 + ts, :]
    conv = conv + ccw_ref[2:3, :] * extc_ref[HALO_C:HALO_C + ts, :]
    mixin_ref[:, 0:D_C] = (h[:, 0:D_C] * conv).astype(BF16)

    z = jax.nn.gelu(h[:, 3 * D_C:])
    u = z[:, :D_D]
    v = _layer_norm(z[:, D_D:], sng_ref[...], snb_ref[...]).astype(BF16)
    tri = (lax.broadcasted_iota(jnp.int32, (SGU_BLOCK, SGU_BLOCK), 0)
           >= lax.broadcasted_iota(jnp.int32, (SGU_BLOCK, SGU_BLOCK), 1))
    for hd in range(SGU_HEADS):
        lo, hi = hd * SGU_HEAD_DIM, (hd + 1) * SGU_HEAD_DIM
        w = jnp.where(tri, sw_ref[hd], 0.0).astype(BF16)
        bias = sb_ref[:, hd:hd + 1]
        for blk in range(ts // SGU_BLOCK):
            r0, r1 = blk * SGU_BLOCK, (blk + 1) * SGU_BLOCK
            mixed = _dot(w, v[r0:r1, lo:hi]) + bias
            mixin_ref[r0:r1, D_C + lo:D_C + hi] = (u[r0:r1, lo:hi] * mixed).astype(BF16)

    mix = _dot(mixin_ref[...], wout_ref[...])
    x1 = _layer_norm(ALPHA * x + mix, g1_ref[...], b1_ref[...])
    o_ref[...] = x1

    extc_ref[0:HALO_C, :] = extc_ref[ts:ts + HALO_C, :]

    x_hi = x1.astype(BF16)
    x_lo = (x1 - x_hi.astype(F32)).astype(BF16)
    logits = _dot(x_hi, rhi_ref[...]) + _dot(x_lo, rhi_ref[...]) + _dot(x_hi, rlo_ref[...])

    lane = lax.broadcasted_iota(jnp.int32, (ts, LANES), 1).astype(F32)
    neg_inf = jnp.float32(-jnp.inf)
    lg = jnp.where(lane < N_EXPERTS, logits, neg_inf)
    m1 = jnp.max(lg, axis=1, keepdims=True)
    e1 = jnp.min(jnp.where(lg == m1, lane, float(LANES)), axis=1, keepdims=True)
    lg2 = jnp.where(lane == e1, neg_inf, lg)
    m2 = jnp.max(lg2, axis=1, keepdims=True)
    e2 = jnp.min(jnp.where(lg2 == m2, lane, float(LANES)), axis=1, keepdims=True)
    ex = jnp.exp(m2 - m1)
    den = 1.0 + ex
    w1 = 1.0 / den
    w2 = ex / den

    onehot = jnp.where(jnp.logical_or(lane == e1, lane == e2), 1.0, 0.0)
    onehot_b = onehot.astype(BF16)
    before = (lax.broadcasted_iota(jnp.int32, (ts, ts), 0)
              > lax.broadcasted_iota(jnp.int32, (ts, ts), 1))
    lower = (lax.broadcasted_iota(jnp.int32, (LANES, LANES), 0)
             < lax.broadcasted_iota(jnp.int32, (LANES, LANES), 1))
    earlier = _dot(jnp.where(before, 1.0, 0.0).astype(BF16), onehot_b)
    run = jnp.sum(onehot, axis=0, keepdims=True)
    run = jnp.floor((run + (SUBLANES - 1)) * (1.0 / SUBLANES)) * SUBLANES
    run_rows = jnp.broadcast_to(run, (SUBLANES, LANES)).astype(BF16)
    lower_rows = _dot(run_rows, jnp.where(lower, 1.0, 0.0).astype(BF16))[0:1, :]
    place = earlier + lower_rows
    p1 = jnp.sum(jnp.where(lane == e1, place, 0.0), axis=1, keepdims=True)
    p2 = jnp.sum(jnp.where(lane == e2, place, 0.0), axis=1, keepdims=True)

    meta = jnp.zeros((ts, LANES), F32)
    for col, val in ((META_E1, e1), (META_E2, e2), (META_W1, w1), (META_W2, w2),
                     (META_P1, p1), (META_P2, p2)):
        meta = jnp.where(lane == col, val, meta)
    meta_ref[...] = meta
    cnt_ref[...] = jnp.broadcast_to(run, (SUBLANES, LANES))


def _odd_mixer(x, w_in, conv_w, sgu_norm_g, sgu_norm_b, sgu_w, sgu_b, w_out, ln_g, ln_b, router):
    ts = TS_MIX
    x_spec = pl.BlockSpec((None, ts, D_MODEL), lambda b, s: (b, s, 0))
    router_pad = jnp.zeros((D_MODEL, LANES), F32).at[:, :N_EXPERTS].set(router)
    router_hi = router_pad.astype(BF16)
    router_lo = (router_pad - router_hi.astype(F32)).astype(BF16)
    return pl.pallas_call(
        _odd_mixer_kernel,
        grid=(BATCH, SEQ // ts),
        in_specs=[
            x_spec,
            _const_spec((D_MODEL, IN_ODD)),
            _const_spec((CONV_C_WIDTH, D_C)),
            _const_spec((1, D_D)),
            _const_spec((1, D_D)),
            _const_spec((SGU_HEADS, SGU_BLOCK, SGU_BLOCK)),
            _const_spec((SGU_BLOCK, SGU_HEADS)),
            _const_spec((D_C + D_D, D_MODEL)),
            _const_spec((1, D_MODEL)),
            _const_spec((1, D_MODEL)),
            _const_spec((D_MODEL, LANES)),
            _const_spec((D_MODEL, LANES)),
        ],
        out_specs=[
            x_spec,
            pl.BlockSpec((ts, LANES), lambda b, s: (b * (SEQ // ts) + s, 0)),
            pl.BlockSpec((None, SUBLANES, LANES), lambda b, s: (b * (SEQ // ts) + s, 0, 0)),
        ],
        out_shape=[
            jax.ShapeDtypeStruct((BATCH, SEQ, D_MODEL), F32),
            jax.ShapeDtypeStruct((TOKENS, LANES), F32),
            jax.ShapeDtypeStruct((N_TOKEN_TILES, SUBLANES, LANES), F32),
        ],
        scratch_shapes=[
            pltpu.VMEM((HALO_C + ts, D_C), F32),
            pltpu.VMEM((ts, D_C + D_D), BF16),
        ],
        compiler_params=pltpu.CompilerParams(
            dimension_semantics=("arbitrary", "arbitrary"), vmem_limit_bytes=VMEM_LIMIT_BYTES),
        name="odd_mixer",
    )(x, w_in.astype(BF16), conv_w, sgu_norm_g[None], sgu_norm_b[None], sgu_w, sgu_b.T,
      w_out.astype(BF16), ln_g[None], ln_b[None], router_hi, router_lo)


SEGMENT_BITS = range(SUBLANES.bit_length() - 1, TS_MIX.bit_length())


def _for_each_segment_chunk(seg_ref, tile, visit):
    for e in range(N_EXPERTS):
        k = tile * N_EXPERTS + e
        n = seg_ref[k]
        local = seg_ref[N_SEGMENTS + k]
        sorted_row = seg_ref[2 * N_SEGMENTS + k]
        for bit in reversed(SEGMENT_BITS):
            @pl.when(((n >> bit) & 1) == 1)
            def _():
                done = (n >> (bit + 1)) << (bit + 1)
                visit(pl.multiple_of(local + done, SUBLANES), pl.multiple_of(sorted_row + done, SUBLANES),
                      1 << bit)


def _dispatch_kernel(seg_ref, pad_tile_ref, x_ref, meta_ref, xs_ref, zero_ref, perm_ref, zero_sem, perm_sem):
    i = pl.program_id(0)
    slot = i % 2

    def chunk_copy(which):
        def make(local, sorted_row, size):
            return pltpu.make_async_copy(perm_ref.at[which, pl.ds(local, size)],
                                         xs_ref.at[pl.ds(sorted_row, size)], perm_sem.at[which])
        return make

    def wait_tile(tile, which):
        _for_each_segment_chunk(seg_ref, tile, lambda *chunk: chunk_copy(which)(*chunk).wait())

    @pl.when(i == 0)
    def _():
        zero_ref[...] = jnp.zeros(zero_ref.shape, F32)

        def clear_tile(t):
            start = pl.multiple_of(t * TM_MOE, TM_MOE)
            cp = pltpu.make_async_copy(zero_ref, xs_ref.at[pl.ds(start, TM_MOE)], zero_sem)
            cp.start()
            cp.wait()

        for e in range(N_EXPERTS):
            @pl.when(pad_tile_ref[e] >= 0)
            def _():
                clear_tile(pad_tile_ref[e])

        def clear_tail(t, carry):
            clear_tile(t)
            return carry

        lax.fori_loop(pad_tile_ref[N_EXPERTS], N_TILES_MOE, clear_tail, 0)

    meta = meta_ref[...]
    col = lax.broadcasted_iota(jnp.int32, (TS_MIX, SLOT_ROWS), 1).astype(F32)
    hit = jnp.logical_or(col == meta[:, META_P1:META_P1 + 1], col == meta[:, META_P2:META_P2 + 1])
    onehot = jnp.where(hit, 1.0, 0.0).astype(BF16)
    permuted = lax.dot_general(onehot, x_ref[...].astype(BF16), (((0,), (0,)), ((), ())),
                               preferred_element_type=F32)

    @pl.when(i >= 2)
    def _():
        wait_tile(i - 2, slot)

    perm_ref[slot] = permuted
    _for_each_segment_chunk(seg_ref, i, lambda *chunk: chunk_copy(slot)(*chunk).start())

    @pl.when(i == N_TOKEN_TILES - 1)
    def _():
        wait_tile(i - 1, 1 - slot)
        wait_tile(i, slot)


def _dispatch(x, meta, seg, pad_tile):
    return pl.pallas_call(
        _dispatch_kernel,
        grid_spec=pltpu.PrefetchScalarGridSpec(
            num_scalar_prefetch=2,
            grid=(N_TOKEN_TILES,),
            in_specs=[pl.BlockSpec((TS_MIX, D_MODEL), lambda i, *_: (i, 0)),
                      pl.BlockSpec((TS_MIX, LANES), lambda i, *_: (i, 0))],
            out_specs=pl.BlockSpec(memory_space=pl.ANY),
            scratch_shapes=[
                pltpu.VMEM((TM_MOE, D_MODEL), F32),
                pltpu.VMEM((2, SLOT_ROWS, D_MODEL), F32),
                pltpu.SemaphoreType.DMA(()),
                pltpu.SemaphoreType.DMA((2,)),
            ],
        ),
        out_shape=jax.ShapeDtypeStruct((ROWS_SORTED, D_MODEL), F32),
        compiler_params=pltpu.CompilerParams(
            dimension_semantics=("arbitrary",), vmem_limit_bytes=VMEM_LIMIT_BYTES),
        name="moe_dispatch",
    )(seg, pad_tile, x, meta)


def _grouped_ffn_kernel(tile_expert_ref, tile_rows_ref, n_active_ref, xs_ref, wg_ref, wu_ref, wd_ref,
                        ys_ref, xb_ref):
    i = pl.program_id(0)
    j = pl.program_id(1)

    active = i < n_active_ref[0]

    @pl.when(j == 0)
    def _():
        ys_ref[...] = jnp.zeros((TM_MOE, D_MODEL), F32)

    @pl.when(jnp.logical_and(active, j == 0))
    def _():
        xb_ref[...] = xs_ref[...].astype(BF16)

    def accumulate(n_rows):
        xb = xb_ref[0:n_rows, :]
        a = _silu(_dot(xb, wg_ref[...].astype(BF16))) * _dot(xb, wu_ref[...].astype(BF16))
        ys_ref[0:n_rows, :] += _dot(a.astype(BF16), wd_ref[...].astype(BF16))

    for n_sub in range(1, TM_MOE // SUB_MOE + 1):
        lo, hi = (n_sub - 1) * SUB_MOE, n_sub * SUB_MOE

        @pl.when(jnp.logical_and(active, jnp.logical_and(tile_rows_ref[i] > lo, tile_rows_ref[i] <= hi)))
        def _():
            accumulate(hi)


def _grouped_ffn(xs, w_gate, w_up, w_down, tile_expert, tile_rows, n_active):
    nj = D_FF_EXPERT // TF_MOE

    def row_map(i, j, te, tr, na):
        return (jnp.minimum(i, na[0] - 1), 0)

    def frozen_j(i, j, na):
        return jnp.where(i < na[0], j, nj - 1)

    def up_map(i, j, te, tr, na):
        return (te[i], 0, frozen_j(i, j, na))

    def down_map(i, j, te, tr, na):
        return (te[i], frozen_j(i, j, na), 0)

    return pl.pallas_call(
        _grouped_ffn_kernel,
        grid_spec=pltpu.PrefetchScalarGridSpec(
            num_scalar_prefetch=3,
            grid=(N_TILES_MOE, nj),
            in_specs=[
                pl.BlockSpec((TM_MOE, D_MODEL), row_map),
                pl.BlockSpec((None, D_MODEL, TF_MOE), up_map),
                pl.BlockSpec((None, D_MODEL, TF_MOE), up_map),
                pl.BlockSpec((None, TF_MOE, D_MODEL), down_map),
            ],
            out_specs=pl.BlockSpec((TM_MOE, D_MODEL), lambda i, j, *_: (i, 0)),
            scratch_shapes=[pltpu.VMEM((TM_MOE, D_MODEL), BF16)],
        ),
        out_shape=jax.ShapeDtypeStruct((ROWS_SORTED, D_MODEL), F32),
        compiler_params=pltpu.CompilerParams(
            dimension_semantics=("arbitrary", "arbitrary"), vmem_limit_bytes=VMEM_LIMIT_BYTES),
        name="moe_grouped_ffn",
    )(tile_expert, tile_rows, n_active, xs, w_gate, w_up, w_down)


def _combine_kernel(seg_ref, x_ref, meta_ref, ys_ref, g_ref, b_ref, o_ref, win_ref, sem):
    i = pl.program_id(0)
    slot = i % 2

    def chunk_copy(which):
        def make(local, sorted_row, size):
            return pltpu.make_async_copy(ys_ref.at[pl.ds(sorted_row, size)],
                                         win_ref.at[which, pl.ds(local, size)], sem.at[which])
        return make

    def fetch(tile, which):
        _for_each_segment_chunk(seg_ref, tile, lambda *chunk: chunk_copy(which)(*chunk).start())

    @pl.when(i == 0)
    def _():
        win_ref[...] = jnp.zeros(win_ref.shape, F32)
        fetch(0, 0)

    @pl.when(i + 1 < N_TOKEN_TILES)
    def _():
        fetch(i + 1, 1 - slot)

    _for_each_segment_chunk(seg_ref, i, lambda *chunk: chunk_copy(slot)(*chunk).wait())

    meta = meta_ref[...]
    col = lax.broadcasted_iota(jnp.int32, (TS_MIX, SLOT_ROWS), 1).astype(F32)
    gate = (jnp.where(col == meta[:, META_P1:META_P1 + 1], meta[:, META_W1:META_W1 + 1], 0.0)
            + jnp.where(col == meta[:, META_P2:META_P2 + 1], meta[:, META_W2:META_W2 + 1], 0.0))
    moe = _dot(gate.astype(BF16), win_ref[slot].astype(BF16))
    o_ref[...] = _layer_norm(ALPHA * x_ref[...] + moe, g_ref[...], b_ref[...])


def _combine(x, meta, ys, seg, ln_g, ln_b):
    row_spec = pl.BlockSpec((TS_MIX, D_MODEL), lambda i, *_: (i, 0))
    return pl.pallas_call(
        _combine_kernel,
        grid_spec=pltpu.PrefetchScalarGridSpec(
            num_scalar_prefetch=1,
            grid=(N_TOKEN_TILES,),
            in_specs=[
                row_spec,
                pl.BlockSpec((TS_MIX, LANES), lambda i, *_: (i, 0)),
                pl.BlockSpec(memory_space=pl.ANY),
                pl.BlockSpec((1, D_MODEL), lambda i, *_: (0, 0)),
                pl.BlockSpec((1, D_MODEL), lambda i, *_: (0, 0)),
            ],
            out_specs=row_spec,
            scratch_shapes=[
                pltpu.VMEM((2, SLOT_ROWS, D_MODEL), F32),
                pltpu.SemaphoreType.DMA((2,)),
            ],
        ),
        out_shape=jax.ShapeDtypeStruct((TOKENS, D_MODEL), F32),
        compiler_params=pltpu.CompilerParams(
            dimension_semantics=("arbitrary",), vmem_limit_bytes=VMEM_LIMIT_BYTES),
        name="moe_combine",
    )(seg, x, meta, ys, ln_g[None], ln_b[None])


def _routing_tables(tile_counts):
    n = tile_counts[:, 0, :N_EXPERTS].astype(jnp.int32)
    counts = jnp.sum(n, axis=0)
    tiles = (counts + TM_MOE - 1) // TM_MOE
    tile_end = jnp.cumsum(tiles)
    tile_start = tile_end - tiles
    offset = tile_start * TM_MOE
    local_start = jnp.cumsum(n, axis=1) - n
    sorted_start = offset[None, :] + jnp.cumsum(n, axis=0) - n
    seg = jnp.concatenate([n.reshape(-1), local_start.reshape(-1), sorted_start.reshape(-1)])
    n_active = tile_end[-1:]
    tile_id = jnp.minimum(jnp.arange(N_TILES_MOE, dtype=jnp.int32), n_active[0] - 1)
    tile_expert = jnp.sum(tile_id[:, None] >= tile_end[None, :], axis=1).astype(jnp.int32)
    tile_rows = jnp.clip(counts[tile_expert] - (tile_id - tile_start[tile_expert]) * TM_MOE, 0, TM_MOE)
    pad_tile = jnp.concatenate([jnp.where(tiles > 0, tile_end - 1, -1), n_active]).astype(jnp.int32)
    return (seg.astype(jnp.int32), pad_tile, tile_expert, tile_rows.astype(jnp.int32),
            n_active.astype(jnp.int32))


def _even_layer(x, w_in, conv_a_w, conv_a_b, norm_a_g, norm_a_b, pool_w, pool_scale, w_out,
                ln1_g, ln1_b, ffn_w_gate, ffn_w_up, ffn_w_down, ln2_g, ln2_b):
    x = _even_mixer(x, w_in, conv_a_w, conv_a_b, norm_a_g, norm_a_b, pool_w, pool_scale, w_out,
                    ln1_g, ln1_b)
    x = _dense_ffn(x.reshape(TOKENS, D_MODEL), ffn_w_gate, ffn_w_up, ffn_w_down, ln2_g, ln2_b)
    return x.reshape(BATCH, SEQ, D_MODEL)


def _odd_layer(x, w_in, conv_c_w, sgu_norm_g, sgu_norm_b, sgu_w, sgu_b, w_out, ln1_g, ln1_b,
               router, moe_w_gate, moe_w_up, moe_w_down, ln2_g, ln2_b):
    x, meta, tile_counts = _odd_mixer(x, w_in, conv_c_w, sgu_norm_g, sgu_norm_b, sgu_w, sgu_b, w_out,
                                      ln1_g, ln1_b, router)
    x = x.reshape(TOKENS, D_MODEL)
    seg, pad_tile, tile_expert, tile_rows, n_active = _routing_tables(tile_counts)
    xs = _dispatch(x, meta, seg, pad_tile)
    ys = _grouped_ffn(xs, moe_w_gate, moe_w_up, moe_w_down, tile_expert, tile_rows, n_active)
    x = _combine(x, meta, ys, seg, ln2_g, ln2_b)
    return x.reshape(BATCH, SEQ, D_MODEL)


def kernel(x, even_w_in, even_conv_a_w, even_conv_a_b, even_norm_a_g, even_norm_a_b, even_pool_w, even_pool_scale, even_w_out, even_ln1_g, even_ln1_b, even_ffn_w_gate, even_ffn_w_up, even_ffn_w_down, even_ln2_g, even_ln2_b, odd_w_in, odd_conv_c_w, odd_sgu_norm_g, odd_sgu_norm_b, odd_sgu_w, odd_sgu_b, odd_w_out, odd_ln1_g, odd_ln1_b, odd_router, odd_moe_w_gate, odd_moe_w_up, odd_moe_w_down, odd_ln2_g, odd_ln2_b):
    for layer in range(DEPTH):
        i = layer // 2
        if layer % 2 == 0:
            x = _even_layer(x, even_w_in[i], even_conv_a_w[i], even_conv_a_b[i], even_norm_a_g[i],
                            even_norm_a_b[i], even_pool_w[i], even_pool_scale[i], even_w_out[i],
                            even_ln1_g[i], even_ln1_b[i], even_ffn_w_gate[i], even_ffn_w_up[i],
                            even_ffn_w_down[i], even_ln2_g[i], even_ln2_b[i])
        else:
            x = _odd_layer(x, odd_w_in[i], odd_conv_c_w[i], odd_sgu_norm_g[i], odd_sgu_norm_b[i],
                           odd_sgu_w[i], odd_sgu_b[i], odd_w_out[i], odd_ln1_g[i], odd_ln1_b[i],
                           odd_router[i], odd_moe_w_gate[i], odd_moe_w_up[i], odd_moe_w_down[i],
                           odd_ln2_g[i], odd_ln2_b[i])
    return x
```

```python
import functools

import jax
import jax.numpy as jnp
from jax import lax
from jax.experimental import pallas as pl
from jax.experimental.pallas import tpu as pltpu

F32 = jnp.float32
BF16 = jnp.bfloat16

D_MODEL = 1024
BATCH = 4
SEQ = 4096
TOKENS = BATCH * SEQ
DEPTH = 2

D_A = 512
CONV_A_WIDTH = 31
D_B = 512
POOL_WINDOWS = (2, 4, 8, 16)
POOL_GROUP_DIM = 128
IN_EVEN = 2 * D_A + D_B

D_C = 512
CONV_C_WIDTH = 3
D_D = 512
SGU_BLOCK = 128
SGU_HEADS = 4
SGU_HEAD_DIM = 128
IN_ODD = 3 * D_C + 2 * D_D

D_FF_DENSE = 2816
N_EXPERTS = 8
D_FF_EXPERT = 3584

ALPHA = (2 * DEPTH) ** 0.25
LN_EPS = 1e-5

SUBLANES = 8
LANES = 128
VMEM_LIMIT_BYTES = 56 * 1024 * 1024

TS_MIX = 512
MIX_CHAINS = 2
ROW_SET_STRIDE = 4
CONV_SETS = 8
HALO_A = 32
HALO_B = 16
HALO_C = 8
TM_FFN = 1024
FF_CHUNK = 512
TM_MOE = 1024
SUB_MOE = 256
TF_MOE = 512
N_TOKEN_TILES = TOKENS // TS_MIX
N_SEGMENTS = N_TOKEN_TILES * N_EXPERTS
SLOT_ROWS = -(-(2 * TS_MIX + N_EXPERTS * (SUBLANES - 1)) // 16) * 16
MAX_SORTED_ROWS = 2 * TOKENS + N_SEGMENTS * (SUBLANES - 1)
N_TILES_MOE = (MAX_SORTED_ROWS + N_EXPERTS * (TM_MOE - 1)) // TM_MOE
ROWS_SORTED = N_TILES_MOE * TM_MOE


def _layer_norm(x, g, b):
    mu = jnp.mean(x, axis=-1, keepdims=True)
    xc = x - mu
    var = jnp.mean(xc * xc, axis=-1, keepdims=True)
    return xc * lax.rsqrt(var + LN_EPS) * g + b


def _silu(x):
    return x * jax.nn.sigmoid(x)


def _dot(a, b):
    return jnp.dot(a, b, preferred_element_type=F32)


def _even_mixer_kernel(x_ref, win_ref, cw_ref, cb_ref, nag_ref, nab_ref, pw_ref, ps_ref,
                       wout_ref, g1_ref, b1_ref, o_ref, exta_ref, extb_ref, mixf_ref, mixin_ref):
    s = pl.program_id(1)
    ts = TS_MIX

    n_a, n_b = D_A // LANES, D_B // LANES

    @pl.when(s == 0)
    def _():
        exta_ref[:, 0:HALO_A, :] = jnp.zeros((n_a, HALO_A, LANES), F32)
        extb_ref[:, 0:HALO_B, :] = jnp.zeros((n_b, HALO_B, LANES), F32)

    x = x_ref[...]
    h = _dot(x.astype(BF16), win_ref[...])
    glu = h[:, :D_A] * jax.nn.sigmoid(h[:, D_A:2 * D_A])
    for cb in range(n_a):
        exta_ref[cb, HALO_A:HALO_A + ts, :] = glu[:, cb * LANES:(cb + 1) * LANES]
    for cb in range(n_b):
        extb_ref[cb, HALO_B:HALO_B + ts, :] = h[:, 2 * D_A + cb * LANES:2 * D_A + (cb + 1) * LANES]

    def row_set(first):
        return pl.ds(first, SUBLANES, stride=ROW_SET_STRIDE)

    set_firsts = [blk * SUBLANES * ROW_SET_STRIDE + r
                  for blk in range(ts // (SUBLANES * ROW_SET_STRIDE)) for r in range(ROW_SET_STRIDE)]
    for cb in range(n_a):
        lanes = slice(cb * LANES, (cb + 1) * LANES)
        for group in range(0, len(set_firsts), CONV_SETS):
            firsts = set_firsts[group:group + CONV_SETS]
            accs = [jnp.broadcast_to(cb_ref[:, lanes], (SUBLANES, LANES))] * len(firsts)
            for k in range(CONV_A_WIDTH):
                wk = jnp.broadcast_to(cw_ref[k:k + 1, lanes], (SUBLANES, LANES))
                back = CONV_A_WIDTH - 1 - k
                for n, first in enumerate(firsts):
                    accs[n] = accs[n] + wk * exta_ref[cb, row_set(HALO_A + first - back), :]
            for n, first in enumerate(firsts):
                mixf_ref[cb, row_set(first), :] = accs[n]

    set_step = ROW_SET_STRIDE * lax.broadcasted_iota(jnp.int32, (SUBLANES, 1), 0)
    for first in set_firsts:
        row_pos = s * ts + first + set_step
        for g, win in enumerate(POOL_WINDOWS):
            frame = extb_ref[g, row_set(HALO_B + first), :]
            wsum = frame
            for i in range(1, win):
                wsum = wsum + extb_ref[g, row_set(HALO_B + first - i), :]
            if first >= win - 1:
                mean = wsum * (1.0 / win)
            else:
                mean = wsum / jnp.minimum(row_pos + 1, win).astype(F32)
            mixf_ref[n_a + g, row_set(first), :] = mean - frame

    norm_rows = 128
    for r0 in range(0, ts, norm_rows):
        conv = jnp.concatenate([mixf_ref[cb, r0:r0 + norm_rows, :] for cb in range(n_a)], axis=1)
        y = _silu(_layer_norm(conv, nag_ref[...], nab_ref[...]))
        mixin_ref[r0:r0 + norm_rows, 0:D_A] = y.astype(BF16)
    for g in range(n_b):
        lo, hi = g * POOL_GROUP_DIM, (g + 1) * POOL_GROUP_DIM
        mixed = _dot(mixf_ref[n_a + g].astype(BF16), pw_ref[g]) * ps_ref[:, lo:hi]
        mixin_ref[:, D_A + lo:D_A + hi] = mixed.astype(BF16)

    mix = _dot(mixin_ref[...], wout_ref[...])
    o_ref[...] = _layer_norm(ALPHA * x + mix, g1_ref[...], b1_ref[...])

    exta_ref[:, 0:HALO_A, :] = exta_ref[:, ts:ts + HALO_A, :]
    extb_ref[:, 0:HALO_B, :] = extb_ref[:, ts:ts + HALO_B, :]


def _const_spec(shape):
    return pl.BlockSpec(shape, lambda *_: (0,) * len(shape))


def _even_mixer(x, w_in, conv_w, conv_b, norm_g, norm_b, pool_w, pool_scale, w_out, ln_g, ln_b):
    ts = TS_MIX
    x_spec = pl.BlockSpec((None, ts, D_MODEL), lambda b, s: (b, s, 0))
    return pl.pallas_call(
        _even_mixer_kernel,
        grid=(BATCH, SEQ // ts),
        in_specs=[
            x_spec,
            _const_spec((D_MODEL, IN_EVEN)),
            _const_spec((CONV_A_WIDTH, D_A)),
            _const_spec((1, D_A)),
            _const_spec((1, D_A)),
            _const_spec((1, D_A)),
            _const_spec((len(POOL_WINDOWS), POOL_GROUP_DIM, POOL_GROUP_DIM)),
            _const_spec((1, D_B)),
            _const_spec((D_A + D_B, D_MODEL)),
            _const_spec((1, D_MODEL)),
            _const_spec((1, D_MODEL)),
        ],
        out_specs=x_spec,
        out_shape=jax.ShapeDtypeStruct((BATCH, SEQ, D_MODEL), F32),
        scratch_shapes=[
            pltpu.VMEM((D_A // LANES, HALO_A + ts, LANES), F32),
            pltpu.VMEM((D_B // LANES, HALO_B + ts, LANES), F32),
            pltpu.VMEM(((D_A + D_B) // LANES, ts, LANES), F32),
            pltpu.VMEM((ts, D_A + D_B), BF16),
        ],
        compiler_params=pltpu.CompilerParams(
            dimension_semantics=("arbitrary", "arbitrary"), vmem_limit_bytes=VMEM_LIMIT_BYTES),
        name="even_mixer",
    )(x, w_in.astype(BF16), conv_w, conv_b[None], norm_g[None], norm_b[None],
      pool_w.astype(BF16), pool_scale[None], w_out.astype(BF16), ln_g[None], ln_b[None])


def _ff_chunks(total, chunk):
    bounds = list(range(0, total, chunk)) + [total]
    return list(zip(bounds[:-1], bounds[1:]))


def _dense_ffn_kernel(x_ref, wg_ref, wu_ref, wd_ref, g_ref, b_ref, o_ref, acc_ref):
    x = x_ref[...]
    xb = x.astype(BF16)
    for n, (lo, hi) in enumerate(_ff_chunks(D_FF_DENSE, FF_CHUNK)):
        a = _silu(_dot(xb, wg_ref[:, lo:hi])) * _dot(xb, wu_ref[:, lo:hi])
        part = _dot(a.astype(BF16), wd_ref[lo:hi, :])
        if n == 0:
            acc_ref[...] = part
        else:
            acc_ref[...] += part
    o_ref[...] = _layer_norm(ALPHA * x + acc_ref[...], g_ref[...], b_ref[...])


def _dense_ffn(x, w_gate, w_up, w_down, ln_g, ln_b):
    tm = TM_FFN
    row_spec = pl.BlockSpec((tm, D_MODEL), lambda i: (i, 0))
    resident = pl.BlockSpec(memory_space=pltpu.VMEM)
    return pl.pallas_call(
        _dense_ffn_kernel,
        grid=(TOKENS // tm,),
        in_specs=[row_spec, resident, resident, resident,
                  _const_spec((1, D_MODEL)), _const_spec((1, D_MODEL))],
        out_specs=row_spec,
        out_shape=jax.ShapeDtypeStruct((TOKENS, D_MODEL), F32),
        scratch_shapes=[pltpu.VMEM((tm, D_MODEL), F32)],
        compiler_params=pltpu.CompilerParams(
            dimension_semantics=("arbitrary",), vmem_limit_bytes=VMEM_LIMIT_BYTES),
        name="dense_ffn",
    )(x, w_gate.astype(BF16), w_up.astype(BF16), w_down.astype(BF16), ln_g[None], ln_b[None])


META_E1, META_E2, META_W1, META_W2, META_P1, META_P2 = range(6)


def _odd_mixer_kernel(x_ref, win_ref, ccw_ref, sng_ref, snb_ref, sw_ref, sb_ref, wout_ref,
                      g1_ref, b1_ref, rhi_ref, rlo_ref, o_ref, meta_ref, cnt_ref,
                      extc_ref, mixin_ref):
    s = pl.program_id(1)
    ts = TS_MIX

    @pl.when(s == 0)
    def _():
        extc_ref[0:HALO_C, :] = jnp.zeros((HALO_C, D_C), F32)

    tri = (lax.broadcasted_iota(jnp.int32, (SGU_BLOCK, SGU_BLOCK), 0)
           >= lax.broadcasted_iota(jnp.int32, (SGU_BLOCK, SGU_BLOCK), 1))
    sgu_w = [jnp.where(tri, sw_ref[hd], 0.0).astype(BF16) for hd in range(SGU_HEADS)]

    chain_rows = ts // MIX_CHAINS
    blocks = chain_rows // SGU_BLOCK
    x1_parts = []
    for c in range(MIX_CHAINS):
        c0, c1 = c * chain_rows, (c + 1) * chain_rows
        x = x_ref[c0:c1, :]
        h = _dot(x.astype(BF16), win_ref[...])

        extc_ref[HALO_C + c0:HALO_C + c1, :] = h[:, D_C:2 * D_C] * h[:, 2 * D_C:3 * D_C]
        conv = ccw_ref[0:1, :] * extc_ref[HALO_C - 2 + c0:HALO_C - 2 + c1, :]
        conv = conv + ccw_ref[1:2, :] * extc_ref[HALO_C - 1 + c0:HALO_C - 1 + c1, :]
        conv = conv + ccw_ref[2:3, :] * extc_ref[HALO_C + c0:HALO_C + c1, :]
        mixin_ref[c0:c1, 0:D_C] = (h[:, 0:D_C] * conv).astype(BF16)

        z = jax.nn.gelu(h[:, 3 * D_C:])
        u = z[:, :D_D]
        v = _layer_norm(z[:, D_D:], sng_ref[...], snb_ref[...]).astype(BF16)
        for hd in range(SGU_HEADS):
            lo, hi = hd * SGU_HEAD_DIM, (hd + 1) * SGU_HEAD_DIM
            v_blocks = jnp.concatenate(
                [v[blk * SGU_BLOCK:(blk + 1) * SGU_BLOCK, lo:hi] for blk in range(blocks)], axis=1)
            mixed = _dot(sgu_w[hd], v_blocks) + sb_ref[:, hd:hd + 1]
            for blk in range(blocks):
                r0, r1 = blk * SGU_BLOCK, (blk + 1) * SGU_BLOCK
                gated = u[r0:r1, lo:hi] * mixed[:, blk * SGU_HEAD_DIM:(blk + 1) * SGU_HEAD_DIM]
                mixin_ref[c0 + r0:c0 + r1, D_C + lo:D_C + hi] = gated.astype(BF16)

        mix = _dot(mixin_ref[c0:c1, :], wout_ref[...])
        x1_c = _layer_norm(ALPHA * x + mix, g1_ref[...], b1_ref[...])
        o_ref[c0:c1, :] = x1_c
        x1_parts.append(x1_c)
    x1 = jnp.concatenate(x1_parts, axis=0)

    extc_ref[0:HALO_C, :] = extc_ref[ts:ts + HALO_C, :]

    x_hi = x1.astype(BF16)
    x_lo = (x1 - x_hi.astype(F32)).astype(BF16)
    logits = _dot(x_hi, rhi_ref[...]) + _dot(x_lo, rhi_ref[...]) + _dot(x_hi, rlo_ref[...])

    lane = lax.broadcasted_iota(jnp.int32, (ts, LANES), 1).astype(F32)
    neg_inf = jnp.float32(-jnp.inf)
    lg = jnp.where(lane < N_EXPERTS, logits, neg_inf)
    m1 = jnp.max(lg, axis=1, keepdims=True)
    e1 = jnp.min(jnp.where(lg == m1, lane, float(LANES)), axis=1, keepdims=True)
    lg2 = jnp.where(lane == e1, neg_inf, lg)
    m2 = jnp.max(lg2, axis=1, keepdims=True)
    e2 = jnp.min(jnp.where(lg2 == m2, lane, float(LANES)), axis=1, keepdims=True)
    ex = jnp.exp(m2 - m1)
    den = 1.0 + ex
    w1 = 1.0 / den
    w2 = ex / den

    onehot = jnp.where(jnp.logical_or(lane == e1, lane == e2), 1.0, 0.0)
    onehot_b = onehot.astype(BF16)
    before = (lax.broadcasted_iota(jnp.int32, (ts, ts), 0)
              > lax.broadcasted_iota(jnp.int32, (ts, ts), 1))
    lower = (lax.broadcasted_iota(jnp.int32, (LANES, LANES), 0)
             < lax.broadcasted_iota(jnp.int32, (LANES, LANES), 1))
    earlier = _dot(jnp.where(before, 1.0, 0.0).astype(BF16), onehot_b)
    run = jnp.sum(onehot, axis=0, keepdims=True)
    run = jnp.floor((run + (SUBLANES - 1)) * (1.0 / SUBLANES)) * SUBLANES
    run_rows = jnp.broadcast_to(run, (SUBLANES, LANES)).astype(BF16)
    lower_rows = _dot(run_rows, jnp.where(lower, 1.0, 0.0).astype(BF16))[0:1, :]
    place = earlier + lower_rows
    p1 = jnp.sum(jnp.where(lane == e1, place, 0.0), axis=1, keepdims=True)
    p2 = jnp.sum(jnp.where(lane == e2, place, 0.0), axis=1, keepdims=True)

    meta = jnp.zeros((ts, LANES), F32)
    for col, val in ((META_E1, e1), (META_E2, e2), (META_W1, w1), (META_W2, w2),
                     (META_P1, p1), (META_P2, p2)):
        meta = jnp.where(lane == col, val, meta)
    meta_ref[...] = meta
    cnt_ref[...] = jnp.broadcast_to(run, (SUBLANES, LANES))


def _odd_mixer(x, w_in, conv_w, sgu_norm_g, sgu_norm_b, sgu_w, sgu_b, w_out, ln_g, ln_b, router):
    ts = TS_MIX
    x_spec = pl.BlockSpec((None, ts, D_MODEL), lambda b, s: (b, s, 0))
    router_pad = jnp.zeros((D_MODEL, LANES), F32).at[:, :N_EXPERTS].set(router)
    router_hi = router_pad.astype(BF16)
    router_lo = (router_pad - router_hi.astype(F32)).astype(BF16)
    return pl.pallas_call(
        _odd_mixer_kernel,
        grid=(BATCH, SEQ // ts),
        in_specs=[
            x_spec,
            _const_spec((D_MODEL, IN_ODD)),
            _const_spec((CONV_C_WIDTH, D_C)),
            _const_spec((1, D_D)),
            _const_spec((1, D_D)),
            _const_spec((SGU_HEADS, SGU_BLOCK, SGU_BLOCK)),
            _const_spec((SGU_BLOCK, SGU_HEADS)),
            _const_spec((D_C + D_D, D_MODEL)),
            _const_spec((1, D_MODEL)),
            _const_spec((1, D_MODEL)),
            _const_spec((D_MODEL, LANES)),
            _const_spec((D_MODEL, LANES)),
        ],
        out_specs=[
            x_spec,
            pl.BlockSpec((ts, LANES), lambda b, s: (b * (SEQ // ts) + s, 0)),
            pl.BlockSpec((None, SUBLANES, LANES), lambda b, s: (b * (SEQ // ts) + s, 0, 0)),
        ],
        out_shape=[
            jax.ShapeDtypeStruct((BATCH, SEQ, D_MODEL), F32),
            jax.ShapeDtypeStruct((TOKENS, LANES), F32),
            jax.ShapeDtypeStruct((N_TOKEN_TILES, SUBLANES, LANES), F32),
        ],
        scratch_shapes=[
            pltpu.VMEM((HALO_C + ts, D_C), F32),
            pltpu.VMEM((ts, D_C + D_D), BF16),
        ],
        compiler_params=pltpu.CompilerParams(
            dimension_semantics=("arbitrary", "arbitrary"), vmem_limit_bytes=VMEM_LIMIT_BYTES),
        name="odd_mixer",
    )(x, w_in.astype(BF16), conv_w, sgu_norm_g[None], sgu_norm_b[None], sgu_w, sgu_b.T,
      w_out.astype(BF16), ln_g[None], ln_b[None], router_hi, router_lo)


SEGMENT_BITS = range(SUBLANES.bit_length() - 1, TS_MIX.bit_length())


def _for_each_segment_chunk(seg_ref, tile, visit):
    for e in range(N_EXPERTS):
        k = tile * N_EXPERTS + e
        n = seg_ref[k]
        local = seg_ref[N_SEGMENTS + k]
        sorted_row = seg_ref[2 * N_SEGMENTS + k]
        for bit in reversed(SEGMENT_BITS):
            @pl.when(((n >> bit) & 1) == 1)
            def _():
                done = (n >> (bit + 1)) << (bit + 1)
                visit(pl.multiple_of(local + done, SUBLANES), pl.multiple_of(sorted_row + done, SUBLANES),
                      1 << bit)


def _dispatch_kernel(seg_ref, pad_tile_ref, x_ref, meta_ref, xs_ref, zero_ref, perm_ref, zero_sem, perm_sem):
    i = pl.program_id(0)
    slot = i % 2

    def chunk_copy(which):
        def make(local, sorted_row, size):
            return pltpu.make_async_copy(perm_ref.at[which, pl.ds(local, size)],
                                         xs_ref.at[pl.ds(sorted_row, size)], perm_sem.at[which])
        return make

    def wait_tile(tile, which):
        _for_each_segment_chunk(seg_ref, tile, lambda *chunk: chunk_copy(which)(*chunk).wait())

    @pl.when(i == 0)
    def _():
        zero_ref[...] = jnp.zeros(zero_ref.shape, F32)

        def clear_tile(t):
            start = pl.multiple_of(t * TM_MOE, TM_MOE)
            cp = pltpu.make_async_copy(zero_ref, xs_ref.at[pl.ds(start, TM_MOE)], zero_sem)
            cp.start()
            cp.wait()

        for e in range(N_EXPERTS):
            @pl.when(pad_tile_ref[e] >= 0)
            def _():
                clear_tile(pad_tile_ref[e])

        def clear_tail(t, carry):
            clear_tile(t)
            return carry

        lax.fori_loop(pad_tile_ref[N_EXPERTS], N_TILES_MOE, clear_tail, 0)

    meta = meta_ref[...]
    col = lax.broadcasted_iota(jnp.int32, (TS_MIX, SLOT_ROWS), 1).astype(F32)
    hit = jnp.logical_or(col == meta[:, META_P1:META_P1 + 1], col == meta[:, META_P2:META_P2 + 1])
    onehot = jnp.where(hit, 1.0, 0.0).astype(BF16)
    permuted = lax.dot_general(onehot, x_ref[...].astype(BF16), (((0,), (0,)), ((), ())),
                               preferred_element_type=F32)

    @pl.when(i >= 2)
    def _():
        wait_tile(i - 2, slot)

    perm_ref[slot] = permuted
    _for_each_segment_chunk(seg_ref, i, lambda *chunk: chunk_copy(slot)(*chunk).start())

    @pl.when(i == N_TOKEN_TILES - 1)
    def _():
        wait_tile(i - 1, 1 - slot)
        wait_tile(i, slot)


def _dispatch(x, meta, seg, pad_tile):
    return pl.pallas_call(
        _dispatch_kernel,
        grid_spec=pltpu.PrefetchScalarGridSpec(
            num_scalar_prefetch=2,
            grid=(N_TOKEN_TILES,),
            in_specs=[pl.BlockSpec((TS_MIX, D_MODEL), lambda i, *_: (i, 0)),
                      pl.BlockSpec((TS_MIX, LANES), lambda i, *_: (i, 0))],
            out_specs=pl.BlockSpec(memory_space=pl.ANY),
            scratch_shapes=[
                pltpu.VMEM((TM_MOE, D_MODEL), F32),
                pltpu.VMEM((2, SLOT_ROWS, D_MODEL), F32),
                pltpu.SemaphoreType.DMA(()),
                pltpu.SemaphoreType.DMA((2,)),
            ],
        ),
        out_shape=jax.ShapeDtypeStruct((ROWS_SORTED, D_MODEL), F32),
        compiler_params=pltpu.CompilerParams(
            dimension_semantics=("arbitrary",), vmem_limit_bytes=VMEM_LIMIT_BYTES),
        name="moe_dispatch",
    )(seg, pad_tile, x, meta)


def _grouped_ffn_kernel(tile_expert_ref, tile_rows_ref, n_active_ref, xs_ref, wg_ref, wu_ref, wd_ref,
                        ys_ref, xb_ref):
    i = pl.program_id(0)
    j = pl.program_id(1)

    active = i < n_active_ref[0]

    @pl.when(j == 0)
    def _():
        ys_ref[...] = jnp.zeros((TM_MOE, D_MODEL), F32)

    @pl.when(jnp.logical_and(active, j == 0))
    def _():
        xb_ref[...] = xs_ref[...].astype(BF16)

    def accumulate(n_rows):
        xb = xb_ref[0:n_rows, :]
        a = _silu(_dot(xb, wg_ref[...].astype(BF16))) * _dot(xb, wu_ref[...].astype(BF16))
        ys_ref[0:n_rows, :] += _dot(a.astype(BF16), wd_ref[...].astype(BF16))

    for n_sub in range(1, TM_MOE // SUB_MOE + 1):
        lo, hi = (n_sub - 1) * SUB_MOE, n_sub * SUB_MOE

        @pl.when(jnp.logical_and(active, jnp.logical_and(tile_rows_ref[i] > lo, tile_rows_ref[i] <= hi)))
        def _():
            accumulate(hi)


def _grouped_ffn(xs, w_gate, w_up, w_down, tile_expert, tile_rows, n_active):
    nj = D_FF_EXPERT // TF_MOE

    def row_map(i, j, te, tr, na):
        return (jnp.minimum(i, na[0] - 1), 0)

    def frozen_j(i, j, na):
        return jnp.where(i < na[0], j, nj - 1)

    def up_map(i, j, te, tr, na):
        return (te[i], 0, frozen_j(i, j, na))

    def down_map(i, j, te, tr, na):
        return (te[i], frozen_j(i, j, na), 0)

    return pl.pallas_call(
        _grouped_ffn_kernel,
        grid_spec=pltpu.PrefetchScalarGridSpec(
            num_scalar_prefetch=3,
            grid=(N_TILES_MOE, nj),
            in_specs=[
                pl.BlockSpec((TM_MOE, D_MODEL), row_map),
                pl.BlockSpec((None, D_MODEL, TF_MOE), up_map),
                pl.BlockSpec((None, D_MODEL, TF_MOE), up_map),
                pl.BlockSpec((None, TF_MOE, D_MODEL), down_map),
            ],
            out_specs=pl.BlockSpec((TM_MOE, D_MODEL), lambda i, j, *_: (i, 0)),
            scratch_shapes=[pltpu.VMEM((TM_MOE, D_MODEL), BF16)],
        ),
        out_shape=jax.ShapeDtypeStruct((ROWS_SORTED, D_MODEL), F32),
        compiler_params=pltpu.CompilerParams(
            dimension_semantics=("arbitrary", "arbitrary"), vmem_limit_bytes=VMEM_LIMIT_BYTES),
        name="moe_grouped_ffn",
    )(tile_expert, tile_rows, n_active, xs, w_gate, w_up, w_down)


def _combine_kernel(seg_ref, x_ref, meta_ref, ys_ref, g_ref, b_ref, o_ref, win_ref, sem):
    i = pl.program_id(0)
    slot = i % 2

    def chunk_copy(which):
        def make(local, sorted_row, size):
            return pltpu.make_async_copy(ys_ref.at[pl.ds(sorted_row, size)],
                                         win_ref.at[which, pl.ds(local, size)], sem.at[which])
        return make

    def fetch(tile, which):
        _for_each_segment_chunk(seg_ref, tile, lambda *chunk: chunk_copy(which)(*chunk).start())

    @pl.when(i == 0)
    def _():
        win_ref[...] = jnp.zeros(win_ref.shape, F32)
        fetch(0, 0)

    @pl.when(i + 1 < N_TOKEN_TILES)
    def _():
        fetch(i + 1, 1 - slot)

    _for_each_segment_chunk(seg_ref, i, lambda *chunk: chunk_copy(slot)(*chunk).wait())

    meta = meta_ref[...]
    col = lax.broadcasted_iota(jnp.int32, (TS_MIX, SLOT_ROWS), 1).astype(F32)
    gate = (jnp.where(col == meta[:, META_P1:META_P1 + 1], meta[:, META_W1:META_W1 + 1], 0.0)
            + jnp.where(col == meta[:, META_P2:META_P2 + 1], meta[:, META_W2:META_W2 + 1], 0.0))
    moe = _dot(gate.astype(BF16), win_ref[slot].astype(BF16))
    o_ref[...] = _layer_norm(ALPHA * x_ref[...] + moe, g_ref[...], b_ref[...])


def _combine(x, meta, ys, seg, ln_g, ln_b):
    row_spec = pl.BlockSpec((TS_MIX, D_MODEL), lambda i, *_: (i, 0))
    return pl.pallas_call(
        _combine_kernel,
        grid_spec=pltpu.PrefetchScalarGridSpec(
            num_scalar_prefetch=1,
            grid=(N_TOKEN_TILES,),
            in_specs=[
                row_spec,
                pl.BlockSpec((TS_MIX, LANES), lambda i, *_: (i, 0)),
                pl.BlockSpec(memory_space=pl.ANY),
                pl.BlockSpec((1, D_MODEL), lambda i, *_: (0, 0)),
                pl.BlockSpec((1, D_MODEL), lambda i, *_: (0, 0)),
            ],
            out_specs=row_spec,
            scratch_shapes=[
                pltpu.VMEM((2, SLOT_ROWS, D_MODEL), F32),
                pltpu.SemaphoreType.DMA((2,)),
            ],
        ),
        out_shape=jax.ShapeDtypeStruct((TOKENS, D_MODEL), F32),
        compiler_params=pltpu.CompilerParams(
            dimension_semantics=("arbitrary",), vmem_limit_bytes=VMEM_LIMIT_BYTES),
        name="moe_combine",
    )(seg, x, meta, ys, ln_g[None], ln_b[None])


def _routing_tables(tile_counts):
    n = tile_counts[:, 0, :N_EXPERTS].astype(jnp.int32)
    counts = jnp.sum(n, axis=0)
    tiles = (counts + TM_MOE - 1) // TM_MOE
    tile_end = jnp.cumsum(tiles)
    tile_start = tile_end - tiles
    offset = tile_start * TM_MOE
    local_start = jnp.cumsum(n, axis=1) - n
    sorted_start = offset[None, :] + jnp.cumsum(n, axis=0) - n
    seg = jnp.concatenate([n.reshape(-1), local_start.reshape(-1), sorted_start.reshape(-1)])
    n_active = tile_end[-1:]
    tile_id = jnp.minimum(jnp.arange(N_TILES_MOE, dtype=jnp.int32), n_active[0] - 1)
    tile_expert = jnp.sum(tile_id[:, None] >= tile_end[None, :], axis=1).astype(jnp.int32)
    tile_rows = jnp.clip(counts[tile_expert] - (tile_id - tile_start[tile_expert]) * TM_MOE, 0, TM_MOE)
    pad_tile = jnp.concatenate([jnp.where(tiles > 0, tile_end - 1, -1), n_active]).astype(jnp.int32)
    return (seg.astype(jnp.int32), pad_tile, tile_expert, tile_rows.astype(jnp.int32),
            n_active.astype(jnp.int32))


def _even_layer(x, w_in, conv_a_w, conv_a_b, norm_a_g, norm_a_b, pool_w, pool_scale, w_out,
                ln1_g, ln1_b, ffn_w_gate, ffn_w_up, ffn_w_down, ln2_g, ln2_b):
    x = _even_mixer(x, w_in, conv_a_w, conv_a_b, norm_a_g, norm_a_b, pool_w, pool_scale, w_out,
                    ln1_g, ln1_b)
    x = _dense_ffn(x.reshape(TOKENS, D_MODEL), ffn_w_gate, ffn_w_up, ffn_w_down, ln2_g, ln2_b)
    return x.reshape(BATCH, SEQ, D_MODEL)


def _odd_layer(x, w_in, conv_c_w, sgu_norm_g, sgu_norm_b, sgu_w, sgu_b, w_out, ln1_g, ln1_b,
               router, moe_w_gate, moe_w_up, moe_w_down, ln2_g, ln2_b):
    x, meta, tile_counts = _odd_mixer(x, w_in, conv_c_w, sgu_norm_g, sgu_norm_b, sgu_w, sgu_b, w_out,
                                      ln1_g, ln1_b, router)
    x = x.reshape(TOKENS, D_MODEL)
    seg, pad_tile, tile_expert, tile_rows, n_active = _routing_tables(tile_counts)
    xs = _dispatch(x, meta, seg, pad_tile)
    ys = _grouped_ffn(xs, moe_w_gate, moe_w_up, moe_w_down, tile_expert, tile_rows, n_active)
    x = _combine(x, meta, ys, seg, ln2_g, ln2_b)
    return x.reshape(BATCH, SEQ, D_MODEL)


def kernel(x, even_w_in, even_conv_a_w, even_conv_a_b, even_norm_a_g, even_norm_a_b, even_pool_w, even_pool_scale, even_w_out, even_ln1_g, even_ln1_b, even_ffn_w_gate, even_ffn_w_up, even_ffn_w_down, even_ln2_g, even_ln2_b, odd_w_in, odd_conv_c_w, odd_sgu_norm_g, odd_sgu_norm_b, odd_sgu_w, odd_sgu_b, odd_w_out, odd_ln1_g, odd_ln1_b, odd_router, odd_moe_w_gate, odd_moe_w_up, odd_moe_w_down, odd_ln2_g, odd_ln2_b):
    for layer in range(DEPTH):
        i = layer // 2
        if layer % 2 == 0:
            x = _even_layer(x, even_w_in[i], even_conv_a_w[i], even_conv_a_b[i], even_norm_a_g[i],
                            even_norm_a_b[i], even_pool_w[i], even_pool_scale[i], even_w_out[i],
                            even_ln1_g[i], even_ln1_b[i], even_ffn_w_gate[i], even_ffn_w_up[i],
                            even_ffn_w_down[i], even_ln2_g[i], even_ln2_b[i])
        else:
            x = _odd_layer(x, odd_w_in[i], odd_conv_c_w[i], odd_sgu_norm_g[i], odd_sgu_norm_b[i],
                           odd_sgu_w[i], odd_sgu_b[i], odd_w_out[i], odd_ln1_g[i], odd_ln1_b[i],
                           odd_router[i], odd_moe_w_gate[i], odd_moe_w_up[i], odd_moe_w_down[i],
                           odd_ln2_g[i], odd_ln2_b[i])
    return x
```

```python
import functools

import jax
import jax.numpy as jnp
from jax import lax
from jax.experimental import pallas as pl
from jax.experimental.pallas import tpu as pltpu

F32 = jnp.float32
BF16 = jnp.bfloat16

D_MODEL = 1024
BATCH = 4
SEQ = 4096
TOKENS = BATCH * SEQ
DEPTH = 2

D_A = 512
CONV_A_WIDTH = 31
D_B = 512
POOL_WINDOWS = (2, 4, 8, 16)
POOL_GROUP_DIM = 128
IN_EVEN = 2 * D_A + D_B

D_C = 512
CONV_C_WIDTH = 3
D_D = 512
SGU_BLOCK = 128
SGU_HEADS = 4
SGU_HEAD_DIM = 128
IN_ODD = 3 * D_C + 2 * D_D

D_FF_DENSE = 2816
N_EXPERTS = 8
D_FF_EXPERT = 3584

ALPHA = (2 * DEPTH) ** 0.25
LN_EPS = 1e-5

SUBLANES = 8
LANES = 128
VMEM_LIMIT_BYTES = 56 * 1024 * 1024

TS_MIX = 512
MIX_CHAINS = 2
ROW_SET_STRIDE = 4
CONV_SETS = 8
HALO_A = 32
HALO_B = 16
HALO_C = 8
TM_FFN = 1024
FF_CHUNK = 512
TM_MOE = 1024
SUB_MOE = 256
TF_MOE = 512
N_TOKEN_TILES = TOKENS // TS_MIX
N_SEGMENTS = N_TOKEN_TILES * N_EXPERTS
RUN_ALIGN = 16
SLOT_ROWS = 2 * TS_MIX + N_EXPERTS * RUN_ALIGN
MAX_SORTED_ROWS = 2 * TOKENS + N_SEGMENTS * (RUN_ALIGN - 1)
N_TILES_MOE = (MAX_SORTED_ROWS + N_EXPERTS * (TM_MOE - 1)) // TM_MOE
ROWS_SORTED = N_TILES_MOE * TM_MOE


def _layer_norm(x, g, b):
    mu = jnp.mean(x, axis=-1, keepdims=True)
    xc = x - mu
    var = jnp.mean(xc * xc, axis=-1, keepdims=True)
    return xc * lax.rsqrt(var + LN_EPS) * g + b


def _silu(x):
    return x * jax.nn.sigmoid(x)


def _dot(a, b):
    return jnp.dot(a, b, preferred_element_type=F32)


def _even_mixer_kernel(x_ref, win_ref, cw_ref, cb_ref, nag_ref, nab_ref, pw_ref, ps_ref,
                       wout_ref, g1_ref, b1_ref, o_ref, exta_ref, extb_ref, mixf_ref, mixin_ref):
    s = pl.program_id(1)
    ts = TS_MIX

    n_a, n_b = D_A // LANES, D_B // LANES

    @pl.when(s == 0)
    def _():
        exta_ref[:, 0:HALO_A, :] = jnp.zeros((n_a, HALO_A, LANES), F32)
        extb_ref[:, 0:HALO_B, :] = jnp.zeros((n_b, HALO_B, LANES), F32)

    x = x_ref[...]
    h = _dot(x.astype(BF16), win_ref[...])
    glu = h[:, :D_A] * jax.nn.sigmoid(h[:, D_A:2 * D_A])
    for cb in range(n_a):
        exta_ref[cb, HALO_A:HALO_A + ts, :] = glu[:, cb * LANES:(cb + 1) * LANES]
    for cb in range(n_b):
        extb_ref[cb, HALO_B:HALO_B + ts, :] = h[:, 2 * D_A + cb * LANES:2 * D_A + (cb + 1) * LANES]

    def row_set(first):
        return pl.ds(first, SUBLANES, stride=ROW_SET_STRIDE)

    set_firsts = [blk * SUBLANES * ROW_SET_STRIDE + r
                  for blk in range(ts // (SUBLANES * ROW_SET_STRIDE)) for r in range(ROW_SET_STRIDE)]
    for cb in range(n_a):
        lanes = slice(cb * LANES, (cb + 1) * LANES)
        for group in range(0, len(set_firsts), CONV_SETS):
            firsts = set_firsts[group:group + CONV_SETS]
            accs = [jnp.broadcast_to(cb_ref[:, lanes], (SUBLANES, LANES))] * len(firsts)
            for k in range(CONV_A_WIDTH):
                wk = jnp.broadcast_to(cw_ref[k:k + 1, lanes], (SUBLANES, LANES))
                back = CONV_A_WIDTH - 1 - k
                for n, first in enumerate(firsts):
                    accs[n] = accs[n] + wk * exta_ref[cb, row_set(HALO_A + first - back), :]
            for n, first in enumerate(firsts):
                mixf_ref[cb, row_set(first), :] = accs[n]

    set_step = ROW_SET_STRIDE * lax.broadcasted_iota(jnp.int32, (SUBLANES, 1), 0)
    for first in set_firsts:
        row_pos = s * ts + first + set_step
        for g, win in enumerate(POOL_WINDOWS):
            frame = extb_ref[g, row_set(HALO_B + first), :]
            wsum = frame
            for i in range(1, win):
                wsum = wsum + extb_ref[g, row_set(HALO_B + first - i), :]
            if first >= win - 1:
                mean = wsum * (1.0 / win)
            else:
                mean = wsum / jnp.minimum(row_pos + 1, win).astype(F32)
            mixf_ref[n_a + g, row_set(first), :] = mean - frame

    norm_rows = 128
    for r0 in range(0, ts, norm_rows):
        conv = jnp.concatenate([mixf_ref[cb, r0:r0 + norm_rows, :] for cb in range(n_a)], axis=1)
        y = _silu(_layer_norm(conv, nag_ref[...], nab_ref[...]))
        mixin_ref[r0:r0 + norm_rows, 0:D_A] = y.astype(BF16)
    for g in range(n_b):
        lo, hi = g * POOL_GROUP_DIM, (g + 1) * POOL_GROUP_DIM
        mixed = _dot(mixf_ref[n_a + g].astype(BF16), pw_ref[g]) * ps_ref[:, lo:hi]
        mixin_ref[:, D_A + lo:D_A + hi] = mixed.astype(BF16)

    mix = _dot(mixin_ref[...], wout_ref[...])
    o_ref[...] = _layer_norm(ALPHA * x + mix, g1_ref[...], b1_ref[...])

    exta_ref[:, 0:HALO_A, :] = exta_ref[:, ts:ts + HALO_A, :]
    extb_ref[:, 0:HALO_B, :] = extb_ref[:, ts:ts + HALO_B, :]


def _const_spec(shape):
    return pl.BlockSpec(shape, lambda *_: (0,) * len(shape))


def _even_mixer(x, w_in, conv_w, conv_b, norm_g, norm_b, pool_w, pool_scale, w_out, ln_g, ln_b):
    ts = TS_MIX
    x_spec = pl.BlockSpec((None, ts, D_MODEL), lambda b, s: (b, s, 0))
    return pl.pallas_call(
        _even_mixer_kernel,
        grid=(BATCH, SEQ // ts),
        in_specs=[
            x_spec,
            _const_spec((D_MODEL, IN_EVEN)),
            _const_spec((CONV_A_WIDTH, D_A)),
            _const_spec((1, D_A)),
            _const_spec((1, D_A)),
            _const_spec((1, D_A)),
            _const_spec((len(POOL_WINDOWS), POOL_GROUP_DIM, POOL_GROUP_DIM)),
            _const_spec((1, D_B)),
            _const_spec((D_A + D_B, D_MODEL)),
            _const_spec((1, D_MODEL)),
            _const_spec((1, D_MODEL)),
        ],
        out_specs=x_spec,
        out_shape=jax.ShapeDtypeStruct((BATCH, SEQ, D_MODEL), F32),
        scratch_shapes=[
            pltpu.VMEM((D_A // LANES, HALO_A + ts, LANES), F32),
            pltpu.VMEM((D_B // LANES, HALO_B + ts, LANES), F32),
            pltpu.VMEM(((D_A + D_B) // LANES, ts, LANES), F32),
            pltpu.VMEM((ts, D_A + D_B), BF16),
        ],
        compiler_params=pltpu.CompilerParams(
            dimension_semantics=("arbitrary", "arbitrary"), vmem_limit_bytes=VMEM_LIMIT_BYTES),
        name="even_mixer",
    )(x, w_in.astype(BF16), conv_w, conv_b[None], norm_g[None], norm_b[None],
      pool_w.astype(BF16), pool_scale[None], w_out.astype(BF16), ln_g[None], ln_b[None])


def _ff_chunks(total, chunk):
    bounds = list(range(0, total, chunk)) + [total]
    return list(zip(bounds[:-1], bounds[1:]))


def _dense_ffn_kernel(x_ref, wg_ref, wu_ref, wd_ref, g_ref, b_ref, o_ref, acc_ref):
    x = x_ref[...]
    xb = x.astype(BF16)
    for n, (lo, hi) in enumerate(_ff_chunks(D_FF_DENSE, FF_CHUNK)):
        a = _silu(_dot(xb, wg_ref[:, lo:hi])) * _dot(xb, wu_ref[:, lo:hi])
        part = _dot(a.astype(BF16), wd_ref[lo:hi, :])
        if n == 0:
            acc_ref[...] = part
        else:
            acc_ref[...] += part
    o_ref[...] = _layer_norm(ALPHA * x + acc_ref[...], g_ref[...], b_ref[...])


def _dense_ffn(x, w_gate, w_up, w_down, ln_g, ln_b):
    tm = TM_FFN
    row_spec = pl.BlockSpec((tm, D_MODEL), lambda i: (i, 0))
    resident = pl.BlockSpec(memory_space=pltpu.VMEM)
    return pl.pallas_call(
        _dense_ffn_kernel,
        grid=(TOKENS // tm,),
        in_specs=[row_spec, resident, resident, resident,
                  _const_spec((1, D_MODEL)), _const_spec((1, D_MODEL))],
        out_specs=row_spec,
        out_shape=jax.ShapeDtypeStruct((TOKENS, D_MODEL), F32),
        scratch_shapes=[pltpu.VMEM((tm, D_MODEL), F32)],
        compiler_params=pltpu.CompilerParams(
            dimension_semantics=("arbitrary",), vmem_limit_bytes=VMEM_LIMIT_BYTES),
        name="dense_ffn",
    )(x, w_gate.astype(BF16), w_up.astype(BF16), w_down.astype(BF16), ln_g[None], ln_b[None])


META_E1, META_E2, META_W1, META_W2, META_P1, META_P2 = range(6)


def _odd_mixer_kernel(x_ref, win_ref, ccw_ref, sng_ref, snb_ref, sw_ref, sb_ref, wout_ref,
                      g1_ref, b1_ref, rhi_ref, rlo_ref, o_ref, meta_ref, cnt_ref,
                      extc_ref, mixin_ref):
    s = pl.program_id(1)
    ts = TS_MIX

    @pl.when(s == 0)
    def _():
        extc_ref[0:HALO_C, :] = jnp.zeros((HALO_C, D_C), F32)

    tri = (lax.broadcasted_iota(jnp.int32, (SGU_BLOCK, SGU_BLOCK), 0)
           >= lax.broadcasted_iota(jnp.int32, (SGU_BLOCK, SGU_BLOCK), 1))
    sgu_w = [jnp.where(tri, sw_ref[hd], 0.0).astype(BF16) for hd in range(SGU_HEADS)]

    chain_rows = ts // MIX_CHAINS
    blocks = chain_rows // SGU_BLOCK
    x1_parts = []
    for c in range(MIX_CHAINS):
        c0, c1 = c * chain_rows, (c + 1) * chain_rows
        x = x_ref[c0:c1, :]
        h = _dot(x.astype(BF16), win_ref[...])

        extc_ref[HALO_C + c0:HALO_C + c1, :] = h[:, D_C:2 * D_C] * h[:, 2 * D_C:3 * D_C]
        conv = ccw_ref[0:1, :] * extc_ref[HALO_C - 2 + c0:HALO_C - 2 + c1, :]
        conv = conv + ccw_ref[1:2, :] * extc_ref[HALO_C - 1 + c0:HALO_C - 1 + c1, :]
        conv = conv + ccw_ref[2:3, :] * extc_ref[HALO_C + c0:HALO_C + c1, :]
        mixin_ref[c0:c1, 0:D_C] = (h[:, 0:D_C] * conv).astype(BF16)

        z = jax.nn.gelu(h[:, 3 * D_C:])
        u = z[:, :D_D]
        v = _layer_norm(z[:, D_D:], sng_ref[...], snb_ref[...]).astype(BF16)
        for hd in range(SGU_HEADS):
            lo, hi = hd * SGU_HEAD_DIM, (hd + 1) * SGU_HEAD_DIM
            v_blocks = jnp.concatenate(
                [v[blk * SGU_BLOCK:(blk + 1) * SGU_BLOCK, lo:hi] for blk in range(blocks)], axis=1)
            mixed = _dot(sgu_w[hd], v_blocks) + sb_ref[:, hd:hd + 1]
            for blk in range(blocks):
                r0, r1 = blk * SGU_BLOCK, (blk + 1) * SGU_BLOCK
                gated = u[r0:r1, lo:hi] * mixed[:, blk * SGU_HEAD_DIM:(blk + 1) * SGU_HEAD_DIM]
                mixin_ref[c0 + r0:c0 + r1, D_C + lo:D_C + hi] = gated.astype(BF16)

        mix = _dot(mixin_ref[c0:c1, :], wout_ref[...])
        x1_c = _layer_norm(ALPHA * x + mix, g1_ref[...], b1_ref[...])
        o_ref[c0:c1, :] = x1_c
        x1_parts.append(x1_c)
    x1 = jnp.concatenate(x1_parts, axis=0)

    extc_ref[0:HALO_C, :] = extc_ref[ts:ts + HALO_C, :]

    x_hi = x1.astype(BF16)
    x_lo = (x1 - x_hi.astype(F32)).astype(BF16)
    logits = _dot(x_hi, rhi_ref[...]) + _dot(x_lo, rhi_ref[...]) + _dot(x_hi, rlo_ref[...])

    lane = lax.broadcasted_iota(jnp.int32, (ts, LANES), 1).astype(F32)
    neg_inf = jnp.float32(-jnp.inf)
    lg = jnp.where(lane < N_EXPERTS, logits, neg_inf)
    m1 = jnp.max(lg, axis=1, keepdims=True)
    e1 = jnp.min(jnp.where(lg == m1, lane, float(LANES)), axis=1, keepdims=True)
    lg2 = jnp.where(lane == e1, neg_inf, lg)
    m2 = jnp.max(lg2, axis=1, keepdims=True)
    e2 = jnp.min(jnp.where(lg2 == m2, lane, float(LANES)), axis=1, keepdims=True)
    ex = jnp.exp(m2 - m1)
    den = 1.0 + ex
    w1 = 1.0 / den
    w2 = ex / den

    onehot = jnp.where(jnp.logical_or(lane == e1, lane == e2), 1.0, 0.0)
    onehot_b = onehot.astype(BF16)
    before = (lax.broadcasted_iota(jnp.int32, (ts, ts), 0)
              > lax.broadcasted_iota(jnp.int32, (ts, ts), 1))
    lower = (lax.broadcasted_iota(jnp.int32, (LANES, LANES), 0)
             < lax.broadcasted_iota(jnp.int32, (LANES, LANES), 1))
    earlier = _dot(jnp.where(before, 1.0, 0.0).astype(BF16), onehot_b)
    run = jnp.sum(onehot, axis=0, keepdims=True)
    run = jnp.floor((run + (RUN_ALIGN - 1)) * (1.0 / RUN_ALIGN)) * RUN_ALIGN
    run_rows = jnp.broadcast_to(run, (SUBLANES, LANES)).astype(BF16)
    lower_rows = _dot(run_rows, jnp.where(lower, 1.0, 0.0).astype(BF16))[0:1, :]
    place = earlier + lower_rows
    p1 = jnp.sum(jnp.where(lane == e1, place, 0.0), axis=1, keepdims=True)
    p2 = jnp.sum(jnp.where(lane == e2, place, 0.0), axis=1, keepdims=True)

    meta = jnp.zeros((ts, LANES), F32)
    for col, val in ((META_E1, e1), (META_E2, e2), (META_W1, w1), (META_W2, w2),
                     (META_P1, p1), (META_P2, p2)):
        meta = jnp.where(lane == col, val, meta)
    meta_ref[...] = meta
    cnt_ref[...] = jnp.broadcast_to(run, (SUBLANES, LANES))


def _odd_mixer(x, w_in, conv_w, sgu_norm_g, sgu_norm_b, sgu_w, sgu_b, w_out, ln_g, ln_b, router):
    ts = TS_MIX
    x_spec = pl.BlockSpec((None, ts, D_MODEL), lambda b, s: (b, s, 0))
    router_pad = jnp.zeros((D_MODEL, LANES), F32).at[:, :N_EXPERTS].set(router)
    router_hi = router_pad.astype(BF16)
    router_lo = (router_pad - router_hi.astype(F32)).astype(BF16)
    return pl.pallas_call(
        _odd_mixer_kernel,
        grid=(BATCH, SEQ // ts),
        in_specs=[
            x_spec,
            _const_spec((D_MODEL, IN_ODD)),
            _const_spec((CONV_C_WIDTH, D_C)),
            _const_spec((1, D_D)),
            _const_spec((1, D_D)),
            _const_spec((SGU_HEADS, SGU_BLOCK, SGU_BLOCK)),
            _const_spec((SGU_BLOCK, SGU_HEADS)),
            _const_spec((D_C + D_D, D_MODEL)),
            _const_spec((1, D_MODEL)),
            _const_spec((1, D_MODEL)),
            _const_spec((D_MODEL, LANES)),
            _const_spec((D_MODEL, LANES)),
        ],
        out_specs=[
            x_spec,
            pl.BlockSpec((ts, LANES), lambda b, s: (b * (SEQ // ts) + s, 0)),
            pl.BlockSpec((None, SUBLANES, LANES), lambda b, s: (b * (SEQ // ts) + s, 0, 0)),
        ],
        out_shape=[
            jax.ShapeDtypeStruct((BATCH, SEQ, D_MODEL), F32),
            jax.ShapeDtypeStruct((TOKENS, LANES), F32),
            jax.ShapeDtypeStruct((N_TOKEN_TILES, SUBLANES, LANES), F32),
        ],
        scratch_shapes=[
            pltpu.VMEM((HALO_C + ts, D_C), F32),
            pltpu.VMEM((ts, D_C + D_D), BF16),
        ],
        compiler_params=pltpu.CompilerParams(
            dimension_semantics=("arbitrary", "arbitrary"), vmem_limit_bytes=VMEM_LIMIT_BYTES),
        name="odd_mixer",
    )(x, w_in.astype(BF16), conv_w, sgu_norm_g[None], sgu_norm_b[None], sgu_w, sgu_b.T,
      w_out.astype(BF16), ln_g[None], ln_b[None], router_hi, router_lo)


SEGMENT_BITS = range(RUN_ALIGN.bit_length() - 1, TS_MIX.bit_length())


def _for_each_segment_chunk(seg_ref, tile, visit):
    for e in range(N_EXPERTS):
        k = tile * N_EXPERTS + e
        n = seg_ref[k]
        local = seg_ref[N_SEGMENTS + k]
        sorted_row = seg_ref[2 * N_SEGMENTS + k]
        for bit in reversed(SEGMENT_BITS):
            @pl.when(((n >> bit) & 1) == 1)
            def _():
                done = (n >> (bit + 1)) << (bit + 1)
                visit(pl.multiple_of(local + done, RUN_ALIGN), pl.multiple_of(sorted_row + done, RUN_ALIGN),
                      1 << bit)


def _dispatch_kernel(seg_ref, pad_tile_ref, x_ref, meta_ref, xs_ref, zero_ref, perm_ref, zero_sem, perm_sem):
    i = pl.program_id(0)
    slot = i % 2

    def chunk_copy(which):
        def make(local, sorted_row, size):
            return pltpu.make_async_copy(perm_ref.at[which, pl.ds(local, size)],
                                         xs_ref.at[pl.ds(sorted_row, size)], perm_sem.at[which])
        return make

    def wait_tile(tile, which):
        _for_each_segment_chunk(seg_ref, tile, lambda *chunk: chunk_copy(which)(*chunk).wait())

    @pl.when(i == 0)
    def _():
        zero_ref[...] = jnp.zeros(zero_ref.shape, BF16)

        def clear_tile(t):
            start = pl.multiple_of(t * TM_MOE, TM_MOE)
            cp = pltpu.make_async_copy(zero_ref, xs_ref.at[pl.ds(start, TM_MOE)], zero_sem)
            cp.start()
            cp.wait()

        for e in range(N_EXPERTS):
            @pl.when(pad_tile_ref[e] >= 0)
            def _():
                clear_tile(pad_tile_ref[e])

        def clear_tail(t, carry):
            clear_tile(t)
            return carry

        lax.fori_loop(pad_tile_ref[N_EXPERTS], N_TILES_MOE, clear_tail, 0)

    meta = meta_ref[...]
    col = lax.broadcasted_iota(jnp.int32, (TS_MIX, SLOT_ROWS), 1).astype(F32)
    hit = jnp.logical_or(col == meta[:, META_P1:META_P1 + 1], col == meta[:, META_P2:META_P2 + 1])
    onehot = jnp.where(hit, 1.0, 0.0).astype(BF16)
    permuted = lax.dot_general(onehot, x_ref[...].astype(BF16), (((0,), (0,)), ((), ())),
                               preferred_element_type=F32)

    @pl.when(i >= 2)
    def _():
        wait_tile(i - 2, slot)

    perm_ref[slot] = permuted.astype(BF16)
    _for_each_segment_chunk(seg_ref, i, lambda *chunk: chunk_copy(slot)(*chunk).start())

    @pl.when(i == N_TOKEN_TILES - 1)
    def _():
        wait_tile(i - 1, 1 - slot)
        wait_tile(i, slot)


def _dispatch(x, meta, seg, pad_tile):
    return pl.pallas_call(
        _dispatch_kernel,
        grid_spec=pltpu.PrefetchScalarGridSpec(
            num_scalar_prefetch=2,
            grid=(N_TOKEN_TILES,),
            in_specs=[pl.BlockSpec((TS_MIX, D_MODEL), lambda i, *_: (i, 0)),
                      pl.BlockSpec((TS_MIX, LANES), lambda i, *_: (i, 0))],
            out_specs=pl.BlockSpec(memory_space=pl.ANY),
            scratch_shapes=[
                pltpu.VMEM((TM_MOE, D_MODEL), BF16),
                pltpu.VMEM((2, SLOT_ROWS, D_MODEL), BF16),
                pltpu.SemaphoreType.DMA(()),
                pltpu.SemaphoreType.DMA((2,)),
            ],
        ),
        out_shape=jax.ShapeDtypeStruct((ROWS_SORTED, D_MODEL), BF16),
        compiler_params=pltpu.CompilerParams(
            dimension_semantics=("arbitrary",), vmem_limit_bytes=VMEM_LIMIT_BYTES),
        name="moe_dispatch",
    )(seg, pad_tile, x, meta)


def _grouped_ffn_kernel(tile_expert_ref, tile_rows_ref, n_active_ref, xs_ref, wg_ref, wu_ref, wd_ref,
                        ys_ref, acc_ref):
    i = pl.program_id(0)
    j = pl.program_id(1)

    active = i < n_active_ref[0]
    rows = tile_rows_ref[i]

    uncovered = jnp.logical_or(jnp.logical_not(active), rows <= TM_MOE - SUB_MOE)

    @pl.when(jnp.logical_and(j == 0, jnp.logical_or(i == 0, uncovered)))
    def _():
        acc_ref[...] = jnp.zeros((TM_MOE, D_MODEL), F32)

    def accumulate(n_rows):
        xb = xs_ref[0:n_rows, :]
        a = _silu(_dot(xb, wg_ref[...].astype(BF16))) * _dot(xb, wu_ref[...].astype(BF16))
        part = _dot(a.astype(BF16), wd_ref[...].astype(BF16))
        acc_ref[0:n_rows, :] = jnp.where(j == 0, part, acc_ref[0:n_rows, :] + part)

    for n_sub in range(1, TM_MOE // SUB_MOE + 1):
        lo, hi = (n_sub - 1) * SUB_MOE, n_sub * SUB_MOE

        @pl.when(jnp.logical_and(active, jnp.logical_and(rows > lo, rows <= hi)))
        def _():
            accumulate(hi)

    @pl.when(j == pl.num_programs(1) - 1)
    def _():
        ys_ref[...] = acc_ref[...].astype(BF16)


def _grouped_ffn(xs, w_gate, w_up, w_down, tile_expert, tile_rows, n_active):
    nj = D_FF_EXPERT // TF_MOE

    def row_map(i, j, te, tr, na):
        return (jnp.minimum(i, na[0] - 1), 0)

    def frozen_j(i, j, na):
        return jnp.where(i < na[0], j, nj - 1)

    def up_map(i, j, te, tr, na):
        return (te[i], 0, frozen_j(i, j, na))

    def down_map(i, j, te, tr, na):
        return (te[i], frozen_j(i, j, na), 0)

    return pl.pallas_call(
        _grouped_ffn_kernel,
        grid_spec=pltpu.PrefetchScalarGridSpec(
            num_scalar_prefetch=3,
            grid=(N_TILES_MOE, nj),
            in_specs=[
                pl.BlockSpec((TM_MOE, D_MODEL), row_map),
                pl.BlockSpec((None, D_MODEL, TF_MOE), up_map),
                pl.BlockSpec((None, D_MODEL, TF_MOE), up_map),
                pl.BlockSpec((None, TF_MOE, D_MODEL), down_map),
            ],
            out_specs=pl.BlockSpec((TM_MOE, D_MODEL), lambda i, j, *_: (i, 0)),
            scratch_shapes=[pltpu.VMEM((TM_MOE, D_MODEL), F32)],
        ),
        out_shape=jax.ShapeDtypeStruct((ROWS_SORTED, D_MODEL), BF16),
        compiler_params=pltpu.CompilerParams(
            dimension_semantics=("arbitrary", "arbitrary"), vmem_limit_bytes=VMEM_LIMIT_BYTES),
        name="moe_grouped_ffn",
    )(tile_expert, tile_rows, n_active, xs, w_gate, w_up, w_down)


def _combine_kernel(seg_ref, x_ref, meta_ref, ys_ref, g_ref, b_ref, o_ref, win_ref, sem):
    i = pl.program_id(0)
    slot = i % 2

    def chunk_copy(which):
        def make(local, sorted_row, size):
            return pltpu.make_async_copy(ys_ref.at[pl.ds(sorted_row, size)],
                                         win_ref.at[which, pl.ds(local, size)], sem.at[which])
        return make

    def fetch(tile, which):
        _for_each_segment_chunk(seg_ref, tile, lambda *chunk: chunk_copy(which)(*chunk).start())

    @pl.when(i == 0)
    def _():
        win_ref[...] = jnp.zeros(win_ref.shape, BF16)
        fetch(0, 0)

    @pl.when(i + 1 < N_TOKEN_TILES)
    def _():
        fetch(i + 1, 1 - slot)

    _for_each_segment_chunk(seg_ref, i, lambda *chunk: chunk_copy(slot)(*chunk).wait())

    meta = meta_ref[...]
    col = lax.broadcasted_iota(jnp.int32, (TS_MIX, SLOT_ROWS), 1).astype(F32)
    gate = (jnp.where(col == meta[:, META_P1:META_P1 + 1], meta[:, META_W1:META_W1 + 1], 0.0)
            + jnp.where(col == meta[:, META_P2:META_P2 + 1], meta[:, META_W2:META_W2 + 1], 0.0))
    moe = _dot(gate.astype(BF16), win_ref[slot])
    o_ref[...] = _layer_norm(ALPHA * x_ref[...] + moe, g_ref[...], b_ref[...])


def _combine(x, meta, ys, seg, ln_g, ln_b):
    row_spec = pl.BlockSpec((TS_MIX, D_MODEL), lambda i, *_: (i, 0))
    return pl.pallas_call(
        _combine_kernel,
        grid_spec=pltpu.PrefetchScalarGridSpec(
            num_scalar_prefetch=1,
            grid=(N_TOKEN_TILES,),
            in_specs=[
                row_spec,
                pl.BlockSpec((TS_MIX, LANES), lambda i, *_: (i, 0)),
                pl.BlockSpec(memory_space=pl.ANY),
                pl.BlockSpec((1, D_MODEL), lambda i, *_: (0, 0)),
                pl.BlockSpec((1, D_MODEL), lambda i, *_: (0, 0)),
            ],
            out_specs=row_spec,
            scratch_shapes=[
                pltpu.VMEM((2, SLOT_ROWS, D_MODEL), BF16),
                pltpu.SemaphoreType.DMA((2,)),
            ],
        ),
        out_shape=jax.ShapeDtypeStruct((TOKENS, D_MODEL), F32),
        compiler_params=pltpu.CompilerParams(
            dimension_semantics=("arbitrary",), vmem_limit_bytes=VMEM_LIMIT_BYTES),
        name="moe_combine",
    )(seg, x, meta, ys, ln_g[None], ln_b[None])


def _routing_tables(tile_counts):
    n = tile_counts[:, 0, :N_EXPERTS].astype(jnp.int32)
    counts = jnp.sum(n, axis=0)
    tiles = (counts + TM_MOE - 1) // TM_MOE
    tile_end = jnp.cumsum(tiles)
    tile_start = tile_end - tiles
    offset = tile_start * TM_MOE
    local_start = jnp.cumsum(n, axis=1) - n
    sorted_start = offset[None, :] + jnp.cumsum(n, axis=0) - n
    seg = jnp.concatenate([n.reshape(-1), local_start.reshape(-1), sorted_start.reshape(-1)])
    n_active = tile_end[-1:]
    tile_id = jnp.minimum(jnp.arange(N_TILES_MOE, dtype=jnp.int32), n_active[0] - 1)
    tile_expert = jnp.sum(tile_id[:, None] >= tile_end[None, :], axis=1).astype(jnp.int32)
    tile_rows = jnp.clip(counts[tile_expert] - (tile_id - tile_start[tile_expert]) * TM_MOE, 0, TM_MOE)
    pad_tile = jnp.concatenate([jnp.where(tiles > 0, tile_end - 1, -1), n_active]).astype(jnp.int32)
    return (seg.astype(jnp.int32), pad_tile, tile_expert, tile_rows.astype(jnp.int32),
            n_active.astype(jnp.int32))


def _even_layer(x, w_in, conv_a_w, conv_a_b, norm_a_g, norm_a_b, pool_w, pool_scale, w_out,
                ln1_g, ln1_b, ffn_w_gate, ffn_w_up, ffn_w_down, ln2_g, ln2_b):
    x = _even_mixer(x, w_in, conv_a_w, conv_a_b, norm_a_g, norm_a_b, pool_w, pool_scale, w_out,
                    ln1_g, ln1_b)
    x = _dense_ffn(x.reshape(TOKENS, D_MODEL), ffn_w_gate, ffn_w_up, ffn_w_down, ln2_g, ln2_b)
    return x.reshape(BATCH, SEQ, D_MODEL)


def _odd_layer(x, w_in, conv_c_w, sgu_norm_g, sgu_norm_b, sgu_w, sgu_b, w_out, ln1_g, ln1_b,
               router, moe_w_gate, moe_w_up, moe_w_down, ln2_g, ln2_b):
    x, meta, tile_counts = _odd_mixer(x, w_in, conv_c_w, sgu_norm_g, sgu_norm_b, sgu_w, sgu_b, w_out,
                                      ln1_g, ln1_b, router)
    x = x.reshape(TOKENS, D_MODEL)
    seg, pad_tile, tile_expert, tile_rows, n_active = _routing_tables(tile_counts)
    xs = _dispatch(x, meta, seg, pad_tile)
    ys = _grouped_ffn(xs, moe_w_gate, moe_w_up, moe_w_down, tile_expert, tile_rows, n_active)
    x = _combine(x, meta, ys, seg, ln2_g, ln2_b)
    return x.reshape(BATCH, SEQ, D_MODEL)


def kernel(x, even_w_in, even_conv_a_w, even_conv_a_b, even_norm_a_g, even_norm_a_b, even_pool_w, even_pool_scale, even_w_out, even_ln1_g, even_ln1_b, even_ffn_w_gate, even_ffn_w_up, even_ffn_w_down, even_ln2_g, even_ln2_b, odd_w_in, odd_conv_c_w, odd_sgu_norm_g, odd_sgu_norm_b, odd_sgu_w, odd_sgu_b, odd_w_out, odd_ln1_g, odd_ln1_b, odd_router, odd_moe_w_gate, odd_moe_w_up, odd_moe_w_down, odd_ln2_g, odd_ln2_b):
    for layer in range(DEPTH):
        i = layer // 2
        if layer % 2 == 0:
            x = _even_layer(x, even_w_in[i], even_conv_a_w[i], even_conv_a_b[i], even_norm_a_g[i],
                            even_norm_a_b[i], even_pool_w[i], even_pool_scale[i], even_w_out[i],
                            even_ln1_g[i], even_ln1_b[i], even_ffn_w_gate[i], even_ffn_w_up[i],
                            even_ffn_w_down[i], even_ln2_g[i], even_ln2_b[i])
        else:
            x = _odd_layer(x, odd_w_in[i], odd_conv_c_w[i], odd_sgu_norm_g[i], odd_sgu_norm_b[i],
                           odd_sgu_w[i], odd_sgu_b[i], odd_w_out[i], odd_ln1_g[i], odd_ln1_b[i],
                           odd_router[i], odd_moe_w_gate[i], odd_moe_w_up[i], odd_moe_w_down[i],
                           odd_ln2_g[i], odd_ln2_b[i])
    return x
```

```python
import functools

import jax
import jax.numpy as jnp
from jax import lax
from jax.experimental import pallas as pl
from jax.experimental.pallas import tpu as pltpu

F32 = jnp.float32
BF16 = jnp.bfloat16

D_MODEL = 1024
BATCH = 4
SEQ = 4096
TOKENS = BATCH * SEQ
DEPTH = 2

D_A = 512
CONV_A_WIDTH = 31
D_B = 512
POOL_WINDOWS = (2, 4, 8, 16)
POOL_GROUP_DIM = 128
IN_EVEN = 2 * D_A + D_B

D_C = 512
CONV_C_WIDTH = 3
D_D = 512
SGU_BLOCK = 128
SGU_HEADS = 4
SGU_HEAD_DIM = 128
IN_ODD = 3 * D_C + 2 * D_D

D_FF_DENSE = 2816
N_EXPERTS = 8
D_FF_EXPERT = 3584

ALPHA = (2 * DEPTH) ** 0.25
LN_EPS = 1e-5

SUBLANES = 8
LANES = 128
VMEM_LIMIT_BYTES = 56 * 1024 * 1024

TS_MIX = 512
MIX_CHAINS = 2
ROW_SET_STRIDE = 4
CONV_SETS = 8
HALO_A = 32
HALO_B = 16
HALO_C = 8
TM_FFN = 1024
FF_CHUNK = 512
TM_MOE = 2048
SUB_MOE = 256
TF_MOE = 512
N_TOKEN_TILES = TOKENS // TS_MIX
N_SEGMENTS = N_TOKEN_TILES * N_EXPERTS
RUN_ALIGN = 16
SLOT_ROWS = 2 * TS_MIX + N_EXPERTS * RUN_ALIGN
MAX_SORTED_ROWS = 2 * TOKENS + N_SEGMENTS * (RUN_ALIGN - 1)
N_TILES_MOE = (MAX_SORTED_ROWS + N_EXPERTS * (TM_MOE - 1)) // TM_MOE
ROWS_SORTED = N_TILES_MOE * TM_MOE


def _layer_norm(x, g, b):
    mu = jnp.mean(x, axis=-1, keepdims=True)
    xc = x - mu
    var = jnp.mean(xc * xc, axis=-1, keepdims=True)
    return xc * lax.rsqrt(var + LN_EPS) * g + b


def _silu(x):
    return x * jax.nn.sigmoid(x)


def _dot(a, b):
    return jnp.dot(a, b, preferred_element_type=F32)


def _even_mixer_kernel(x_ref, win_ref, cw_ref, cb_ref, nag_ref, nab_ref, pw_ref, ps_ref,
                       wout_ref, g1_ref, b1_ref, o_ref, exta_ref, extb_ref, mixf_ref, mixin_ref):
    s = pl.program_id(1)
    ts = TS_MIX

    n_a, n_b = D_A // LANES, D_B // LANES

    @pl.when(s == 0)
    def _():
        exta_ref[:, 0:HALO_A, :] = jnp.zeros((n_a, HALO_A, LANES), F32)
        extb_ref[:, 0:HALO_B, :] = jnp.zeros((n_b, HALO_B, LANES), F32)

    x = x_ref[...]
    h = _dot(x.astype(BF16), win_ref[...])
    glu = h[:, :D_A] * jax.nn.sigmoid(h[:, D_A:2 * D_A])
    for cb in range(n_a):
        exta_ref[cb, HALO_A:HALO_A + ts, :] = glu[:, cb * LANES:(cb + 1) * LANES]
    for cb in range(n_b):
        extb_ref[cb, HALO_B:HALO_B + ts, :] = h[:, 2 * D_A + cb * LANES:2 * D_A + (cb + 1) * LANES]

    def row_set(first):
        return pl.ds(first, SUBLANES, stride=ROW_SET_STRIDE)

    set_firsts = [blk * SUBLANES * ROW_SET_STRIDE + r
                  for blk in range(ts // (SUBLANES * ROW_SET_STRIDE)) for r in range(ROW_SET_STRIDE)]
    for cb in range(n_a):
        lanes = slice(cb * LANES, (cb + 1) * LANES)
        for group in range(0, len(set_firsts), CONV_SETS):
            firsts = set_firsts[group:group + CONV_SETS]
            accs = [jnp.broadcast_to(cb_ref[:, lanes], (SUBLANES, LANES))] * len(firsts)
            for k in range(CONV_A_WIDTH):
                wk = jnp.broadcast_to(cw_ref[k:k + 1, lanes], (SUBLANES, LANES))
                back = CONV_A_WIDTH - 1 - k
                for n, first in enumerate(firsts):
                    accs[n] = accs[n] + wk * exta_ref[cb, row_set(HALO_A + first - back), :]
            for n, first in enumerate(firsts):
                mixf_ref[cb, row_set(first), :] = accs[n]

    set_step = ROW_SET_STRIDE * lax.broadcasted_iota(jnp.int32, (SUBLANES, 1), 0)
    for first in set_firsts:
        row_pos = s * ts + first + set_step
        for g, win in enumerate(POOL_WINDOWS):
            frame = extb_ref[g, row_set(HALO_B + first), :]
            wsum = frame
            for i in range(1, win):
                wsum = wsum + extb_ref[g, row_set(HALO_B + first - i), :]
            if first >= win - 1:
                mean = wsum * (1.0 / win)
            else:
                mean = wsum / jnp.minimum(row_pos + 1, win).astype(F32)
            mixf_ref[n_a + g, row_set(first), :] = mean - frame

    norm_rows = 128
    for r0 in range(0, ts, norm_rows):
        conv = jnp.concatenate([mixf_ref[cb, r0:r0 + norm_rows, :] for cb in range(n_a)], axis=1)
        y = _silu(_layer_norm(conv, nag_ref[...], nab_ref[...]))
        mixin_ref[r0:r0 + norm_rows, 0:D_A] = y.astype(BF16)
    for g in range(n_b):
        lo, hi = g * POOL_GROUP_DIM, (g + 1) * POOL_GROUP_DIM
        mixed = _dot(mixf_ref[n_a + g].astype(BF16), pw_ref[g]) * ps_ref[:, lo:hi]
        mixin_ref[:, D_A + lo:D_A + hi] = mixed.astype(BF16)

    mix = _dot(mixin_ref[...], wout_ref[...])
    o_ref[...] = _layer_norm(ALPHA * x + mix, g1_ref[...], b1_ref[...])

    exta_ref[:, 0:HALO_A, :] = exta_ref[:, ts:ts + HALO_A, :]
    extb_ref[:, 0:HALO_B, :] = extb_ref[:, ts:ts + HALO_B, :]


def _const_spec(shape):
    return pl.BlockSpec(shape, lambda *_: (0,) * len(shape))


def _even_mixer(x, w_in, conv_w, conv_b, norm_g, norm_b, pool_w, pool_scale, w_out, ln_g, ln_b):
    ts = TS_MIX
    x_spec = pl.BlockSpec((None, ts, D_MODEL), lambda b, s: (b, s, 0))
    return pl.pallas_call(
        _even_mixer_kernel,
        grid=(BATCH, SEQ // ts),
        in_specs=[
            x_spec,
            _const_spec((D_MODEL, IN_EVEN)),
            _const_spec((CONV_A_WIDTH, D_A)),
            _const_spec((1, D_A)),
            _const_spec((1, D_A)),
            _const_spec((1, D_A)),
            _const_spec((len(POOL_WINDOWS), POOL_GROUP_DIM, POOL_GROUP_DIM)),
            _const_spec((1, D_B)),
            _const_spec((D_A + D_B, D_MODEL)),
            _const_spec((1, D_MODEL)),
            _const_spec((1, D_MODEL)),
        ],
        out_specs=x_spec,
        out_shape=jax.ShapeDtypeStruct((BATCH, SEQ, D_MODEL), F32),
        scratch_shapes=[
            pltpu.VMEM((D_A // LANES, HALO_A + ts, LANES), F32),
            pltpu.VMEM((D_B // LANES, HALO_B + ts, LANES), F32),
            pltpu.VMEM(((D_A + D_B) // LANES, ts, LANES), F32),
            pltpu.VMEM((ts, D_A + D_B), BF16),
        ],
        compiler_params=pltpu.CompilerParams(
            dimension_semantics=("arbitrary", "arbitrary"), vmem_limit_bytes=VMEM_LIMIT_BYTES),
        name="even_mixer",
    )(x, w_in.astype(BF16), conv_w, conv_b[None], norm_g[None], norm_b[None],
      pool_w.astype(BF16), pool_scale[None], w_out.astype(BF16), ln_g[None], ln_b[None])


def _ff_chunks(total, chunk):
    bounds = list(range(0, total, chunk)) + [total]
    return list(zip(bounds[:-1], bounds[1:]))


def _dense_ffn_kernel(x_ref, wg_ref, wu_ref, wd_ref, g_ref, b_ref, o_ref, acc_ref):
    x = x_ref[...]
    xb = x.astype(BF16)
    for n, (lo, hi) in enumerate(_ff_chunks(D_FF_DENSE, FF_CHUNK)):
        a = _silu(_dot(xb, wg_ref[:, lo:hi])) * _dot(xb, wu_ref[:, lo:hi])
        part = _dot(a.astype(BF16), wd_ref[lo:hi, :])
        if n == 0:
            acc_ref[...] = part
        else:
            acc_ref[...] += part
    o_ref[...] = _layer_norm(ALPHA * x + acc_ref[...], g_ref[...], b_ref[...])


def _dense_ffn(x, w_gate, w_up, w_down, ln_g, ln_b):
    tm = TM_FFN
    row_spec = pl.BlockSpec((tm, D_MODEL), lambda i: (i, 0))
    resident = pl.BlockSpec(memory_space=pltpu.VMEM)
    return pl.pallas_call(
        _dense_ffn_kernel,
        grid=(TOKENS // tm,),
        in_specs=[row_spec, resident, resident, resident,
                  _const_spec((1, D_MODEL)), _const_spec((1, D_MODEL))],
        out_specs=row_spec,
        out_shape=jax.ShapeDtypeStruct((TOKENS, D_MODEL), F32),
        scratch_shapes=[pltpu.VMEM((tm, D_MODEL), F32)],
        compiler_params=pltpu.CompilerParams(
            dimension_semantics=("arbitrary",), vmem_limit_bytes=VMEM_LIMIT_BYTES),
        name="dense_ffn",
    )(x, w_gate.astype(BF16), w_up.astype(BF16), w_down.astype(BF16), ln_g[None], ln_b[None])


META_E1, META_E2, META_W1, META_W2, META_P1, META_P2 = range(6)


def _odd_mixer_kernel(x_ref, win_ref, ccw_ref, sng_ref, snb_ref, sw_ref, sb_ref, wout_ref,
                      g1_ref, b1_ref, rhi_ref, rlo_ref, o_ref, meta_ref, cnt_ref,
                      extc_ref, mixin_ref):
    s = pl.program_id(1)
    ts = TS_MIX

    @pl.when(s == 0)
    def _():
        extc_ref[0:HALO_C, :] = jnp.zeros((HALO_C, D_C), F32)

    tri = (lax.broadcasted_iota(jnp.int32, (SGU_BLOCK, SGU_BLOCK), 0)
           >= lax.broadcasted_iota(jnp.int32, (SGU_BLOCK, SGU_BLOCK), 1))
    sgu_w = [jnp.where(tri, sw_ref[hd], 0.0).astype(BF16) for hd in range(SGU_HEADS)]

    chain_rows = ts // MIX_CHAINS
    blocks = chain_rows // SGU_BLOCK
    x1_parts = []
    for c in range(MIX_CHAINS):
        c0, c1 = c * chain_rows, (c + 1) * chain_rows
        x = x_ref[c0:c1, :]
        h = _dot(x.astype(BF16), win_ref[...])

        extc_ref[HALO_C + c0:HALO_C + c1, :] = h[:, D_C:2 * D_C] * h[:, 2 * D_C:3 * D_C]
        conv = ccw_ref[0:1, :] * extc_ref[HALO_C - 2 + c0:HALO_C - 2 + c1, :]
        conv = conv + ccw_ref[1:2, :] * extc_ref[HALO_C - 1 + c0:HALO_C - 1 + c1, :]
        conv = conv + ccw_ref[2:3, :] * extc_ref[HALO_C + c0:HALO_C + c1, :]
        mixin_ref[c0:c1, 0:D_C] = (h[:, 0:D_C] * conv).astype(BF16)

        z = jax.nn.gelu(h[:, 3 * D_C:])
        u = z[:, :D_D]
        v = _layer_norm(z[:, D_D:], sng_ref[...], snb_ref[...]).astype(BF16)
        for hd in range(SGU_HEADS):
            lo, hi = hd * SGU_HEAD_DIM, (hd + 1) * SGU_HEAD_DIM
            v_blocks = jnp.concatenate(
                [v[blk * SGU_BLOCK:(blk + 1) * SGU_BLOCK, lo:hi] for blk in range(blocks)], axis=1)
            mixed = _dot(sgu_w[hd], v_blocks) + sb_ref[:, hd:hd + 1]
            for blk in range(blocks):
                r0, r1 = blk * SGU_BLOCK, (blk + 1) * SGU_BLOCK
                gated = u[r0:r1, lo:hi] * mixed[:, blk * SGU_HEAD_DIM:(blk + 1) * SGU_HEAD_DIM]
                mixin_ref[c0 + r0:c0 + r1, D_C + lo:D_C + hi] = gated.astype(BF16)

        mix = _dot(mixin_ref[c0:c1, :], wout_ref[...])
        x1_c = _layer_norm(ALPHA * x + mix, g1_ref[...], b1_ref[...])
        o_ref[c0:c1, :] = x1_c
        x1_parts.append(x1_c)
    x1 = jnp.concatenate(x1_parts, axis=0)

    extc_ref[0:HALO_C, :] = extc_ref[ts:ts + HALO_C, :]

    x_hi = x1.astype(BF16)
    x_lo = (x1 - x_hi.astype(F32)).astype(BF16)
    logits = _dot(x_hi, rhi_ref[...]) + _dot(x_lo, rhi_ref[...]) + _dot(x_hi, rlo_ref[...])

    lane = lax.broadcasted_iota(jnp.int32, (ts, LANES), 1).astype(F32)
    neg_inf = jnp.float32(-jnp.inf)
    lg = jnp.where(lane < N_EXPERTS, logits, neg_inf)
    m1 = jnp.max(lg, axis=1, keepdims=True)
    e1 = jnp.min(jnp.where(lg == m1, lane, float(LANES)), axis=1, keepdims=True)
    lg2 = jnp.where(lane == e1, neg_inf, lg)
    m2 = jnp.max(lg2, axis=1, keepdims=True)
    e2 = jnp.min(jnp.where(lg2 == m2, lane, float(LANES)), axis=1, keepdims=True)
    ex = jnp.exp(m2 - m1)
    den = 1.0 + ex
    w1 = 1.0 / den
    w2 = ex / den

    onehot = jnp.where(jnp.logical_or(lane == e1, lane == e2), 1.0, 0.0)
    onehot_b = onehot.astype(BF16)
    before = (lax.broadcasted_iota(jnp.int32, (ts, ts), 0)
              > lax.broadcasted_iota(jnp.int32, (ts, ts), 1))
    lower = (lax.broadcasted_iota(jnp.int32, (LANES, LANES), 0)
             < lax.broadcasted_iota(jnp.int32, (LANES, LANES), 1))
    earlier = _dot(jnp.where(before, 1.0, 0.0).astype(BF16), onehot_b)
    run = jnp.sum(onehot, axis=0, keepdims=True)
    run = jnp.floor((run + (RUN_ALIGN - 1)) * (1.0 / RUN_ALIGN)) * RUN_ALIGN
    run_rows = jnp.broadcast_to(run, (SUBLANES, LANES)).astype(BF16)
    lower_rows = _dot(run_rows, jnp.where(lower, 1.0, 0.0).astype(BF16))[0:1, :]
    place = earlier + lower_rows
    p1 = jnp.sum(jnp.where(lane == e1, place, 0.0), axis=1, keepdims=True)
    p2 = jnp.sum(jnp.where(lane == e2, place, 0.0), axis=1, keepdims=True)

    meta = jnp.zeros((ts, LANES), F32)
    for col, val in ((META_E1, e1), (META_E2, e2), (META_W1, w1), (META_W2, w2),
                     (META_P1, p1), (META_P2, p2)):
        meta = jnp.where(lane == col, val, meta)
    meta_ref[...] = meta
    cnt_ref[...] = jnp.broadcast_to(run, (SUBLANES, LANES))


def _odd_mixer(x, w_in, conv_w, sgu_norm_g, sgu_norm_b, sgu_w, sgu_b, w_out, ln_g, ln_b, router):
    ts = TS_MIX
    x_spec = pl.BlockSpec((None, ts, D_MODEL), lambda b, s: (b, s, 0))
    router_pad = jnp.zeros((D_MODEL, LANES), F32).at[:, :N_EXPERTS].set(router)
    router_hi = router_pad.astype(BF16)
    router_lo = (router_pad - router_hi.astype(F32)).astype(BF16)
    return pl.pallas_call(
        _odd_mixer_kernel,
        grid=(BATCH, SEQ // ts),
        in_specs=[
            x_spec,
            _const_spec((D_MODEL, IN_ODD)),
            _const_spec((CONV_C_WIDTH, D_C)),
            _const_spec((1, D_D)),
            _const_spec((1, D_D)),
            _const_spec((SGU_HEADS, SGU_BLOCK, SGU_BLOCK)),
            _const_spec((SGU_BLOCK, SGU_HEADS)),
            _const_spec((D_C + D_D, D_MODEL)),
            _const_spec((1, D_MODEL)),
            _const_spec((1, D_MODEL)),
            _const_spec((D_MODEL, LANES)),
            _const_spec((D_MODEL, LANES)),
        ],
        out_specs=[
            x_spec,
            pl.BlockSpec((ts, LANES), lambda b, s: (b * (SEQ // ts) + s, 0)),
            pl.BlockSpec((None, SUBLANES, LANES), lambda b, s: (b * (SEQ // ts) + s, 0, 0)),
        ],
        out_shape=[
            jax.ShapeDtypeStruct((BATCH, SEQ, D_MODEL), F32),
            jax.ShapeDtypeStruct((TOKENS, LANES), F32),
            jax.ShapeDtypeStruct((N_TOKEN_TILES, SUBLANES, LANES), F32),
        ],
        scratch_shapes=[
            pltpu.VMEM((HALO_C + ts, D_C), F32),
            pltpu.VMEM((ts, D_C + D_D), BF16),
        ],
        compiler_params=pltpu.CompilerParams(
            dimension_semantics=("arbitrary", "arbitrary"), vmem_limit_bytes=VMEM_LIMIT_BYTES),
        name="odd_mixer",
    )(x, w_in.astype(BF16), conv_w, sgu_norm_g[None], sgu_norm_b[None], sgu_w, sgu_b.T,
      w_out.astype(BF16), ln_g[None], ln_b[None], router_hi, router_lo)


SEGMENT_BITS = range(RUN_ALIGN.bit_length() - 1, TS_MIX.bit_length())


def _for_each_segment_chunk(seg_ref, tile, visit):
    for e in range(N_EXPERTS):
        k = tile * N_EXPERTS + e
        n = seg_ref[k]
        local = seg_ref[N_SEGMENTS + k]
        sorted_row = seg_ref[2 * N_SEGMENTS + k]
        for bit in reversed(SEGMENT_BITS):
            @pl.when(((n >> bit) & 1) == 1)
            def _():
                done = (n >> (bit + 1)) << (bit + 1)
                visit(pl.multiple_of(local + done, RUN_ALIGN), pl.multiple_of(sorted_row + done, RUN_ALIGN),
                      1 << bit)


def _dispatch_kernel(seg_ref, pad_tile_ref, x_ref, meta_ref, xs_ref, zero_ref, perm_ref, zero_sem, perm_sem):
    i = pl.program_id(0)
    slot = i % 2

    def chunk_copy(which):
        def make(local, sorted_row, size):
            return pltpu.make_async_copy(perm_ref.at[which, pl.ds(local, size)],
                                         xs_ref.at[pl.ds(sorted_row, size)], perm_sem.at[which])
        return make

    def wait_tile(tile, which):
        _for_each_segment_chunk(seg_ref, tile, lambda *chunk: chunk_copy(which)(*chunk).wait())

    @pl.when(i == 0)
    def _():
        zero_ref[...] = jnp.zeros(zero_ref.shape, BF16)

        def clear_tile(t):
            start = pl.multiple_of(t * TM_MOE, TM_MOE)
            cp = pltpu.make_async_copy(zero_ref, xs_ref.at[pl.ds(start, TM_MOE)], zero_sem)
            cp.start()
            cp.wait()

        for e in range(N_EXPERTS):
            @pl.when(pad_tile_ref[e] >= 0)
            def _():
                clear_tile(pad_tile_ref[e])

        def clear_tail(t, carry):
            clear_tile(t)
            return carry

        lax.fori_loop(pad_tile_ref[N_EXPERTS], N_TILES_MOE, clear_tail, 0)

    meta = meta_ref[...]
    col = lax.broadcasted_iota(jnp.int32, (TS_MIX, SLOT_ROWS), 1).astype(F32)
    hit = jnp.logical_or(col == meta[:, META_P1:META_P1 + 1], col == meta[:, META_P2:META_P2 + 1])
    onehot = jnp.where(hit, 1.0, 0.0).astype(BF16)
    permuted = lax.dot_general(onehot, x_ref[...].astype(BF16), (((0,), (0,)), ((), ())),
                               preferred_element_type=F32)

    @pl.when(i >= 2)
    def _():
        wait_tile(i - 2, slot)

    perm_ref[slot] = permuted.astype(BF16)
    _for_each_segment_chunk(seg_ref, i, lambda *chunk: chunk_copy(slot)(*chunk).start())

    @pl.when(i == N_TOKEN_TILES - 1)
    def _():
        wait_tile(i - 1, 1 - slot)
        wait_tile(i, slot)


def _dispatch(x, meta, seg, pad_tile):
    return pl.pallas_call(
        _dispatch_kernel,
        grid_spec=pltpu.PrefetchScalarGridSpec(
            num_scalar_prefetch=2,
            grid=(N_TOKEN_TILES,),
            in_specs=[pl.BlockSpec((TS_MIX, D_MODEL), lambda i, *_: (i, 0)),
                      pl.BlockSpec((TS_MIX, LANES), lambda i, *_: (i, 0))],
            out_specs=pl.BlockSpec(memory_space=pl.ANY),
            scratch_shapes=[
                pltpu.VMEM((TM_MOE, D_MODEL), BF16),
                pltpu.VMEM((2, SLOT_ROWS, D_MODEL), BF16),
                pltpu.SemaphoreType.DMA(()),
                pltpu.SemaphoreType.DMA((2,)),
            ],
        ),
        out_shape=jax.ShapeDtypeStruct((ROWS_SORTED, D_MODEL), BF16),
        compiler_params=pltpu.CompilerParams(
            dimension_semantics=("arbitrary",), vmem_limit_bytes=VMEM_LIMIT_BYTES),
        name="moe_dispatch",
    )(seg, pad_tile, x, meta)


def _grouped_ffn_kernel(tile_expert_ref, tile_rows_ref, n_active_ref, xs_ref, wg_ref, wu_ref, wd_ref,
                        ys_ref, acc_ref):
    i = pl.program_id(0)
    j = pl.program_id(1)

    active = i < n_active_ref[0]
    rows = tile_rows_ref[i]

    uncovered = jnp.logical_or(jnp.logical_not(active), rows <= TM_MOE - SUB_MOE)

    @pl.when(jnp.logical_and(j == 0, jnp.logical_or(i == 0, uncovered)))
    def _():
        acc_ref[...] = jnp.zeros((TM_MOE, D_MODEL), F32)

    def accumulate(n_rows):
        xb = xs_ref[0:n_rows, :]
        a = _silu(_dot(xb, wg_ref[...].astype(BF16))) * _dot(xb, wu_ref[...].astype(BF16))
        part = _dot(a.astype(BF16), wd_ref[...].astype(BF16))
        acc_ref[0:n_rows, :] = jnp.where(j == 0, part, acc_ref[0:n_rows, :] + part)

    for n_sub in range(1, TM_MOE // SUB_MOE + 1):
        lo, hi = (n_sub - 1) * SUB_MOE, n_sub * SUB_MOE

        @pl.when(jnp.logical_and(active, jnp.logical_and(rows > lo, rows <= hi)))
        def _():
            accumulate(hi)

    @pl.when(j == pl.num_programs(1) - 1)
    def _():
        ys_ref[...] = acc_ref[...].astype(BF16)


def _grouped_ffn(xs, w_gate, w_up, w_down, tile_expert, tile_rows, n_active):
    nj = D_FF_EXPERT // TF_MOE

    def row_map(i, j, te, tr, na):
        return (jnp.minimum(i, na[0] - 1), 0)

    def frozen_j(i, j, na):
        return jnp.where(i < na[0], j, nj - 1)

    def up_map(i, j, te, tr, na):
        return (te[i], 0, frozen_j(i, j, na))

    def down_map(i, j, te, tr, na):
        return (te[i], frozen_j(i, j, na), 0)

    return pl.pallas_call(
        _grouped_ffn_kernel,
        grid_spec=pltpu.PrefetchScalarGridSpec(
            num_scalar_prefetch=3,
            grid=(N_TILES_MOE, nj),
            in_specs=[
                pl.BlockSpec((TM_MOE, D_MODEL), row_map),
                pl.BlockSpec((None, D_MODEL, TF_MOE), up_map),
                pl.BlockSpec((None, D_MODEL, TF_MOE), up_map),
                pl.BlockSpec((None, TF_MOE, D_MODEL), down_map),
            ],
            out_specs=pl.BlockSpec((TM_MOE, D_MODEL), lambda i, j, *_: (i, 0)),
            scratch_shapes=[pltpu.VMEM((TM_MOE, D_MODEL), F32)],
        ),
        out_shape=jax.ShapeDtypeStruct((ROWS_SORTED, D_MODEL), BF16),
        compiler_params=pltpu.CompilerParams(
            dimension_semantics=("arbitrary", "arbitrary"), vmem_limit_bytes=VMEM_LIMIT_BYTES),
        name="moe_grouped_ffn",
    )(tile_expert, tile_rows, n_active, xs, w_gate, w_up, w_down)


def _combine_kernel(seg_ref, x_ref, meta_ref, ys_ref, g_ref, b_ref, o_ref, win_ref, sem):
    i = pl.program_id(0)
    slot = i % 2

    def chunk_copy(which):
        def make(local, sorted_row, size):
            return pltpu.make_async_copy(ys_ref.at[pl.ds(sorted_row, size)],
                                         win_ref.at[which, pl.ds(local, size)], sem.at[which])
        return make

    def fetch(tile, which):
        _for_each_segment_chunk(seg_ref, tile, lambda *chunk: chunk_copy(which)(*chunk).start())

    @pl.when(i == 0)
    def _():
        win_ref[...] = jnp.zeros(win_ref.shape, BF16)
        fetch(0, 0)

    @pl.when(i + 1 < N_TOKEN_TILES)
    def _():
        fetch(i + 1, 1 - slot)

    _for_each_segment_chunk(seg_ref, i, lambda *chunk: chunk_copy(slot)(*chunk).wait())

    meta = meta_ref[...]
    col = lax.broadcasted_iota(jnp.int32, (TS_MIX, SLOT_ROWS), 1).astype(F32)
    gate = (jnp.where(col == meta[:, META_P1:META_P1 + 1], meta[:, META_W1:META_W1 + 1], 0.0)
            + jnp.where(col == meta[:, META_P2:META_P2 + 1], meta[:, META_W2:META_W2 + 1], 0.0))
    moe = _dot(gate.astype(BF16), win_ref[slot])
    o_ref[...] = _layer_norm(ALPHA * x_ref[...] + moe, g_ref[...], b_ref[...])


def _combine(x, meta, ys, seg, ln_g, ln_b):
    row_spec = pl.BlockSpec((TS_MIX, D_MODEL), lambda i, *_: (i, 0))
    return pl.pallas_call(
        _combine_kernel,
        grid_spec=pltpu.PrefetchScalarGridSpec(
            num_scalar_prefetch=1,
            grid=(N_TOKEN_TILES,),
            in_specs=[
                row_spec,
                pl.BlockSpec((TS_MIX, LANES), lambda i, *_: (i, 0)),
                pl.BlockSpec(memory_space=pl.ANY),
                pl.BlockSpec((1, D_MODEL), lambda i, *_: (0, 0)),
                pl.BlockSpec((1, D_MODEL), lambda i, *_: (0, 0)),
            ],
            out_specs=row_spec,
            scratch_shapes=[
                pltpu.VMEM((2, SLOT_ROWS, D_MODEL), BF16),
                pltpu.SemaphoreType.DMA((2,)),
            ],
        ),
        out_shape=jax.ShapeDtypeStruct((TOKENS, D_MODEL), F32),
        compiler_params=pltpu.CompilerParams(
            dimension_semantics=("arbitrary",), vmem_limit_bytes=VMEM_LIMIT_BYTES),
        name="moe_combine",
    )(seg, x, meta, ys, ln_g[None], ln_b[None])


def _routing_tables(tile_counts):
    n = tile_counts[:, 0, :N_EXPERTS].astype(jnp.int32)
    counts = jnp.sum(n, axis=0)
    tiles = (counts + TM_MOE - 1) // TM_MOE
    tile_end = jnp.cumsum(tiles)
    tile_start = tile_end - tiles
    offset = tile_start * TM_MOE
    local_start = jnp.cumsum(n, axis=1) - n
    sorted_start = offset[None, :] + jnp.cumsum(n, axis=0) - n
    seg = jnp.concatenate([n.reshape(-1), local_start.reshape(-1), sorted_start.reshape(-1)])
    n_active = tile_end[-1:]
    tile_id = jnp.minimum(jnp.arange(N_TILES_MOE, dtype=jnp.int32), n_active[0] - 1)
    tile_expert = jnp.sum(tile_id[:, None] >= tile_end[None, :], axis=1).astype(jnp.int32)
    tile_rows = jnp.clip(counts[tile_expert] - (tile_id - tile_start[tile_expert]) * TM_MOE, 0, TM_MOE)
    pad_tile = jnp.concatenate([jnp.where(tiles > 0, tile_end - 1, -1), n_active]).astype(jnp.int32)
    return (seg.astype(jnp.int32), pad_tile, tile_expert, tile_rows.astype(jnp.int32),
            n_active.astype(jnp.int32))


def _even_layer(x, w_in, conv_a_w, conv_a_b, norm_a_g, norm_a_b, pool_w, pool_scale, w_out,
                ln1_g, ln1_b, ffn_w_gate, ffn_w_up, ffn_w_down, ln2_g, ln2_b):
    x = _even_mixer(x, w_in, conv_a_w, conv_a_b, norm_a_g, norm_a_b, pool_w, pool_scale, w_out,
                    ln1_g, ln1_b)
    x = _dense_ffn(x.reshape(TOKENS, D_MODEL), ffn_w_gate, ffn_w_up, ffn_w_down, ln2_g, ln2_b)
    return x.reshape(BATCH, SEQ, D_MODEL)


def _odd_layer(x, w_in, conv_c_w, sgu_norm_g, sgu_norm_b, sgu_w, sgu_b, w_out, ln1_g, ln1_b,
               router, moe_w_gate, moe_w_up, moe_w_down, ln2_g, ln2_b):
    x, meta, tile_counts = _odd_mixer(x, w_in, conv_c_w, sgu_norm_g, sgu_norm_b, sgu_w, sgu_b, w_out,
                                      ln1_g, ln1_b, router)
    x = x.reshape(TOKENS, D_MODEL)
    seg, pad_tile, tile_expert, tile_rows, n_active = _routing_tables(tile_counts)
    xs = _dispatch(x, meta, seg, pad_tile)
    ys = _grouped_ffn(xs, moe_w_gate, moe_w_up, moe_w_down, tile_expert, tile_rows, n_active)
    x = _combine(x, meta, ys, seg, ln2_g, ln2_b)
    return x.reshape(BATCH, SEQ, D_MODEL)


def kernel(x, even_w_in, even_conv_a_w, even_conv_a_b, even_norm_a_g, even_norm_a_b, even_pool_w, even_pool_scale, even_w_out, even_ln1_g, even_ln1_b, even_ffn_w_gate, even_ffn_w_up, even_ffn_w_down, even_ln2_g, even_ln2_b, odd_w_in, odd_conv_c_w, odd_sgu_norm_g, odd_sgu_norm_b, odd_sgu_w, odd_sgu_b, odd_w_out, odd_ln1_g, odd_ln1_b, odd_router, odd_moe_w_gate, odd_moe_w_up, odd_moe_w_down, odd_ln2_g, odd_ln2_b):
    for layer in range(DEPTH):
        i = layer // 2
        if layer % 2 == 0:
            x = _even_layer(x, even_w_in[i], even_conv_a_w[i], even_conv_a_b[i], even_norm_a_g[i],
                            even_norm_a_b[i], even_pool_w[i], even_pool_scale[i], even_w_out[i],
                            even_ln1_g[i], even_ln1_b[i], even_ffn_w_gate[i], even_ffn_w_up[i],
                            even_ffn_w_down[i], even_ln2_g[i], even_ln2_b[i])
        else:
            x = _odd_layer(x, odd_w_in[i], odd_conv_c_w[i], odd_sgu_norm_g[i], odd_sgu_norm_b[i],
                           odd_sgu_w[i], odd_sgu_b[i], odd_w_out[i], odd_ln1_g[i], odd_ln1_b[i],
                           odd_router[i], odd_moe_w_gate[i], odd_moe_w_up[i], odd_moe_w_down[i],
                           odd_ln2_g[i], odd_ln2_b[i])
    return x
```

```python
import functools

import jax
import jax.numpy as jnp
from jax import lax
from jax.experimental import pallas as pl
from jax.experimental.pallas import tpu as pltpu

F32 = jnp.float32
BF16 = jnp.bfloat16

D_MODEL = 1024
BATCH = 4
SEQ = 4096
TOKENS = BATCH * SEQ
DEPTH = 2

D_A = 512
CONV_A_WIDTH = 31
D_B = 512
POOL_WINDOWS = (2, 4, 8, 16)
POOL_GROUP_DIM = 128
IN_EVEN = 2 * D_A + D_B

D_C = 512
CONV_C_WIDTH = 3
D_D = 512
SGU_BLOCK = 128
SGU_HEADS = 4
SGU_HEAD_DIM = 128
IN_ODD = 3 * D_C + 2 * D_D

D_FF_DENSE = 2816
N_EXPERTS = 8
D_FF_EXPERT = 3584

ALPHA = (2 * DEPTH) ** 0.25
LN_EPS = 1e-5

SUBLANES = 8
LANES = 128
VMEM_LIMIT_BYTES = 56 * 1024 * 1024

TS_MIX = 512
MIX_CHAINS = 2
ROW_SET_STRIDE = 4
CONV_SETS = 8
HALO_A = 32
HALO_B = 16
HALO_C = 8
TM_FFN = 1024
FF_CHUNK = 512
TM_MOE = 1024
SUB_MOE = 256
TF_MOE = 512
N_TOKEN_TILES = TOKENS // TS_MIX
N_SEGMENTS = N_TOKEN_TILES * N_EXPERTS
RUN_ALIGN = 16
SLOT_ROWS = 2 * TS_MIX + N_EXPERTS * RUN_ALIGN
MAX_SORTED_ROWS = 2 * TOKENS + N_SEGMENTS * (RUN_ALIGN - 1)
N_TILES_MOE = (MAX_SORTED_ROWS + N_EXPERTS * (TM_MOE - 1)) // TM_MOE
ROWS_SORTED = N_TILES_MOE * TM_MOE


def _layer_norm(x, g, b):
    mu = jnp.mean(x, axis=-1, keepdims=True)
    xc = x - mu
    var = jnp.mean(xc * xc, axis=-1, keepdims=True)
    return xc * lax.rsqrt(var + LN_EPS) * g + b


def _silu(x):
    return x * jax.nn.sigmoid(x)


def _dot(a, b):
    return jnp.dot(a, b, preferred_element_type=F32)


def _even_mixer_kernel(x_ref, win_ref, cw_ref, cb_ref, nag_ref, nab_ref, pw_ref, ps_ref,
                       wout_ref, g1_ref, b1_ref, o_ref, exta_ref, extb_ref, mixf_ref, mixin_ref):
    s = pl.program_id(1)
    ts = TS_MIX

    n_a, n_b = D_A // LANES, D_B // LANES

    @pl.when(s == 0)
    def _():
        exta_ref[:, 0:HALO_A, :] = jnp.zeros((n_a, HALO_A, LANES), F32)
        extb_ref[:, 0:HALO_B, :] = jnp.zeros((n_b, HALO_B, LANES), F32)

    x = x_ref[...]
    h = _dot(x.astype(BF16), win_ref[...])
    glu = h[:, :D_A] * jax.nn.sigmoid(h[:, D_A:2 * D_A])
    for cb in range(n_a):
        exta_ref[cb, HALO_A:HALO_A + ts, :] = glu[:, cb * LANES:(cb + 1) * LANES]
    for cb in range(n_b):
        extb_ref[cb, HALO_B:HALO_B + ts, :] = h[:, 2 * D_A + cb * LANES:2 * D_A + (cb + 1) * LANES]

    def row_set(first):
        return pl.ds(first, SUBLANES, stride=ROW_SET_STRIDE)

    set_firsts = [blk * SUBLANES * ROW_SET_STRIDE + r
                  for blk in range(ts // (SUBLANES * ROW_SET_STRIDE)) for r in range(ROW_SET_STRIDE)]
    for cb in range(n_a):
        lanes = slice(cb * LANES, (cb + 1) * LANES)
        for group in range(0, len(set_firsts), CONV_SETS):
            firsts = set_firsts[group:group + CONV_SETS]
            accs = [jnp.broadcast_to(cb_ref[:, lanes], (SUBLANES, LANES))] * len(firsts)
            for k in range(CONV_A_WIDTH):
                wk = jnp.broadcast_to(cw_ref[k:k + 1, lanes], (SUBLANES, LANES))
                back = CONV_A_WIDTH - 1 - k
                for n, first in enumerate(firsts):
                    accs[n] = accs[n] + wk * exta_ref[cb, row_set(HALO_A + first - back), :]
            for n, first in enumerate(firsts):
                mixf_ref[cb, row_set(first), :] = accs[n]

    set_step = ROW_SET_STRIDE * lax.broadcasted_iota(jnp.int32, (SUBLANES, 1), 0)
    for first in set_firsts:
        row_pos = s * ts + first + set_step
        for g, win in enumerate(POOL_WINDOWS):
            frame = extb_ref[g, row_set(HALO_B + first), :]
            wsum = frame
            for i in range(1, win):
                wsum = wsum + extb_ref[g, row_set(HALO_B + first - i), :]
            if first >= win - 1:
                mean = wsum * (1.0 / win)
            else:
                mean = wsum / jnp.minimum(row_pos + 1, win).astype(F32)
            mixf_ref[n_a + g, row_set(first), :] = mean - frame

    norm_rows = 128
    for r0 in range(0, ts, norm_rows):
        conv = jnp.concatenate([mixf_ref[cb, r0:r0 + norm_rows, :] for cb in range(n_a)], axis=1)
        y = _silu(_layer_norm(conv, nag_ref[...], nab_ref[...]))
        mixin_ref[r0:r0 + norm_rows, 0:D_A] = y.astype(BF16)
    for g in range(n_b):
        lo, hi = g * POOL_GROUP_DIM, (g + 1) * POOL_GROUP_DIM
        mixed = _dot(mixf_ref[n_a + g].astype(BF16), pw_ref[g]) * ps_ref[:, lo:hi]
        mixin_ref[:, D_A + lo:D_A + hi] = mixed.astype(BF16)

    mix = _dot(mixin_ref[...], wout_ref[...])
    o_ref[...] = _layer_norm(ALPHA * x + mix, g1_ref[...], b1_ref[...])

    exta_ref[:, 0:HALO_A, :] = exta_ref[:, ts:ts + HALO_A, :]
    extb_ref[:, 0:HALO_B, :] = extb_ref[:, ts:ts + HALO_B, :]


def _const_spec(shape):
    return pl.BlockSpec(shape, lambda *_: (0,) * len(shape))


def _even_mixer(x, w_in, conv_w, conv_b, norm_g, norm_b, pool_w, pool_scale, w_out, ln_g, ln_b):
    ts = TS_MIX
    x_spec = pl.BlockSpec((None, ts, D_MODEL), lambda b, s: (b, s, 0))
    return pl.pallas_call(
        _even_mixer_kernel,
        grid=(BATCH, SEQ // ts),
        in_specs=[
            x_spec,
            _const_spec((D_MODEL, IN_EVEN)),
            _const_spec((CONV_A_WIDTH, D_A)),
            _const_spec((1, D_A)),
            _const_spec((1, D_A)),
            _const_spec((1, D_A)),
            _const_spec((len(POOL_WINDOWS), POOL_GROUP_DIM, POOL_GROUP_DIM)),
            _const_spec((1, D_B)),
            _const_spec((D_A + D_B, D_MODEL)),
            _const_spec((1, D_MODEL)),
            _const_spec((1, D_MODEL)),
        ],
        out_specs=x_spec,
        out_shape=jax.ShapeDtypeStruct((BATCH, SEQ, D_MODEL), F32),
        scratch_shapes=[
            pltpu.VMEM((D_A // LANES, HALO_A + ts, LANES), F32),
            pltpu.VMEM((D_B // LANES, HALO_B + ts, LANES), F32),
            pltpu.VMEM(((D_A + D_B) // LANES, ts, LANES), F32),
            pltpu.VMEM((ts, D_A + D_B), BF16),
        ],
        compiler_params=pltpu.CompilerParams(
            dimension_semantics=("arbitrary", "arbitrary"), vmem_limit_bytes=VMEM_LIMIT_BYTES),
        name="even_mixer",
    )(x, w_in.astype(BF16), conv_w, conv_b[None], norm_g[None], norm_b[None],
      pool_w.astype(BF16), pool_scale[None], w_out.astype(BF16), ln_g[None], ln_b[None])


def _ff_chunks(total, chunk):
    bounds = list(range(0, total, chunk)) + [total]
    return list(zip(bounds[:-1], bounds[1:]))


def _dense_ffn_kernel(x_ref, wg_ref, wu_ref, wd_ref, g_ref, b_ref, o_ref, acc_ref):
    x = x_ref[...]
    xb = x.astype(BF16)
    for n, (lo, hi) in enumerate(_ff_chunks(D_FF_DENSE, FF_CHUNK)):
        a = _silu(_dot(xb, wg_ref[:, lo:hi])) * _dot(xb, wu_ref[:, lo:hi])
        part = _dot(a.astype(BF16), wd_ref[lo:hi, :])
        if n == 0:
            acc_ref[...] = part
        else:
            acc_ref[...] += part
    o_ref[...] = _layer_norm(ALPHA * x + acc_ref[...], g_ref[...], b_ref[...])


def _dense_ffn(x, w_gate, w_up, w_down, ln_g, ln_b):
    tm = TM_FFN
    row_spec = pl.BlockSpec((tm, D_MODEL), lambda i: (i, 0))
    resident = pl.BlockSpec(memory_space=pltpu.VMEM)
    return pl.pallas_call(
        _dense_ffn_kernel,
        grid=(TOKENS // tm,),
        in_specs=[row_spec, resident, resident, resident,
                  _const_spec((1, D_MODEL)), _const_spec((1, D_MODEL))],
        out_specs=row_spec,
        out_shape=jax.ShapeDtypeStruct((TOKENS, D_MODEL), F32),
        scratch_shapes=[pltpu.VMEM((tm, D_MODEL), F32)],
        compiler_params=pltpu.CompilerParams(
            dimension_semantics=("arbitrary",), vmem_limit_bytes=VMEM_LIMIT_BYTES),
        name="dense_ffn",
    )(x, w_gate.astype(BF16), w_up.astype(BF16), w_down.astype(BF16), ln_g[None], ln_b[None])


META_E1, META_E2, META_W1, META_W2, META_P1, META_P2 = range(6)
ROUTER_ROWS = 16


def _odd_mixer_kernel(x_ref, win_ref, ccw_ref, sng_ref, snb_ref, sw_ref, sb_ref, wout_ref,
                      g1_ref, b1_ref, rhi_ref, rlo_ref, o_ref, meta_ref, cnt_ref,
                      extc_ref, mixin_ref):
    s = pl.program_id(1)
    ts = TS_MIX

    @pl.when(s == 0)
    def _():
        extc_ref[0:HALO_C, :] = jnp.zeros((HALO_C, D_C), F32)

    tri = (lax.broadcasted_iota(jnp.int32, (SGU_BLOCK, SGU_BLOCK), 0)
           >= lax.broadcasted_iota(jnp.int32, (SGU_BLOCK, SGU_BLOCK), 1))
    sgu_w = [jnp.where(tri, sw_ref[hd], 0.0).astype(BF16) for hd in range(SGU_HEADS)]

    chain_rows = ts // MIX_CHAINS
    blocks = chain_rows // SGU_BLOCK
    x1_parts = []
    for c in range(MIX_CHAINS):
        c0, c1 = c * chain_rows, (c + 1) * chain_rows
        x = x_ref[c0:c1, :]
        h = _dot(x.astype(BF16), win_ref[...])

        extc_ref[HALO_C + c0:HALO_C + c1, :] = h[:, D_C:2 * D_C] * h[:, 2 * D_C:3 * D_C]
        conv = ccw_ref[0:1, :] * extc_ref[HALO_C - 2 + c0:HALO_C - 2 + c1, :]
        conv = conv + ccw_ref[1:2, :] * extc_ref[HALO_C - 1 + c0:HALO_C - 1 + c1, :]
        conv = conv + ccw_ref[2:3, :] * extc_ref[HALO_C + c0:HALO_C + c1, :]
        mixin_ref[c0:c1, 0:D_C] = (h[:, 0:D_C] * conv).astype(BF16)

        z = jax.nn.gelu(h[:, 3 * D_C:])
        u = z[:, :D_D]
        v = _layer_norm(z[:, D_D:], sng_ref[...], snb_ref[...]).astype(BF16)
        for hd in range(SGU_HEADS):
            lo, hi = hd * SGU_HEAD_DIM, (hd + 1) * SGU_HEAD_DIM
            v_blocks = jnp.concatenate(
                [v[blk * SGU_BLOCK:(blk + 1) * SGU_BLOCK, lo:hi] for blk in range(blocks)], axis=1)
            mixed = _dot(sgu_w[hd], v_blocks) + sb_ref[:, hd:hd + 1]
            for blk in range(blocks):
                r0, r1 = blk * SGU_BLOCK, (blk + 1) * SGU_BLOCK
                gated = u[r0:r1, lo:hi] * mixed[:, blk * SGU_HEAD_DIM:(blk + 1) * SGU_HEAD_DIM]
                mixin_ref[c0 + r0:c0 + r1, D_C + lo:D_C + hi] = gated.astype(BF16)

        mix = _dot(mixin_ref[c0:c1, :], wout_ref[...])
        x1_c = _layer_norm(ALPHA * x + mix, g1_ref[...], b1_ref[...])
        o_ref[c0:c1, :] = x1_c
        x1_parts.append(x1_c)
    x1 = jnp.concatenate(x1_parts, axis=0)

    extc_ref[0:HALO_C, :] = extc_ref[ts:ts + HALO_C, :]

    x_hi = x1.astype(BF16)
    x_lo = (x1 - x_hi.astype(F32)).astype(BF16)
    nt = (((1,), (1,)), ((), ()))
    logits = (lax.dot_general(rhi_ref[...], x_hi, nt, preferred_element_type=F32)
              + lax.dot_general(rhi_ref[...], x_lo, nt, preferred_element_type=F32)
              + lax.dot_general(rlo_ref[...], x_hi, nt, preferred_element_type=F32))[0:N_EXPERTS, :]

    expert = lax.broadcasted_iota(jnp.int32, (N_EXPERTS, ts), 0).astype(F32)
    neg_inf = jnp.float32(-jnp.inf)
    m1 = jnp.max(logits, axis=0, keepdims=True)
    e1 = jnp.min(jnp.where(logits == m1, expert, float(N_EXPERTS)), axis=0, keepdims=True)
    rest = jnp.where(expert == e1, neg_inf, logits)
    m2 = jnp.max(rest, axis=0, keepdims=True)
    e2 = jnp.min(jnp.where(rest == m2, expert, float(N_EXPERTS)), axis=0, keepdims=True)
    ex = jnp.exp(m2 - m1)
    den = 1.0 + ex
    w1 = 1.0 / den
    w2 = ex / den

    onehot = jnp.where(jnp.logical_or(expert == e1, expert == e2), 1.0, 0.0)
    onehot_b = jnp.concatenate([onehot, jnp.zeros_like(onehot)], axis=0).astype(BF16)
    before = (lax.broadcasted_iota(jnp.int32, (ts, ts), 0)
              < lax.broadcasted_iota(jnp.int32, (ts, ts), 1))
    earlier = _dot(onehot_b, jnp.where(before, 1.0, 0.0).astype(BF16))[0:N_EXPERTS, :]
    run = jnp.sum(onehot, axis=1, keepdims=True)
    run = jnp.floor((run + (RUN_ALIGN - 1)) * (1.0 / RUN_ALIGN)) * RUN_ALIGN
    expert_col = expert[:, 0:1]
    lower_rows = jnp.zeros((N_EXPERTS, 1), F32)
    for e in range(N_EXPERTS - 1):
        lower_rows = lower_rows + jnp.where(expert_col > e, run[e:e + 1, :], 0.0)
    place = earlier + lower_rows
    p1 = jnp.sum(jnp.where(expert == e1, place, 0.0), axis=0, keepdims=True)
    p2 = jnp.sum(jnp.where(expert == e2, place, 0.0), axis=0, keepdims=True)

    records = {META_E1: e1, META_E2: e2, META_W1: w1, META_W2: w2, META_P1: p1, META_P2: p2}
    meta_ref[...] = jnp.concatenate(
        [records.get(row, jnp.zeros((1, ts), F32)) for row in range(SUBLANES)], axis=0)
    cnt_ref[...] = jnp.broadcast_to(run, (N_EXPERTS, LANES))


def _odd_mixer(x, w_in, conv_w, sgu_norm_g, sgu_norm_b, sgu_w, sgu_b, w_out, ln_g, ln_b, router):
    ts = TS_MIX
    x_spec = pl.BlockSpec((None, ts, D_MODEL), lambda b, s: (b, s, 0))
    router_pad = jnp.zeros((ROUTER_ROWS, D_MODEL), F32).at[:N_EXPERTS, :].set(router.T)
    router_hi = router_pad.astype(BF16)
    router_lo = (router_pad - router_hi.astype(F32)).astype(BF16)
    return pl.pallas_call(
        _odd_mixer_kernel,
        grid=(BATCH, SEQ // ts),
        in_specs=[
            x_spec,
            _const_spec((D_MODEL, IN_ODD)),
            _const_spec((CONV_C_WIDTH, D_C)),
            _const_spec((1, D_D)),
            _const_spec((1, D_D)),
            _const_spec((SGU_HEADS, SGU_BLOCK, SGU_BLOCK)),
            _const_spec((SGU_BLOCK, SGU_HEADS)),
            _const_spec((D_C + D_D, D_MODEL)),
            _const_spec((1, D_MODEL)),
            _const_spec((1, D_MODEL)),
            _const_spec((ROUTER_ROWS, D_MODEL)),
            _const_spec((ROUTER_ROWS, D_MODEL)),
        ],
        out_specs=[
            x_spec,
            pl.BlockSpec((SUBLANES, ts), lambda b, s: (0, b * (SEQ // ts) + s)),
            pl.BlockSpec((None, N_EXPERTS, LANES), lambda b, s: (b * (SEQ // ts) + s, 0, 0)),
        ],
        out_shape=[
            jax.ShapeDtypeStruct((BATCH, SEQ, D_MODEL), F32),
            jax.ShapeDtypeStruct((SUBLANES, TOKENS), F32),
            jax.ShapeDtypeStruct((N_TOKEN_TILES, N_EXPERTS, LANES), F32),
        ],
        scratch_shapes=[
            pltpu.VMEM((HALO_C + ts, D_C), F32),
            pltpu.VMEM((ts, D_C + D_D), BF16),
        ],
        compiler_params=pltpu.CompilerParams(
            dimension_semantics=("arbitrary", "arbitrary"), vmem_limit_bytes=VMEM_LIMIT_BYTES),
        name="odd_mixer",
    )(x, w_in.astype(BF16), conv_w, sgu_norm_g[None], sgu_norm_b[None], sgu_w, sgu_b.T,
      w_out.astype(BF16), ln_g[None], ln_b[None], router_hi, router_lo)


SEGMENT_BITS = range(RUN_ALIGN.bit_length() - 1, TS_MIX.bit_length())


def _for_each_segment_chunk(seg_ref, tile, visit):
    for e in range(N_EXPERTS):
        k = tile * N_EXPERTS + e
        n = seg_ref[k]
        local = seg_ref[N_SEGMENTS + k]
        sorted_row = seg_ref[2 * N_SEGMENTS + k]
        for bit in reversed(SEGMENT_BITS):
            @pl.when(((n >> bit) & 1) == 1)
            def _():
                done = (n >> (bit + 1)) << (bit + 1)
                visit(pl.multiple_of(local + done, RUN_ALIGN), pl.multiple_of(sorted_row + done, RUN_ALIGN),
                      1 << bit)


def _dispatch_kernel(seg_ref, pad_tile_ref, x_ref, meta_ref, xs_ref, zero_ref, perm_ref, zero_sem, perm_sem):
    i = pl.program_id(0)
    slot = i % 2

    def chunk_copy(which):
        def make(local, sorted_row, size):
            return pltpu.make_async_copy(perm_ref.at[which, pl.ds(local, size)],
                                         xs_ref.at[pl.ds(sorted_row, size)], perm_sem.at[which])
        return make

    def wait_tile(tile, which):
        _for_each_segment_chunk(seg_ref, tile, lambda *chunk: chunk_copy(which)(*chunk).wait())

    @pl.when(i == 0)
    def _():
        zero_ref[...] = jnp.zeros(zero_ref.shape, BF16)

        def clear_tile(t):
            start = pl.multiple_of(t * TM_MOE, TM_MOE)
            cp = pltpu.make_async_copy(zero_ref, xs_ref.at[pl.ds(start, TM_MOE)], zero_sem)
            cp.start()
            cp.wait()

        for e in range(N_EXPERTS):
            @pl.when(pad_tile_ref[e] >= 0)
            def _():
                clear_tile(pad_tile_ref[e])

        def clear_tail(t, carry):
            clear_tile(t)
            return carry

        lax.fori_loop(pad_tile_ref[N_EXPERTS], N_TILES_MOE, clear_tail, 0)

    meta = meta_ref[...]
    row = lax.broadcasted_iota(jnp.int32, (SLOT_ROWS, TS_MIX), 0).astype(F32)
    hit = jnp.logical_or(row == meta[META_P1:META_P1 + 1, :], row == meta[META_P2:META_P2 + 1, :])
    onehot = jnp.where(hit, 1.0, 0.0).astype(BF16)
    permuted = _dot(onehot, x_ref[...].astype(BF16))

    @pl.when(i >= 2)
    def _():
        wait_tile(i - 2, slot)

    perm_ref[slot] = permuted.astype(BF16)
    _for_each_segment_chunk(seg_ref, i, lambda *chunk: chunk_copy(slot)(*chunk).start())

    @pl.when(i == N_TOKEN_TILES - 1)
    def _():
        wait_tile(i - 1, 1 - slot)
        wait_tile(i, slot)


def _dispatch(x, meta, seg, pad_tile):
    return pl.pallas_call(
        _dispatch_kernel,
        grid_spec=pltpu.PrefetchScalarGridSpec(
            num_scalar_prefetch=2,
            grid=(N_TOKEN_TILES,),
            in_specs=[pl.BlockSpec((TS_MIX, D_MODEL), lambda i, *_: (i, 0)),
                      pl.BlockSpec((SUBLANES, TS_MIX), lambda i, *_: (0, i))],
            out_specs=pl.BlockSpec(memory_space=pl.ANY),
            scratch_shapes=[
                pltpu.VMEM((TM_MOE, D_MODEL), BF16),
                pltpu.VMEM((2, SLOT_ROWS, D_MODEL), BF16),
                pltpu.SemaphoreType.DMA(()),
                pltpu.SemaphoreType.DMA((2,)),
            ],
        ),
        out_shape=jax.ShapeDtypeStruct((ROWS_SORTED, D_MODEL), BF16),
        compiler_params=pltpu.CompilerParams(
            dimension_semantics=("arbitrary",), vmem_limit_bytes=VMEM_LIMIT_BYTES),
        name="moe_dispatch",
    )(seg, pad_tile, x, meta)


def _grouped_ffn_kernel(tile_expert_ref, tile_rows_ref, n_active_ref, xs_ref, wg_ref, wu_ref, wd_ref,
                        ys_ref, acc_ref):
    i = pl.program_id(0)
    j = pl.program_id(1)

    active = i < n_active_ref[0]
    rows = tile_rows_ref[i]

    uncovered = jnp.logical_or(jnp.logical_not(active), rows <= TM_MOE - SUB_MOE)

    @pl.when(jnp.logical_and(j == 0, jnp.logical_or(i == 0, uncovered)))
    def _():
        acc_ref[...] = jnp.zeros((TM_MOE, D_MODEL), F32)

    def accumulate(n_rows):
        xb = xs_ref[0:n_rows, :]
        a = _silu(_dot(xb, wg_ref[...].astype(BF16))) * _dot(xb, wu_ref[...].astype(BF16))
        part = _dot(a.astype(BF16), wd_ref[...].astype(BF16))
        acc_ref[0:n_rows, :] = jnp.where(j == 0, part, acc_ref[0:n_rows, :] + part)

    for n_sub in range(1, TM_MOE // SUB_MOE + 1):
        lo, hi = (n_sub - 1) * SUB_MOE, n_sub * SUB_MOE

        @pl.when(jnp.logical_and(active, jnp.logical_and(rows > lo, rows <= hi)))
        def _():
            accumulate(hi)

    @pl.when(j == pl.num_programs(1) - 1)
    def _():
        ys_ref[...] = acc_ref[...].astype(BF16)


def _grouped_ffn(xs, w_gate, w_up, w_down, tile_expert, tile_rows, n_active):
    nj = D_FF_EXPERT // TF_MOE

    def row_map(i, j, te, tr, na):
        return (jnp.minimum(i, na[0] - 1), 0)

    def frozen_j(i, j, na):
        return jnp.where(i < na[0], j, nj - 1)

    def up_map(i, j, te, tr, na):
        return (te[i], 0, frozen_j(i, j, na))

    def down_map(i, j, te, tr, na):
        return (te[i], frozen_j(i, j, na), 0)

    return pl.pallas_call(
        _grouped_ffn_kernel,
        grid_spec=pltpu.PrefetchScalarGridSpec(
            num_scalar_prefetch=3,
            grid=(N_TILES_MOE, nj),
            in_specs=[
                pl.BlockSpec((TM_MOE, D_MODEL), row_map),
                pl.BlockSpec((None, D_MODEL, TF_MOE), up_map),
                pl.BlockSpec((None, D_MODEL, TF_MOE), up_map),
                pl.BlockSpec((None, TF_MOE, D_MODEL), down_map),
            ],
            out_specs=pl.BlockSpec((TM_MOE, D_MODEL), lambda i, j, *_: (i, 0)),
            scratch_shapes=[pltpu.VMEM((TM_MOE, D_MODEL), F32)],
        ),
        out_shape=jax.ShapeDtypeStruct((ROWS_SORTED, D_MODEL), BF16),
        compiler_params=pltpu.CompilerParams(
            dimension_semantics=("arbitrary", "arbitrary"), vmem_limit_bytes=VMEM_LIMIT_BYTES),
        name="moe_grouped_ffn",
    )(tile_expert, tile_rows, n_active, xs, w_gate, w_up, w_down)


def _combine_kernel(seg_ref, x_ref, meta_ref, ys_ref, g_ref, b_ref, o_ref, win_ref, sem):
    i = pl.program_id(0)
    slot = i % 2

    def chunk_copy(which):
        def make(local, sorted_row, size):
            return pltpu.make_async_copy(ys_ref.at[pl.ds(sorted_row, size)],
                                         win_ref.at[which, pl.ds(local, size)], sem.at[which])
        return make

    def fetch(tile, which):
        _for_each_segment_chunk(seg_ref, tile, lambda *chunk: chunk_copy(which)(*chunk).start())

    @pl.when(i == 0)
    def _():
        win_ref[...] = jnp.zeros(win_ref.shape, BF16)
        fetch(0, 0)

    @pl.when(i + 1 < N_TOKEN_TILES)
    def _():
        fetch(i + 1, 1 - slot)

    _for_each_segment_chunk(seg_ref, i, lambda *chunk: chunk_copy(slot)(*chunk).wait())

    meta = meta_ref[...]
    row = lax.broadcasted_iota(jnp.int32, (SLOT_ROWS, TS_MIX), 0).astype(F32)
    gate = (jnp.where(row == meta[META_P1:META_P1 + 1, :], meta[META_W1:META_W1 + 1, :], 0.0)
            + jnp.where(row == meta[META_P2:META_P2 + 1, :], meta[META_W2:META_W2 + 1, :], 0.0))
    moe = lax.dot_general(gate.astype(BF16), win_ref[slot], (((0,), (0,)), ((), ())),
                          preferred_element_type=F32)
    o_ref[...] = _layer_norm(ALPHA * x_ref[...] + moe, g_ref[...], b_ref[...])


def _combine(x, meta, ys, seg, ln_g, ln_b):
    row_spec = pl.BlockSpec((TS_MIX, D_MODEL), lambda i, *_: (i, 0))
    return pl.pallas_call(
        _combine_kernel,
        grid_spec=pltpu.PrefetchScalarGridSpec(
            num_scalar_prefetch=1,
            grid=(N_TOKEN_TILES,),
            in_specs=[
                row_spec,
                pl.BlockSpec((SUBLANES, TS_MIX), lambda i, *_: (0, i)),
                pl.BlockSpec(memory_space=pl.ANY),
                pl.BlockSpec((1, D_MODEL), lambda i, *_: (0, 0)),
                pl.BlockSpec((1, D_MODEL), lambda i, *_: (0, 0)),
            ],
            out_specs=row_spec,
            scratch_shapes=[
                pltpu.VMEM((2, SLOT_ROWS, D_MODEL), BF16),
                pltpu.SemaphoreType.DMA((2,)),
            ],
        ),
        out_shape=jax.ShapeDtypeStruct((TOKENS, D_MODEL), F32),
        compiler_params=pltpu.CompilerParams(
            dimension_semantics=("arbitrary",), vmem_limit_bytes=VMEM_LIMIT_BYTES),
        name="moe_combine",
    )(seg, x, meta, ys, ln_g[None], ln_b[None])


def _routing_tables(tile_counts):
    n = tile_counts[:, :, 0].astype(jnp.int32)
    counts = jnp.sum(n, axis=0)
    tiles = (counts + TM_MOE - 1) // TM_MOE
    tile_end = jnp.cumsum(tiles)
    tile_start = tile_end - tiles
    offset = tile_start * TM_MOE
    local_start = jnp.cumsum(n, axis=1) - n
    sorted_start = offset[None, :] + jnp.cumsum(n, axis=0) - n
    seg = jnp.concatenate([n.reshape(-1), local_start.reshape(-1), sorted_start.reshape(-1)])
    n_active = tile_end[-1:]
    tile_id = jnp.minimum(jnp.arange(N_TILES_MOE, dtype=jnp.int32), n_active[0] - 1)
    tile_expert = jnp.sum(tile_id[:, None] >= tile_end[None, :], axis=1).astype(jnp.int32)
    tile_rows = jnp.clip(counts[tile_expert] - (tile_id - tile_start[tile_expert]) * TM_MOE, 0, TM_MOE)
    pad_tile = jnp.concatenate([jnp.where(tiles > 0, tile_end - 1, -1), n_active]).astype(jnp.int32)
    return (seg.astype(jnp.int32), pad_tile, tile_expert, tile_rows.astype(jnp.int32),
            n_active.astype(jnp.int32))


def _even_layer(x, w_in, conv_a_w, conv_a_b, norm_a_g, norm_a_b, pool_w, pool_scale, w_out,
                ln1_g, ln1_b, ffn_w_gate, ffn_w_up, ffn_w_down, ln2_g, ln2_b):
    x = _even_mixer(x, w_in, conv_a_w, conv_a_b, norm_a_g, norm_a_b, pool_w, pool_scale, w_out,
                    ln1_g, ln1_b)
    x = _dense_ffn(x.reshape(TOKENS, D_MODEL), ffn_w_gate, ffn_w_up, ffn_w_down, ln2_g, ln2_b)
    return x.reshape(BATCH, SEQ, D_MODEL)


def _odd_layer(x, w_in, conv_c_w, sgu_norm_g, sgu_norm_b, sgu_w, sgu_b, w_out, ln1_g, ln1_b,
               router, moe_w_gate, moe_w_up, moe_w_down, ln2_g, ln2_b):
    x, meta, tile_counts = _odd_mixer(x, w_in, conv_c_w, sgu_norm_g, sgu_norm_b, sgu_w, sgu_b, w_out,
                                      ln1_g, ln1_b, router)
    x = x.reshape(TOKENS, D_MODEL)
    seg, pad_tile, tile_expert, tile_rows, n_active = _routing_tables(tile_counts)
    xs = _dispatch(x, meta, seg, pad_tile)
    ys = _grouped_ffn(xs, moe_w_gate, moe_w_up, moe_w_down, tile_expert, tile_rows, n_active)
    x = _combine(x, meta, ys, seg, ln2_g, ln2_b)
    return x.reshape(BATCH, SEQ, D_MODEL)


def kernel(x, even_w_in, even_conv_a_w, even_conv_a_b, even_norm_a_g, even_norm_a_b, even_pool_w, even_pool_scale, even_w_out, even_ln1_g, even_ln1_b, even_ffn_w_gate, even_ffn_w_up, even_ffn_w_down, even_ln2_g, even_ln2_b, odd_w_in, odd_conv_c_w, odd_sgu_norm_g, odd_sgu_norm_b, odd_sgu_w, odd_sgu_b, odd_w_out, odd_ln1_g, odd_ln1_b, odd_router, odd_moe_w_gate, odd_moe_w_up, odd_moe_w_down, odd_ln2_g, odd_ln2_b):
    for layer in range(DEPTH):
        i = layer // 2
        if layer % 2 == 0:
            x = _even_layer(x, even_w_in[i], even_conv_a_w[i], even_conv_a_b[i], even_norm_a_g[i],
                            even_norm_a_b[i], even_pool_w[i], even_pool_scale[i], even_w_out[i],
                            even_ln1_g[i], even_ln1_b[i], even_ffn_w_gate[i], even_ffn_w_up[i],
                            even_ffn_w_down[i], even_ln2_g[i], even_ln2_b[i])
        else:
            x = _odd_layer(x, odd_w_in[i], odd_conv_c_w[i], odd_sgu_norm_g[i], odd_sgu_norm_b[i],
                           odd_sgu_w[i], odd_sgu_b[i], odd_w_out[i], odd_ln1_g[i], odd_ln1_b[i],
                           odd_router[i], odd_moe_w_gate[i], odd_moe_w_up[i], odd_moe_w_down[i],
                           odd_ln2_g[i], odd_ln2_b[i])
    return x
```

```python
import functools

import jax
import jax.numpy as jnp
from jax import lax
from jax.experimental import pallas as pl
from jax.experimental.pallas import tpu as pltpu

F32 = jnp.float32
BF16 = jnp.bfloat16

D_MODEL = 1024
BATCH = 4
SEQ = 4096
TOKENS = BATCH * SEQ
DEPTH = 2

D_A = 512
CONV_A_WIDTH = 31
D_B = 512
POOL_WINDOWS = (2, 4, 8, 16)
POOL_GROUP_DIM = 128
IN_EVEN = 2 * D_A + D_B

D_C = 512
CONV_C_WIDTH = 3
D_D = 512
SGU_BLOCK = 128
SGU_HEADS = 4
SGU_HEAD_DIM = 128
IN_ODD = 3 * D_C + 2 * D_D

D_FF_DENSE = 2816
N_EXPERTS = 8
D_FF_EXPERT = 3584

ALPHA = (2 * DEPTH) ** 0.25
LN_EPS = 1e-5

SUBLANES = 8
LANES = 128
VMEM_LIMIT_BYTES = 56 * 1024 * 1024

TS_MIX = 512
EVEN_BANDS = 2
MIX_CHAINS = 2
ROW_SET_STRIDE = 4
CONV_SETS = 8
HALO_A = 32
HALO_B = 16
HALO_C = 8
TM_FFN = 1024
FF_CHUNK = 768
TM_MOE = 1024
SUB_MOE = 128
TF_MOE = 512
N_TOKEN_TILES = TOKENS // TS_MIX
N_SEGMENTS = N_TOKEN_TILES * N_EXPERTS
RUN_ALIGN = 16
SLOT_ROWS = 2 * TS_MIX + N_EXPERTS * RUN_ALIGN
MAX_SORTED_ROWS = 2 * TOKENS + N_SEGMENTS * (RUN_ALIGN - 1)
N_TILES_MOE = (MAX_SORTED_ROWS + N_EXPERTS * (TM_MOE - 1)) // TM_MOE
ROWS_SORTED = N_TILES_MOE * TM_MOE


def _layer_norm(x, g, b):
    mu = jnp.mean(x, axis=-1, keepdims=True)
    xc = x - mu
    var = jnp.mean(xc * xc, axis=-1, keepdims=True)
    return xc * lax.rsqrt(var + LN_EPS) * g + b


def _silu(x):
    return x * jax.nn.sigmoid(x)


def _dot(a, b):
    return jnp.dot(a, b, preferred_element_type=F32)


def _even_mixer_kernel(x_ref, win_ref, cw_ref, cb_ref, nag_ref, nab_ref, pw_ref, ps_ref,
                       wout_ref, g1_ref, b1_ref, o_ref, exta_ref, extb_ref, mixf_ref, mixin_ref):
    s = pl.program_id(1)
    ts = TS_MIX

    n_a, n_b = D_A // LANES, D_B // LANES

    @pl.when(s == 0)
    def _():
        exta_ref[:, 0:HALO_A, :] = jnp.zeros((n_a, HALO_A, LANES), F32)
        extb_ref[:, 0:HALO_B, :] = jnp.zeros((n_b, HALO_B, LANES), F32)

    def row_set(first):
        return pl.ds(first, SUBLANES, stride=ROW_SET_STRIDE)

    set_firsts = [blk * SUBLANES * ROW_SET_STRIDE + r
                  for blk in range(ts // (SUBLANES * ROW_SET_STRIDE)) for r in range(ROW_SET_STRIDE)]
    set_step = ROW_SET_STRIDE * lax.broadcasted_iota(jnp.int32, (SUBLANES, 1), 0)

    def pool_sets(firsts):
        for first in firsts:
            row_pos = s * ts + first + set_step
            for g, win in enumerate(POOL_WINDOWS):
                frame = extb_ref[g, row_set(HALO_B + first), :]
                wsum = frame
                for i in range(1, win):
                    wsum = wsum + extb_ref[g, row_set(HALO_B + first - i), :]
                if first >= win - 1:
                    mean = wsum * (1.0 / win)
                else:
                    mean = wsum / jnp.minimum(row_pos + 1, win).astype(F32)
                mixf_ref[n_a + g, row_set(first), :] = mean - frame

    x = x_ref[...]
    xb = x.astype(BF16)
    hb = _dot(xb, win_ref[:, 2 * D_A:])
    for cb in range(n_b):
        extb_ref[cb, HALO_B:HALO_B + ts, :] = hb[:, cb * LANES:(cb + 1) * LANES]
    parts = 2
    cols = D_A // parts
    sets_per_part = len(set_firsts) // parts
    for part in range(parts):
        pool_sets(set_firsts[part * sets_per_part:(part + 1) * sets_per_part])
        val = _dot(xb, win_ref[:, part * cols:(part + 1) * cols])
        gate = _dot(xb, win_ref[:, D_A + part * cols:D_A + (part + 1) * cols])
        glu = val * jax.nn.sigmoid(gate)
        for cb in range(cols // LANES):
            exta_ref[part * (cols // LANES) + cb, HALO_A:HALO_A + ts, :] = glu[:, cb * LANES:(cb + 1) * LANES]

    for g in range(n_b):
        lo, hi = g * POOL_GROUP_DIM, (g + 1) * POOL_GROUP_DIM
        mixed = _dot(mixf_ref[n_a + g].astype(BF16), pw_ref[g]) * ps_ref[:, lo:hi]
        mixin_ref[:, D_A + lo:D_A + hi] = mixed.astype(BF16)
    mix_pool = _dot(mixin_ref[:, D_A:], wout_ref[D_A:, :])

    band_rows = ts // EVEN_BANDS
    sets_per_band = len(set_firsts) // EVEN_BANDS
    norm_rows = 128
    for band in range(EVEN_BANDS):
        band_sets = set_firsts[band * sets_per_band:(band + 1) * sets_per_band]
        for cb in range(n_a):
            lanes = slice(cb * LANES, (cb + 1) * LANES)
            for group in range(0, len(band_sets), CONV_SETS):
                firsts = band_sets[group:group + CONV_SETS]
                accs = [jnp.broadcast_to(cb_ref[:, lanes], (SUBLANES, LANES))] * len(firsts)
                for k in range(CONV_A_WIDTH):
                    wk = jnp.broadcast_to(cw_ref[k:k + 1, lanes], (SUBLANES, LANES))
                    back = CONV_A_WIDTH - 1 - k
                    for n, first in enumerate(firsts):
                        accs[n] = accs[n] + wk * exta_ref[cb, row_set(HALO_A + first - back), :]
                for n, first in enumerate(firsts):
                    mixf_ref[cb, row_set(first), :] = accs[n]

        b0, b1 = band * band_rows, (band + 1) * band_rows
        for r0 in range(b0, b1, norm_rows):
            conv = jnp.concatenate([mixf_ref[cb, r0:r0 + norm_rows, :] for cb in range(n_a)], axis=1)
            y = _silu(_layer_norm(conv, nag_ref[...], nab_ref[...]))
            mixin_ref[r0:r0 + norm_rows, 0:D_A] = y.astype(BF16)
        mix = _dot(mixin_ref[b0:b1, 0:D_A], wout_ref[0:D_A, :]) + mix_pool[b0:b1, :]
        o_ref[b0:b1, :] = _layer_norm(ALPHA * x[b0:b1, :] + mix, g1_ref[...], b1_ref[...])

    exta_ref[:, 0:HALO_A, :] = exta_ref[:, ts:ts + HALO_A, :]
    extb_ref[:, 0:HALO_B, :] = extb_ref[:, ts:ts + HALO_B, :]


def _const_spec(shape):
    return pl.BlockSpec(shape, lambda *_: (0,) * len(shape))


def _even_mixer(x, w_in, conv_w, conv_b, norm_g, norm_b, pool_w, pool_scale, w_out, ln_g, ln_b):
    ts = TS_MIX
    x_spec = pl.BlockSpec((None, ts, D_MODEL), lambda b, s: (b, s, 0))
    return pl.pallas_call(
        _even_mixer_kernel,
        grid=(BATCH, SEQ // ts),
        in_specs=[
            x_spec,
            _const_spec((D_MODEL, IN_EVEN)),
            _const_spec((CONV_A_WIDTH, D_A)),
            _const_spec((1, D_A)),
            _const_spec((1, D_A)),
            _const_spec((1, D_A)),
            _const_spec((len(POOL_WINDOWS), POOL_GROUP_DIM, POOL_GROUP_DIM)),
            _const_spec((1, D_B)),
            _const_spec((D_A + D_B, D_MODEL)),
            _const_spec((1, D_MODEL)),
            _const_spec((1, D_MODEL)),
        ],
        out_specs=x_spec,
        out_shape=jax.ShapeDtypeStruct((BATCH, SEQ, D_MODEL), F32),
        scratch_shapes=[
            pltpu.VMEM((D_A // LANES, HALO_A + ts, LANES), F32),
            pltpu.VMEM((D_B // LANES, HALO_B + ts, LANES), F32),
            pltpu.VMEM(((D_A + D_B) // LANES, ts, LANES), F32),
            pltpu.VMEM((ts, D_A + D_B), BF16),
        ],
        compiler_params=pltpu.CompilerParams(
            dimension_semantics=("arbitrary", "arbitrary"), vmem_limit_bytes=VMEM_LIMIT_BYTES),
        name="even_mixer",
    )(x, w_in.astype(BF16), conv_w, conv_b[None], norm_g[None], norm_b[None],
      pool_w.astype(BF16), pool_scale[None], w_out.astype(BF16), ln_g[None], ln_b[None])


def _ff_chunks(total, chunk):
    bounds = list(range(0, total, chunk)) + [total]
    return list(zip(bounds[:-1], bounds[1:]))


def _dense_ffn_kernel(x_ref, wg_ref, wu_ref, wd_ref, g_ref, b_ref, o_ref, acc_ref):
    x = x_ref[...]
    xb = x.astype(BF16)
    for n, (lo, hi) in enumerate(_ff_chunks(D_FF_DENSE, FF_CHUNK)):
        a = _silu(_dot(xb, wg_ref[:, lo:hi])) * _dot(xb, wu_ref[:, lo:hi])
        part = _dot(a.astype(BF16), wd_ref[lo:hi, :])
        if n == 0:
            acc_ref[...] = part
        else:
            acc_ref[...] += part
    o_ref[...] = _layer_norm(ALPHA * x + acc_ref[...], g_ref[...], b_ref[...])


def _dense_ffn(x, w_gate, w_up, w_down, ln_g, ln_b):
    tm = TM_FFN
    row_spec = pl.BlockSpec((tm, D_MODEL), lambda i: (i, 0))
    resident = pl.BlockSpec(memory_space=pltpu.VMEM)
    return pl.pallas_call(
        _dense_ffn_kernel,
        grid=(TOKENS // tm,),
        in_specs=[row_spec, resident, resident, resident,
                  _const_spec((1, D_MODEL)), _const_spec((1, D_MODEL))],
        out_specs=row_spec,
        out_shape=jax.ShapeDtypeStruct((TOKENS, D_MODEL), F32),
        scratch_shapes=[pltpu.VMEM((tm, D_MODEL), F32)],
        compiler_params=pltpu.CompilerParams(
            dimension_semantics=("arbitrary",), vmem_limit_bytes=VMEM_LIMIT_BYTES),
        name="dense_ffn",
    )(x, w_gate.astype(BF16), w_up.astype(BF16), w_down.astype(BF16), ln_g[None], ln_b[None])


META_E1, META_E2, META_W1, META_W2, META_P1, META_P2 = range(6)
ROUTER_ROWS = 16


def _odd_mixer_kernel(x_ref, win_ref, ccw_ref, sng_ref, snb_ref, sw_ref, sb_ref, wout_ref,
                      g1_ref, b1_ref, rtr_ref, o_ref, meta_ref, cnt_ref,
                      extc_ref, mixin_ref):
    s = pl.program_id(1)
    ts = TS_MIX

    @pl.when(s == 0)
    def _():
        extc_ref[0:HALO_C, :] = jnp.zeros((HALO_C, D_C), F32)

    tri = (lax.broadcasted_iota(jnp.int32, (SGU_BLOCK, SGU_BLOCK), 0)
           >= lax.broadcasted_iota(jnp.int32, (SGU_BLOCK, SGU_BLOCK), 1))
    sgu_w = [jnp.where(tri, sw_ref[hd], 0.0).astype(BF16) for hd in range(SGU_HEADS)]

    chain_rows = ts // MIX_CHAINS
    blocks = chain_rows // SGU_BLOCK
    x1_parts = []
    for c in range(MIX_CHAINS):
        c0, c1 = c * chain_rows, (c + 1) * chain_rows
        x = x_ref[c0:c1, :]
        h = _dot(x.astype(BF16), win_ref[...])

        extc_ref[HALO_C + c0:HALO_C + c1, :] = h[:, D_C:2 * D_C] * h[:, 2 * D_C:3 * D_C]
        conv = ccw_ref[0:1, :] * extc_ref[HALO_C - 2 + c0:HALO_C - 2 + c1, :]
        conv = conv + ccw_ref[1:2, :] * extc_ref[HALO_C - 1 + c0:HALO_C - 1 + c1, :]
        conv = conv + ccw_ref[2:3, :] * extc_ref[HALO_C + c0:HALO_C + c1, :]
        mixin_ref[c0:c1, 0:D_C] = (h[:, 0:D_C] * conv).astype(BF16)

        z = jax.nn.gelu(h[:, 3 * D_C:])
        u = z[:, :D_D]
        v = _layer_norm(z[:, D_D:], sng_ref[...], snb_ref[...]).astype(BF16)
        for hd in range(SGU_HEADS):
            lo, hi = hd * SGU_HEAD_DIM, (hd + 1) * SGU_HEAD_DIM
            v_blocks = jnp.concatenate(
                [v[blk * SGU_BLOCK:(blk + 1) * SGU_BLOCK, lo:hi] for blk in range(blocks)], axis=1)
            mixed = _dot(sgu_w[hd], v_blocks) + sb_ref[:, hd:hd + 1]
            for blk in range(blocks):
                r0, r1 = blk * SGU_BLOCK, (blk + 1) * SGU_BLOCK
                gated = u[r0:r1, lo:hi] * mixed[:, blk * SGU_HEAD_DIM:(blk + 1) * SGU_HEAD_DIM]
                mixin_ref[c0 + r0:c0 + r1, D_C + lo:D_C + hi] = gated.astype(BF16)

        mix = _dot(mixin_ref[c0:c1, :], wout_ref[...])
        x1_c = _layer_norm(ALPHA * x + mix, g1_ref[...], b1_ref[...])
        o_ref[c0:c1, :] = x1_c
        x1_parts.append(x1_c)
    x1 = jnp.concatenate(x1_parts, axis=0)

    extc_ref[0:HALO_C, :] = extc_ref[ts:ts + HALO_C, :]

    x_hi = x1.astype(BF16)
    x_lo = (x1 - x_hi.astype(F32)).astype(BF16)
    nt = (((1,), (1,)), ((), ()))
    by_hi = lax.dot_general(rtr_ref[...], x_hi, nt, preferred_element_type=F32)
    by_lo = lax.dot_general(rtr_ref[0:ROUTER_ROWS, :], x_lo, nt, preferred_element_type=F32)
    logits = (by_hi[0:N_EXPERTS, :] + by_hi[ROUTER_ROWS:ROUTER_ROWS + N_EXPERTS, :]
              + by_lo[0:N_EXPERTS, :])

    expert = lax.broadcasted_iota(jnp.int32, (N_EXPERTS, ts), 0).astype(F32)
    neg_inf = jnp.float32(-jnp.inf)
    m1 = jnp.max(logits, axis=0, keepdims=True)
    e1 = jnp.min(jnp.where(logits == m1, expert, float(N_EXPERTS)), axis=0, keepdims=True)
    rest = jnp.where(expert == e1, neg_inf, logits)
    m2 = jnp.max(rest, axis=0, keepdims=True)
    e2 = jnp.min(jnp.where(rest == m2, expert, float(N_EXPERTS)), axis=0, keepdims=True)
    ex = jnp.exp(m2 - m1)
    den = 1.0 + ex
    w1 = 1.0 / den
    w2 = ex / den

    onehot = jnp.where(jnp.logical_or(expert == e1, expert == e2), 1.0, 0.0)
    onehot_b = jnp.concatenate([onehot, jnp.zeros_like(onehot)], axis=0).astype(BF16)
    before = (lax.broadcasted_iota(jnp.int32, (ts, ts), 0)
              < lax.broadcasted_iota(jnp.int32, (ts, ts), 1))
    earlier = _dot(onehot_b, jnp.where(before, 1.0, 0.0).astype(BF16))[0:N_EXPERTS, :]
    run = jnp.sum(onehot, axis=1, keepdims=True)
    run = jnp.floor((run + (RUN_ALIGN - 1)) * (1.0 / RUN_ALIGN)) * RUN_ALIGN
    expert_col = expert[:, 0:1]
    lower_rows = jnp.zeros((N_EXPERTS, 1), F32)
    for e in range(N_EXPERTS - 1):
        lower_rows = lower_rows + jnp.where(expert_col > e, run[e:e + 1, :], 0.0)
    place = earlier + lower_rows
    p1 = jnp.sum(jnp.where(expert == e1, place, 0.0), axis=0, keepdims=True)
    p2 = jnp.sum(jnp.where(expert == e2, place, 0.0), axis=0, keepdims=True)

    records = {META_E1: e1, META_E2: e2, META_W1: w1, META_W2: w2, META_P1: p1, META_P2: p2}
    meta_ref[...] = jnp.concatenate(
        [records.get(row, jnp.zeros((1, ts), F32)) for row in range(SUBLANES)], axis=0)
    cnt_ref[...] = jnp.broadcast_to(run, (N_EXPERTS, LANES))


def _odd_mixer(x, w_in, conv_w, sgu_norm_g, sgu_norm_b, sgu_w, sgu_b, w_out, ln_g, ln_b, router):
    ts = TS_MIX
    x_spec = pl.BlockSpec((None, ts, D_MODEL), lambda b, s: (b, s, 0))
    router_pad = jnp.zeros((ROUTER_ROWS, D_MODEL), F32).at[:N_EXPERTS, :].set(router.T)
    router_hi = router_pad.astype(BF16)
    router_lo = (router_pad - router_hi.astype(F32)).astype(BF16)
    router_parts = jnp.concatenate([router_hi, router_lo], axis=0)
    return pl.pallas_call(
        _odd_mixer_kernel,
        grid=(BATCH, SEQ // ts),
        in_specs=[
            x_spec,
            _const_spec((D_MODEL, IN_ODD)),
            _const_spec((CONV_C_WIDTH, D_C)),
            _const_spec((1, D_D)),
            _const_spec((1, D_D)),
            _const_spec((SGU_HEADS, SGU_BLOCK, SGU_BLOCK)),
            _const_spec((SGU_BLOCK, SGU_HEADS)),
            _const_spec((D_C + D_D, D_MODEL)),
            _const_spec((1, D_MODEL)),
            _const_spec((1, D_MODEL)),
            _const_spec((2 * ROUTER_ROWS, D_MODEL)),
        ],
        out_specs=[
            x_spec,
            pl.BlockSpec((SUBLANES, ts), lambda b, s: (0, b * (SEQ // ts) + s)),
            pl.BlockSpec((None, N_EXPERTS, LANES), lambda b, s: (b * (SEQ // ts) + s, 0, 0)),
        ],
        out_shape=[
            jax.ShapeDtypeStruct((BATCH, SEQ, D_MODEL), F32),
            jax.ShapeDtypeStruct((SUBLANES, TOKENS), F32),
            jax.ShapeDtypeStruct((N_TOKEN_TILES, N_EXPERTS, LANES), F32),
        ],
        scratch_shapes=[
            pltpu.VMEM((HALO_C + ts, D_C), F32),
            pltpu.VMEM((ts, D_C + D_D), BF16),
        ],
        compiler_params=pltpu.CompilerParams(
            dimension_semantics=("arbitrary", "arbitrary"), vmem_limit_bytes=VMEM_LIMIT_BYTES),
        name="odd_mixer",
    )(x, w_in.astype(BF16), conv_w, sgu_norm_g[None], sgu_norm_b[None], sgu_w, sgu_b.T,
      w_out.astype(BF16), ln_g[None], ln_b[None], router_parts)


SEGMENT_BITS = range(RUN_ALIGN.bit_length() - 1, TS_MIX.bit_length())


def _for_each_segment_chunk(seg_ref, tile, visit):
    for e in range(N_EXPERTS):
        k = tile * N_EXPERTS + e
        n = seg_ref[k]
        local = seg_ref[N_SEGMENTS + k]
        sorted_row = seg_ref[2 * N_SEGMENTS + k]
        for bit in reversed(SEGMENT_BITS):
            @pl.when(((n >> bit) & 1) == 1)
            def _():
                done = (n >> (bit + 1)) << (bit + 1)
                visit(pl.multiple_of(local + done, RUN_ALIGN), pl.multiple_of(sorted_row + done, RUN_ALIGN),
                      1 << bit)


def _dispatch_kernel(seg_ref, pad_tile_ref, x_ref, meta_ref, xs_ref, zero_ref, perm_ref, zero_sem, perm_sem):
    i = pl.program_id(0)
    slot = i % 2

    def chunk_copy(which):
        def make(local, sorted_row, size):
            return pltpu.make_async_copy(perm_ref.at[which, pl.ds(local, size)],
                                         xs_ref.at[pl.ds(sorted_row, size)], perm_sem.at[which])
        return make

    def wait_tile(tile, which):
        _for_each_segment_chunk(seg_ref, tile, lambda *chunk: chunk_copy(which)(*chunk).wait())

    @pl.when(i == 0)
    def _():
        zero_ref[...] = jnp.zeros(zero_ref.shape, BF16)

        def clear_tile(t):
            start = pl.multiple_of(t * TM_MOE, TM_MOE)
            cp = pltpu.make_async_copy(zero_ref, xs_ref.at[pl.ds(start, TM_MOE)], zero_sem)
            cp.start()
            cp.wait()

        for e in range(N_EXPERTS):
            @pl.when(pad_tile_ref[e] >= 0)
            def _():
                clear_tile(pad_tile_ref[e])

        def clear_tail(t, carry):
            clear_tile(t)
            return carry

        lax.fori_loop(pad_tile_ref[N_EXPERTS], N_TILES_MOE, clear_tail, 0)

    meta = meta_ref[...]
    row = lax.broadcasted_iota(jnp.int32, (SLOT_ROWS, TS_MIX), 0).astype(F32)
    hit = jnp.logical_or(row == meta[META_P1:META_P1 + 1, :], row == meta[META_P2:META_P2 + 1, :])
    onehot = jnp.where(hit, 1.0, 0.0).astype(BF16)
    permuted = _dot(onehot, x_ref[...].astype(BF16))

    @pl.when(i >= 2)
    def _():
        wait_tile(i - 2, slot)

    perm_ref[slot] = permuted.astype(BF16)
    _for_each_segment_chunk(seg_ref, i, lambda *chunk: chunk_copy(slot)(*chunk).start())

    @pl.when(i == N_TOKEN_TILES - 1)
    def _():
        wait_tile(i - 1, 1 - slot)
        wait_tile(i, slot)


def _dispatch(x, meta, seg, pad_tile):
    return pl.pallas_call(
        _dispatch_kernel,
        grid_spec=pltpu.PrefetchScalarGridSpec(
            num_scalar_prefetch=2,
            grid=(N_TOKEN_TILES,),
            in_specs=[pl.BlockSpec((TS_MIX, D_MODEL), lambda i, *_: (i, 0)),
                      pl.BlockSpec((SUBLANES, TS_MIX), lambda i, *_: (0, i))],
            out_specs=pl.BlockSpec(memory_space=pl.ANY),
            scratch_shapes=[
                pltpu.VMEM((TM_MOE, D_MODEL), BF16),
                pltpu.VMEM((2, SLOT_ROWS, D_MODEL), BF16),
                pltpu.SemaphoreType.DMA(()),
                pltpu.SemaphoreType.DMA((2,)),
            ],
        ),
        out_shape=jax.ShapeDtypeStruct((ROWS_SORTED, D_MODEL), BF16),
        compiler_params=pltpu.CompilerParams(
            dimension_semantics=("arbitrary",), vmem_limit_bytes=VMEM_LIMIT_BYTES),
        name="moe_dispatch",
    )(seg, pad_tile, x, meta)


def _grouped_ffn_kernel(tile_expert_ref, tile_rows_ref, n_active_ref, xs_ref, wg_ref, wu_ref, wd_ref,
                        ys_ref, acc_ref):
    i = pl.program_id(0)
    j = pl.program_id(1)

    active = i < n_active_ref[0]
    rows = tile_rows_ref[i]

    uncovered = jnp.logical_or(jnp.logical_not(active), rows <= TM_MOE - SUB_MOE)

    @pl.when(jnp.logical_and(j == 0, jnp.logical_or(i == 0, uncovered)))
    def _():
        acc_ref[...] = jnp.zeros((TM_MOE, D_MODEL), F32)

    def accumulate(n_rows):
        xb = xs_ref[0:n_rows, :]
        a = _silu(_dot(xb, wg_ref[...].astype(BF16))) * _dot(xb, wu_ref[...].astype(BF16))
        part = _dot(a.astype(BF16), wd_ref[...].astype(BF16))
        acc_ref[0:n_rows, :] = jnp.where(j == 0, part, acc_ref[0:n_rows, :] + part)

    for n_sub in range(1, TM_MOE // SUB_MOE + 1):
        lo, hi = (n_sub - 1) * SUB_MOE, n_sub * SUB_MOE

        @pl.when(jnp.logical_and(active, jnp.logical_and(rows > lo, rows <= hi)))
        def _():
            accumulate(hi)

    @pl.when(j == pl.num_programs(1) - 1)
    def _():
        ys_ref[...] = acc_ref[...].astype(BF16)


def _grouped_ffn(xs, w_gate, w_up, w_down, tile_expert, tile_rows, n_active):
    nj = D_FF_EXPERT // TF_MOE

    def row_map(i, j, te, tr, na):
        return (jnp.minimum(i, na[0] - 1), 0)

    def frozen_j(i, j, na):
        return jnp.where(i < na[0], j, nj - 1)

    def up_map(i, j, te, tr, na):
        return (te[i], 0, frozen_j(i, j, na))

    def down_map(i, j, te, tr, na):
        return (te[i], frozen_j(i, j, na), 0)

    return pl.pallas_call(
        _grouped_ffn_kernel,
        grid_spec=pltpu.PrefetchScalarGridSpec(
            num_scalar_prefetch=3,
            grid=(N_TILES_MOE, nj),
            in_specs=[
                pl.BlockSpec((TM_MOE, D_MODEL), row_map),
                pl.BlockSpec((None, D_MODEL, TF_MOE), up_map),
                pl.BlockSpec((None, D_MODEL, TF_MOE), up_map),
                pl.BlockSpec((None, TF_MOE, D_MODEL), down_map),
            ],
            out_specs=pl.BlockSpec((TM_MOE, D_MODEL), lambda i, j, *_: (i, 0)),
            scratch_shapes=[pltpu.VMEM((TM_MOE, D_MODEL), F32)],
        ),
        out_shape=jax.ShapeDtypeStruct((ROWS_SORTED, D_MODEL), BF16),
        compiler_params=pltpu.CompilerParams(
            dimension_semantics=("arbitrary", "arbitrary"), vmem_limit_bytes=VMEM_LIMIT_BYTES),
        name="moe_grouped_ffn",
    )(tile_expert, tile_rows, n_active, xs, w_gate, w_up, w_down)


def _combine_kernel(seg_ref, x_ref, meta_ref, ys_ref, g_ref, b_ref, o_ref, win_ref, sem):
    i = pl.program_id(0)
    slot = i % 2

    def chunk_copy(which):
        def make(local, sorted_row, size):
            return pltpu.make_async_copy(ys_ref.at[pl.ds(sorted_row, size)],
                                         win_ref.at[which, pl.ds(local, size)], sem.at[which])
        return make

    def fetch(tile, which):
        _for_each_segment_chunk(seg_ref, tile, lambda *chunk: chunk_copy(which)(*chunk).start())

    @pl.when(i == 0)
    def _():
        win_ref[...] = jnp.zeros(win_ref.shape, BF16)
        fetch(0, 0)

    @pl.when(i + 1 < N_TOKEN_TILES)
    def _():
        fetch(i + 1, 1 - slot)

    _for_each_segment_chunk(seg_ref, i, lambda *chunk: chunk_copy(slot)(*chunk).wait())

    meta = meta_ref[...]
    row = lax.broadcasted_iota(jnp.int32, (SLOT_ROWS, TS_MIX), 0).astype(F32)
    gate = (jnp.where(row == meta[META_P1:META_P1 + 1, :], meta[META_W1:META_W1 + 1, :], 0.0)
            + jnp.where(row == meta[META_P2:META_P2 + 1, :], meta[META_W2:META_W2 + 1, :], 0.0))
    moe = lax.dot_general(gate.astype(BF16), win_ref[slot], (((0,), (0,)), ((), ())),
                          preferred_element_type=F32)
    o_ref[...] = _layer_norm(ALPHA * x_ref[...] + moe, g_ref[...], b_ref[...])


def _combine(x, meta, ys, seg, ln_g, ln_b):
    row_spec = pl.BlockSpec((TS_MIX, D_MODEL), lambda i, *_: (i, 0))
    return pl.pallas_call(
        _combine_kernel,
        grid_spec=pltpu.PrefetchScalarGridSpec(
            num_scalar_prefetch=1,
            grid=(N_TOKEN_TILES,),
            in_specs=[
                row_spec,
                pl.BlockSpec((SUBLANES, TS_MIX), lambda i, *_: (0, i)),
                pl.BlockSpec(memory_space=pl.ANY),
                pl.BlockSpec((1, D_MODEL), lambda i, *_: (0, 0)),
                pl.BlockSpec((1, D_MODEL), lambda i, *_: (0, 0)),
            ],
            out_specs=row_spec,
            scratch_shapes=[
                pltpu.VMEM((2, SLOT_ROWS, D_MODEL), BF16),
                pltpu.SemaphoreType.DMA((2,)),
            ],
        ),
        out_shape=jax.ShapeDtypeStruct((TOKENS, D_MODEL), F32),
        compiler_params=pltpu.CompilerParams(
            dimension_semantics=("arbitrary",), vmem_limit_bytes=VMEM_LIMIT_BYTES),
        name="moe_combine",
    )(seg, x, meta, ys, ln_g[None], ln_b[None])


def _routing_tables(tile_counts):
    n = tile_counts[:, :, 0].astype(jnp.int32)
    counts = jnp.sum(n, axis=0)
    tiles = (counts + TM_MOE - 1) // TM_MOE
    tile_end = jnp.cumsum(tiles)
    tile_start = tile_end - tiles
    offset = tile_start * TM_MOE
    local_start = jnp.cumsum(n, axis=1) - n
    sorted_start = offset[None, :] + jnp.cumsum(n, axis=0) - n
    seg = jnp.concatenate([n.reshape(-1), local_start.reshape(-1), sorted_start.reshape(-1)])
    n_active = tile_end[-1:]
    tile_id = jnp.minimum(jnp.arange(N_TILES_MOE, dtype=jnp.int32), n_active[0] - 1)
    tile_expert = jnp.sum(tile_id[:, None] >= tile_end[None, :], axis=1).astype(jnp.int32)
    tile_rows = jnp.clip(counts[tile_expert] - (tile_id - tile_start[tile_expert]) * TM_MOE, 0, TM_MOE)
    pad_tile = jnp.concatenate([jnp.where(tiles > 0, tile_end - 1, -1), n_active]).astype(jnp.int32)
    return (seg.astype(jnp.int32), pad_tile, tile_expert, tile_rows.astype(jnp.int32),
            n_active.astype(jnp.int32))


def _even_layer(x, w_in, conv_a_w, conv_a_b, norm_a_g, norm_a_b, pool_w, pool_scale, w_out,
                ln1_g, ln1_b, ffn_w_gate, ffn_w_up, ffn_w_down, ln2_g, ln2_b):
    x = _even_mixer(x, w_in, conv_a_w, conv_a_b, norm_a_g, norm_a_b, pool_w, pool_scale, w_out,
                    ln1_g, ln1_b)
    x = _dense_ffn(x.reshape(TOKENS, D_MODEL), ffn_w_gate, ffn_w_up, ffn_w_down, ln2_g, ln2_b)
    return x.reshape(BATCH, SEQ, D_MODEL)


def _odd_layer(x, w_in, conv_c_w, sgu_norm_g, sgu_norm_b, sgu_w, sgu_b, w_out, ln1_g, ln1_b,
               router, moe_w_gate, moe_w_up, moe_w_down, ln2_g, ln2_b):
    x, meta, tile_counts = _odd_mixer(x, w_in, conv_c_w, sgu_norm_g, sgu_norm_b, sgu_w, sgu_b, w_out,
                                      ln1_g, ln1_b, router)
    x = x.reshape(TOKENS, D_MODEL)
    seg, pad_tile, tile_expert, tile_rows, n_active = _routing_tables(tile_counts)
    xs = _dispatch(x, meta, seg, pad_tile)
    ys = _grouped_ffn(xs, moe_w_gate, moe_w_up, moe_w_down, tile_expert, tile_rows, n_active)
    x = _combine(x, meta, ys, seg, ln2_g, ln2_b)
    return x.reshape(BATCH, SEQ, D_MODEL)


def kernel(x, even_w_in, even_conv_a_w, even_conv_a_b, even_norm_a_g, even_norm_a_b, even_pool_w, even_pool_scale, even_w_out, even_ln1_g, even_ln1_b, even_ffn_w_gate, even_ffn_w_up, even_ffn_w_down, even_ln2_g, even_ln2_b, odd_w_in, odd_conv_c_w, odd_sgu_norm_g, odd_sgu_norm_b, odd_sgu_w, odd_sgu_b, odd_w_out, odd_ln1_g, odd_ln1_b, odd_router, odd_moe_w_gate, odd_moe_w_up, odd_moe_w_down, odd_ln2_g, odd_ln2_b):
    for layer in range(DEPTH):
        i = layer // 2
        if layer % 2 == 0:
            x = _even_layer(x, even_w_in[i], even_conv_a_w[i], even_conv_a_b[i], even_norm_a_g[i],
                            even_norm_a_b[i], even_pool_w[i], even_pool_scale[i], even_w_out[i],
                            even_ln1_g[i], even_ln1_b[i], even_ffn_w_gate[i], even_ffn_w_up[i],
                            even_ffn_w_down[i], even_ln2_g[i], even_ln2_b[i])
        else:
            x = _odd_layer(x, odd_w_in[i], odd_conv_c_w[i], odd_sgu_norm_g[i], odd_sgu_norm_b[i],
                           odd_sgu_w[i], odd_sgu_b[i], odd_w_out[i], odd_ln1_g[i], odd_ln1_b[i],
                           odd_router[i], odd_moe_w_gate[i], odd_moe_w_up[i], odd_moe_w_down[i],
                           odd_ln2_g[i], odd_ln2_b[i])
    return x
```

```python
import functools

import jax
import jax.numpy as jnp
from jax import lax
from jax.experimental import pallas as pl
from jax.experimental.pallas import tpu as pltpu

F32 = jnp.float32
BF16 = jnp.bfloat16

D_MODEL = 1024
BATCH = 4
SEQ = 4096
TOKENS = BATCH * SEQ
DEPTH = 2

D_A = 512
CONV_A_WIDTH = 31
D_B = 512
POOL_WINDOWS = (2, 4, 8, 16)
POOL_GROUP_DIM = 128
IN_EVEN = 2 * D_A + D_B

D_C = 512
CONV_C_WIDTH = 3
D_D = 512
SGU_BLOCK = 128
SGU_HEADS = 4
SGU_HEAD_DIM = 128
IN_ODD = 3 * D_C + 2 * D_D

D_FF_DENSE = 2816
N_EXPERTS = 8
D_FF_EXPERT = 3584

ALPHA = (2 * DEPTH) ** 0.25
LN_EPS = 1e-5

SUBLANES = 8
LANES = 128
VMEM_LIMIT_BYTES = 56 * 1024 * 1024

TS_MIX = 512
EVEN_BANDS = 2
MIX_CHAINS = 2
ROW_SET_STRIDE = 4
CONV_SETS = 8
HALO_A = 32
HALO_B = 16
HALO_C = 8
TM_FFN = 1024
FF_CHUNK = 768
TM_MOE = 1024
SUB_MOE = 128
TF_MOE = 512
N_TOKEN_TILES = TOKENS // TS_MIX
N_SEGMENTS = N_TOKEN_TILES * N_EXPERTS
RUN_ALIGN = 16
SLOT_ROWS = 2 * TS_MIX + N_EXPERTS * RUN_ALIGN
MAX_SORTED_ROWS = 2 * TOKENS + N_SEGMENTS * (RUN_ALIGN - 1)
N_TILES_MOE = (MAX_SORTED_ROWS + N_EXPERTS * (TM_MOE - 1)) // TM_MOE
ROWS_SORTED = N_TILES_MOE * TM_MOE


def _layer_norm(x, g, b):
    mu = jnp.mean(x, axis=-1, keepdims=True)
    xc = x - mu
    var = jnp.mean(xc * xc, axis=-1, keepdims=True)
    return xc * lax.rsqrt(var + LN_EPS) * g + b


def _silu(x):
    return x * jax.nn.sigmoid(x)


def _dot(a, b):
    return jnp.dot(a, b, preferred_element_type=F32)


def _even_mixer_kernel(x_ref, win_ref, cw_ref, cb_ref, nag_ref, nab_ref, pw_ref, ps_ref,
                       wout_ref, g1_ref, b1_ref, o_ref, exta_ref, extb_ref, mixf_ref, mixin_ref):
    s = pl.program_id(1)
    ts = TS_MIX

    n_a, n_b = D_A // LANES, D_B // LANES

    @pl.when(s == 0)
    def _():
        exta_ref[:, 0:HALO_A, :] = jnp.zeros((n_a, HALO_A, LANES), F32)
        extb_ref[:, 0:HALO_B, :] = jnp.zeros((n_b, HALO_B, LANES), F32)

    def row_set(first):
        return pl.ds(first, SUBLANES, stride=ROW_SET_STRIDE)

    set_firsts = [blk * SUBLANES * ROW_SET_STRIDE + r
                  for blk in range(ts // (SUBLANES * ROW_SET_STRIDE)) for r in range(ROW_SET_STRIDE)]
    set_step = ROW_SET_STRIDE * lax.broadcasted_iota(jnp.int32, (SUBLANES, 1), 0)

    def pool_sets(firsts):
        for first in firsts:
            row_pos = s * ts + first + set_step
            for g, win in enumerate(POOL_WINDOWS):
                frame = extb_ref[g, row_set(HALO_B + first), :]
                wsum = frame
                for i in range(1, win):
                    wsum = wsum + extb_ref[g, row_set(HALO_B + first - i), :]
                if first >= win - 1:
                    mean = wsum * (1.0 / win)
                else:
                    mean = wsum / jnp.minimum(row_pos + 1, win).astype(F32)
                mixf_ref[n_a + g, row_set(first), :] = mean - frame

    x = x_ref[...]
    xb = x.astype(BF16)
    hb = _dot(xb, win_ref[:, 2 * D_A:])
    for cb in range(n_b):
        extb_ref[cb, HALO_B:HALO_B + ts, :] = hb[:, cb * LANES:(cb + 1) * LANES]
    parts = 2
    cols = D_A // parts
    sets_per_part = len(set_firsts) // parts
    for part in range(parts):
        pool_sets(set_firsts[part * sets_per_part:(part + 1) * sets_per_part])
        val = _dot(xb, win_ref[:, part * cols:(part + 1) * cols])
        gate = _dot(xb, win_ref[:, D_A + part * cols:D_A + (part + 1) * cols])
        glu = val * jax.nn.sigmoid(gate)
        for cb in range(cols // LANES):
            exta_ref[part * (cols // LANES) + cb, HALO_A:HALO_A + ts, :] = glu[:, cb * LANES:(cb + 1) * LANES]

    for g in range(n_b):
        lo, hi = g * POOL_GROUP_DIM, (g + 1) * POOL_GROUP_DIM
        mixed = _dot(mixf_ref[n_a + g].astype(BF16), pw_ref[g]) * ps_ref[:, lo:hi]
        mixin_ref[:, D_A + lo:D_A + hi] = mixed.astype(BF16)
    mix_pool = _dot(mixin_ref[:, D_A:], wout_ref[D_A:, :])

    band_rows = ts // EVEN_BANDS
    sets_per_band = len(set_firsts) // EVEN_BANDS
    norm_rows = 128
    for band in range(EVEN_BANDS):
        band_sets = set_firsts[band * sets_per_band:(band + 1) * sets_per_band]
        for cb in range(n_a):
            lanes = slice(cb * LANES, (cb + 1) * LANES)
            for group in range(0, len(band_sets), CONV_SETS):
                firsts = band_sets[group:group + CONV_SETS]
                accs = [jnp.broadcast_to(cb_ref[:, lanes], (SUBLANES, LANES))] * len(firsts)
                for k in range(CONV_A_WIDTH):
                    wk = jnp.broadcast_to(cw_ref[k:k + 1, lanes], (SUBLANES, LANES))
                    back = CONV_A_WIDTH - 1 - k
                    for n, first in enumerate(firsts):
                        accs[n] = accs[n] + wk * exta_ref[cb, row_set(HALO_A + first - back), :]
                for n, first in enumerate(firsts):
                    mixf_ref[cb, row_set(first), :] = accs[n]

        b0, b1 = band * band_rows, (band + 1) * band_rows
        for r0 in range(b0, b1, norm_rows):
            conv = jnp.concatenate([mixf_ref[cb, r0:r0 + norm_rows, :] for cb in range(n_a)], axis=1)
            y = _silu(_layer_norm(conv, nag_ref[...], nab_ref[...]))
            mixin_ref[r0:r0 + norm_rows, 0:D_A] = y.astype(BF16)
        mix = _dot(mixin_ref[b0:b1, 0:D_A], wout_ref[0:D_A, :]) + mix_pool[b0:b1, :]
        o_ref[b0:b1, :] = _layer_norm(ALPHA * x[b0:b1, :] + mix, g1_ref[...], b1_ref[...])

    exta_ref[:, 0:HALO_A, :] = exta_ref[:, ts:ts + HALO_A, :]
    extb_ref[:, 0:HALO_B, :] = extb_ref[:, ts:ts + HALO_B, :]


def _const_spec(shape):
    return pl.BlockSpec(shape, lambda *_: (0,) * len(shape))


def _even_mixer(x, w_in, conv_w, conv_b, norm_g, norm_b, pool_w, pool_scale, w_out, ln_g, ln_b):
    ts = TS_MIX
    x_spec = pl.BlockSpec((None, ts, D_MODEL), lambda b, s: (b, s, 0))
    return pl.pallas_call(
        _even_mixer_kernel,
        grid=(BATCH, SEQ // ts),
        in_specs=[
            x_spec,
            _const_spec((D_MODEL, IN_EVEN)),
            _const_spec((CONV_A_WIDTH, D_A)),
            _const_spec((1, D_A)),
            _const_spec((1, D_A)),
            _const_spec((1, D_A)),
            _const_spec((len(POOL_WINDOWS), POOL_GROUP_DIM, POOL_GROUP_DIM)),
            _const_spec((1, D_B)),
            _const_spec((D_A + D_B, D_MODEL)),
            _const_spec((1, D_MODEL)),
            _const_spec((1, D_MODEL)),
        ],
        out_specs=x_spec,
        out_shape=jax.ShapeDtypeStruct((BATCH, SEQ, D_MODEL), F32),
        scratch_shapes=[
            pltpu.VMEM((D_A // LANES, HALO_A + ts, LANES), F32),
            pltpu.VMEM((D_B // LANES, HALO_B + ts, LANES), F32),
            pltpu.VMEM(((D_A + D_B) // LANES, ts, LANES), F32),
            pltpu.VMEM((ts, D_A + D_B), BF16),
        ],
        compiler_params=pltpu.CompilerParams(
            dimension_semantics=("arbitrary", "arbitrary"), vmem_limit_bytes=VMEM_LIMIT_BYTES),
        name="even_mixer",
    )(x, w_in.astype(BF16), conv_w, conv_b[None], norm_g[None], norm_b[None],
      pool_w.astype(BF16), pool_scale[None], w_out.astype(BF16), ln_g[None], ln_b[None])


def _ff_chunks(total, chunk):
    bounds = list(range(0, total, chunk)) + [total]
    return list(zip(bounds[:-1], bounds[1:]))


def _dense_ffn_kernel(x_ref, wg_ref, wu_ref, wd_ref, g_ref, b_ref, o_ref, acc_ref):
    x = x_ref[...]
    xb = x.astype(BF16)
    for n, (lo, hi) in enumerate(_ff_chunks(D_FF_DENSE, FF_CHUNK)):
        a = _silu(_dot(xb, wg_ref[:, lo:hi])) * _dot(xb, wu_ref[:, lo:hi])
        part = _dot(a.astype(BF16), wd_ref[lo:hi, :])
        if n == 0:
            acc_ref[...] = part
        else:
            acc_ref[...] += part
    o_ref[...] = _layer_norm(ALPHA * x + acc_ref[...], g_ref[...], b_ref[...])


def _dense_ffn(x, w_gate, w_up, w_down, ln_g, ln_b):
    tm = TM_FFN
    row_spec = pl.BlockSpec((tm, D_MODEL), lambda i: (i, 0))
    resident = pl.BlockSpec(memory_space=pltpu.VMEM)
    return pl.pallas_call(
        _dense_ffn_kernel,
        grid=(TOKENS // tm,),
        in_specs=[row_spec, resident, resident, resident,
                  _const_spec((1, D_MODEL)), _const_spec((1, D_MODEL))],
        out_specs=row_spec,
        out_shape=jax.ShapeDtypeStruct((TOKENS, D_MODEL), F32),
        scratch_shapes=[pltpu.VMEM((tm, D_MODEL), F32)],
        compiler_params=pltpu.CompilerParams(
            dimension_semantics=("arbitrary",), vmem_limit_bytes=VMEM_LIMIT_BYTES),
        name="dense_ffn",
    )(x, w_gate.astype(BF16), w_up.astype(BF16), w_down.astype(BF16), ln_g[None], ln_b[None])


META_E1, META_E2, META_W1, META_W2, META_P1, META_P2 = range(6)
ROUTER_ROWS = 16


def _odd_mixer_kernel(x_ref, win_ref, ccw_ref, sng_ref, snb_ref, sw_ref, sb_ref, wout_ref,
                      g1_ref, b1_ref, rtr_ref, o_ref, meta_ref, cnt_ref,
                      extc_ref, mixin_ref):
    s = pl.program_id(1)
    ts = TS_MIX

    @pl.when(s == 0)
    def _():
        extc_ref[0:HALO_C, :] = jnp.zeros((HALO_C, D_C), F32)

    tri = (lax.broadcasted_iota(jnp.int32, (SGU_BLOCK, SGU_BLOCK), 0)
           >= lax.broadcasted_iota(jnp.int32, (SGU_BLOCK, SGU_BLOCK), 1))
    sgu_w = [jnp.where(tri, sw_ref[hd], 0.0).astype(BF16) for hd in range(SGU_HEADS)]

    chain_rows = ts // MIX_CHAINS
    blocks = chain_rows // SGU_BLOCK
    x_parts = [x_ref[c * chain_rows:(c + 1) * chain_rows, :] for c in range(MIX_CHAINS)]
    xb_parts = [x.astype(BF16) for x in x_parts]

    def project(c, lo, hi):
        return _dot(xb_parts[c], win_ref[:, lo:hi])

    def short_conv(c, conv_in):
        c0, c1 = c * chain_rows, (c + 1) * chain_rows
        extc_ref[HALO_C + c0:HALO_C + c1, :] = conv_in[:, D_C:2 * D_C] * conv_in[:, 2 * D_C:3 * D_C]
        conv = ccw_ref[0:1, :] * extc_ref[HALO_C - 2 + c0:HALO_C - 2 + c1, :]
        conv = conv + ccw_ref[1:2, :] * extc_ref[HALO_C - 1 + c0:HALO_C - 1 + c1, :]
        conv = conv + ccw_ref[2:3, :] * extc_ref[HALO_C + c0:HALO_C + c1, :]
        mixin_ref[c0:c1, 0:D_C] = (conv_in[:, 0:D_C] * conv).astype(BF16)

    def spatial_gate(c, sgu_in):
        c0 = c * chain_rows
        z = jax.nn.gelu(sgu_in)
        u = z[:, :D_D]
        v = _layer_norm(z[:, D_D:], sng_ref[...], snb_ref[...]).astype(BF16)
        for hd in range(SGU_HEADS):
            lo, hi = hd * SGU_HEAD_DIM, (hd + 1) * SGU_HEAD_DIM
            v_blocks = jnp.concatenate(
                [v[blk * SGU_BLOCK:(blk + 1) * SGU_BLOCK, lo:hi] for blk in range(blocks)], axis=1)
            mixed = _dot(sgu_w[hd], v_blocks) + sb_ref[:, hd:hd + 1]
            for blk in range(blocks):
                r0, r1 = blk * SGU_BLOCK, (blk + 1) * SGU_BLOCK
                gated = u[r0:r1, lo:hi] * mixed[:, blk * SGU_HEAD_DIM:(blk + 1) * SGU_HEAD_DIM]
                mixin_ref[c0 + r0:c0 + r1, D_C + lo:D_C + hi] = gated.astype(BF16)

    def finish(c):
        c0, c1 = c * chain_rows, (c + 1) * chain_rows
        mix = _dot(mixin_ref[c0:c1, :], wout_ref[...])
        x1_c = _layer_norm(ALPHA * x_parts[c] + mix, g1_ref[...], b1_ref[...])
        o_ref[c0:c1, :] = x1_c
        return x1_c

    conv_cols, sgu_cols = (0, 3 * D_C), (3 * D_C, IN_ODD)
    conv_in = project(0, *conv_cols)
    sgu_in = project(0, *sgu_cols)
    x1_parts = []
    for c in range(MIX_CHAINS):
        following = c + 1 < MIX_CHAINS
        next_conv_in = project(c + 1, *conv_cols) if following else None
        short_conv(c, conv_in)
        next_sgu_in = project(c + 1, *sgu_cols) if following else None
        spatial_gate(c, sgu_in)
        x1_parts.append(finish(c))
        conv_in, sgu_in = next_conv_in, next_sgu_in
    x1 = jnp.concatenate(x1_parts, axis=0)

    extc_ref[0:HALO_C, :] = extc_ref[ts:ts + HALO_C, :]

    x_hi = x1.astype(BF16)
    x_lo = (x1 - x_hi.astype(F32)).astype(BF16)
    nt = (((1,), (1,)), ((), ()))
    by_hi = lax.dot_general(rtr_ref[...], x_hi, nt, preferred_element_type=F32)
    by_lo = lax.dot_general(rtr_ref[0:ROUTER_ROWS, :], x_lo, nt, preferred_element_type=F32)
    logits = (by_hi[0:N_EXPERTS, :] + by_hi[ROUTER_ROWS:ROUTER_ROWS + N_EXPERTS, :]
              + by_lo[0:N_EXPERTS, :])

    expert = lax.broadcasted_iota(jnp.int32, (N_EXPERTS, ts), 0).astype(F32)
    neg_inf = jnp.float32(-jnp.inf)
    m1 = jnp.max(logits, axis=0, keepdims=True)
    e1 = jnp.min(jnp.where(logits == m1, expert, float(N_EXPERTS)), axis=0, keepdims=True)
    rest = jnp.where(expert == e1, neg_inf, logits)
    m2 = jnp.max(rest, axis=0, keepdims=True)
    e2 = jnp.min(jnp.where(rest == m2, expert, float(N_EXPERTS)), axis=0, keepdims=True)
    ex = jnp.exp(m2 - m1)
    den = 1.0 + ex
    w1 = 1.0 / den
    w2 = ex / den

    onehot = jnp.where(jnp.logical_or(expert == e1, expert == e2), 1.0, 0.0)
    onehot_b = jnp.concatenate([onehot, jnp.zeros_like(onehot)], axis=0).astype(BF16)
    before = (lax.broadcasted_iota(jnp.int32, (ts, ts), 0)
              < lax.broadcasted_iota(jnp.int32, (ts, ts), 1))
    earlier = _dot(onehot_b, jnp.where(before, 1.0, 0.0).astype(BF16))[0:N_EXPERTS, :]
    run = jnp.sum(onehot, axis=1, keepdims=True)
    run = jnp.floor((run + (RUN_ALIGN - 1)) * (1.0 / RUN_ALIGN)) * RUN_ALIGN
    expert_col = expert[:, 0:1]
    lower_rows = jnp.zeros((N_EXPERTS, 1), F32)
    for e in range(N_EXPERTS - 1):
        lower_rows = lower_rows + jnp.where(expert_col > e, run[e:e + 1, :], 0.0)
    place = earlier + lower_rows
    p1 = jnp.sum(jnp.where(expert == e1, place, 0.0), axis=0, keepdims=True)
    p2 = jnp.sum(jnp.where(expert == e2, place, 0.0), axis=0, keepdims=True)

    records = {META_E1: e1, META_E2: e2, META_W1: w1, META_W2: w2, META_P1: p1, META_P2: p2}
    meta_ref[...] = jnp.concatenate(
        [records.get(row, jnp.zeros((1, ts), F32)) for row in range(SUBLANES)], axis=0)
    cnt_ref[...] = jnp.broadcast_to(run, (N_EXPERTS, LANES))


def _odd_mixer(x, w_in, conv_w, sgu_norm_g, sgu_norm_b, sgu_w, sgu_b, w_out, ln_g, ln_b, router):
    ts = TS_MIX
    x_spec = pl.BlockSpec((None, ts, D_MODEL), lambda b, s: (b, s, 0))
    router_pad = jnp.zeros((ROUTER_ROWS, D_MODEL), F32).at[:N_EXPERTS, :].set(router.T)
    router_hi = router_pad.astype(BF16)
    router_lo = (router_pad - router_hi.astype(F32)).astype(BF16)
    router_parts = jnp.concatenate([router_hi, router_lo], axis=0)
    return pl.pallas_call(
        _odd_mixer_kernel,
        grid=(BATCH, SEQ // ts),
        in_specs=[
            x_spec,
            _const_spec((D_MODEL, IN_ODD)),
            _const_spec((CONV_C_WIDTH, D_C)),
            _const_spec((1, D_D)),
            _const_spec((1, D_D)),
            _const_spec((SGU_HEADS, SGU_BLOCK, SGU_BLOCK)),
            _const_spec((SGU_BLOCK, SGU_HEADS)),
            _const_spec((D_C + D_D, D_MODEL)),
            _const_spec((1, D_MODEL)),
            _const_spec((1, D_MODEL)),
            _const_spec((2 * ROUTER_ROWS, D_MODEL)),
        ],
        out_specs=[
            x_spec,
            pl.BlockSpec((SUBLANES, ts), lambda b, s: (0, b * (SEQ // ts) + s)),
            pl.BlockSpec((None, N_EXPERTS, LANES), lambda b, s: (b * (SEQ // ts) + s, 0, 0)),
        ],
        out_shape=[
            jax.ShapeDtypeStruct((BATCH, SEQ, D_MODEL), F32),
            jax.ShapeDtypeStruct((SUBLANES, TOKENS), F32),
            jax.ShapeDtypeStruct((N_TOKEN_TILES, N_EXPERTS, LANES), F32),
        ],
        scratch_shapes=[
            pltpu.VMEM((HALO_C + ts, D_C), F32),
            pltpu.VMEM((ts, D_C + D_D), BF16),
        ],
        compiler_params=pltpu.CompilerParams(
            dimension_semantics=("arbitrary", "arbitrary"), vmem_limit_bytes=VMEM_LIMIT_BYTES),
        name="odd_mixer",
    )(x, w_in.astype(BF16), conv_w, sgu_norm_g[None], sgu_norm_b[None], sgu_w, sgu_b.T,
      w_out.astype(BF16), ln_g[None], ln_b[None], router_parts)


SEGMENT_BITS = range(RUN_ALIGN.bit_length() - 1, TS_MIX.bit_length())


def _for_each_segment_chunk(seg_ref, tile, visit):
    for e in range(N_EXPERTS):
        k = tile * N_EXPERTS + e
        n = seg_ref[k]
        local = seg_ref[N_SEGMENTS + k]
        sorted_row = seg_ref[2 * N_SEGMENTS + k]
        for bit in reversed(SEGMENT_BITS):
            @pl.when(((n >> bit) & 1) == 1)
            def _():
                done = (n >> (bit + 1)) << (bit + 1)
                visit(pl.multiple_of(local + done, RUN_ALIGN), pl.multiple_of(sorted_row + done, RUN_ALIGN),
                      1 << bit)


def _dispatch_kernel(seg_ref, pad_tile_ref, x_ref, meta_ref, xs_ref, zero_ref, perm_ref, zero_sem, perm_sem):
    i = pl.program_id(0)
    slot = i % 2

    def chunk_copy(which):
        def make(local, sorted_row, size):
            return pltpu.make_async_copy(perm_ref.at[which, pl.ds(local, size)],
                                         xs_ref.at[pl.ds(sorted_row, size)], perm_sem.at[which])
        return make

    def wait_tile(tile, which):
        _for_each_segment_chunk(seg_ref, tile, lambda *chunk: chunk_copy(which)(*chunk).wait())

    @pl.when(i == 0)
    def _():
        zero_ref[...] = jnp.zeros(zero_ref.shape, BF16)

        def clear_tile(t):
            start = pl.multiple_of(t * TM_MOE, TM_MOE)
            cp = pltpu.make_async_copy(zero_ref, xs_ref.at[pl.ds(start, TM_MOE)], zero_sem)
            cp.start()
            cp.wait()

        for e in range(N_EXPERTS):
            @pl.when(pad_tile_ref[e] >= 0)
            def _():
                clear_tile(pad_tile_ref[e])

        def clear_tail(t, carry):
            clear_tile(t)
            return carry

        lax.fori_loop(pad_tile_ref[N_EXPERTS], N_TILES_MOE, clear_tail, 0)

    meta = meta_ref[...]
    row = lax.broadcasted_iota(jnp.int32, (SLOT_ROWS, TS_MIX), 0).astype(F32)
    hit = jnp.logical_or(row == meta[META_P1:META_P1 + 1, :], row == meta[META_P2:META_P2 + 1, :])
    onehot = jnp.where(hit, 1.0, 0.0).astype(BF16)
    permuted = _dot(onehot, x_ref[...].astype(BF16))

    @pl.when(i >= 2)
    def _():
        wait_tile(i - 2, slot)

    perm_ref[slot] = permuted.astype(BF16)
    _for_each_segment_chunk(seg_ref, i, lambda *chunk: chunk_copy(slot)(*chunk).start())

    @pl.when(i == N_TOKEN_TILES - 1)
    def _():
        wait_tile(i - 1, 1 - slot)
        wait_tile(i, slot)


def _dispatch(x, meta, seg, pad_tile):
    return pl.pallas_call(
        _dispatch_kernel,
        grid_spec=pltpu.PrefetchScalarGridSpec(
            num_scalar_prefetch=2,
            grid=(N_TOKEN_TILES,),
            in_specs=[pl.BlockSpec((TS_MIX, D_MODEL), lambda i, *_: (i, 0)),
                      pl.BlockSpec((SUBLANES, TS_MIX), lambda i, *_: (0, i))],
            out_specs=pl.BlockSpec(memory_space=pl.ANY),
            scratch_shapes=[
                pltpu.VMEM((TM_MOE, D_MODEL), BF16),
                pltpu.VMEM((2, SLOT_ROWS, D_MODEL), BF16),
                pltpu.SemaphoreType.DMA(()),
                pltpu.SemaphoreType.DMA((2,)),
            ],
        ),
        out_shape=jax.ShapeDtypeStruct((ROWS_SORTED, D_MODEL), BF16),
        compiler_params=pltpu.CompilerParams(
            dimension_semantics=("arbitrary",), vmem_limit_bytes=VMEM_LIMIT_BYTES),
        name="moe_dispatch",
    )(seg, pad_tile, x, meta)


def _grouped_ffn_kernel(tile_expert_ref, tile_rows_ref, n_active_ref, xs_ref, wg_ref, wu_ref, wd_ref,
                        ys_ref, acc_ref):
    i = pl.program_id(0)
    j = pl.program_id(1)

    active = i < n_active_ref[0]
    rows = tile_rows_ref[i]

    uncovered = jnp.logical_or(jnp.logical_not(active), rows <= TM_MOE - SUB_MOE)

    @pl.when(jnp.logical_and(j == 0, jnp.logical_or(i == 0, uncovered)))
    def _():
        acc_ref[...] = jnp.zeros((TM_MOE, D_MODEL), F32)
        ys_ref[...] = jnp.zeros((TM_MOE, D_MODEL), BF16)

    def accumulate(n_rows):
        xb = xs_ref[0:n_rows, :]
        a = _silu(_dot(xb, wg_ref[...].astype(BF16))) * _dot(xb, wu_ref[...].astype(BF16))
        part = _dot(a.astype(BF16), wd_ref[...].astype(BF16))
        total = jnp.where(j == 0, part, acc_ref[0:n_rows, :] + part)
        acc_ref[0:n_rows, :] = total
        ys_ref[0:n_rows, :] = total.astype(BF16)

    for n_sub in range(1, TM_MOE // SUB_MOE + 1):
        lo, hi = (n_sub - 1) * SUB_MOE, n_sub * SUB_MOE

        @pl.when(jnp.logical_and(active, jnp.logical_and(rows > lo, rows <= hi)))
        def _():
            accumulate(hi)


def _grouped_ffn(xs, w_gate, w_up, w_down, tile_expert, tile_rows, n_active):
    nj = D_FF_EXPERT // TF_MOE

    def row_map(i, j, te, tr, na):
        return (jnp.minimum(i, na[0] - 1), 0)

    def frozen_j(i, j, na):
        return jnp.where(i < na[0], j, nj - 1)

    def up_map(i, j, te, tr, na):
        return (te[i], 0, frozen_j(i, j, na))

    def down_map(i, j, te, tr, na):
        return (te[i], frozen_j(i, j, na), 0)

    return pl.pallas_call(
        _grouped_ffn_kernel,
        grid_spec=pltpu.PrefetchScalarGridSpec(
            num_scalar_prefetch=3,
            grid=(N_TILES_MOE, nj),
            in_specs=[
                pl.BlockSpec((TM_MOE, D_MODEL), row_map),
                pl.BlockSpec((None, D_MODEL, TF_MOE), up_map),
                pl.BlockSpec((None, D_MODEL, TF_MOE), up_map),
                pl.BlockSpec((None, TF_MOE, D_MODEL), down_map),
            ],
            out_specs=pl.BlockSpec((TM_MOE, D_MODEL), lambda i, j, *_: (i, 0)),
            scratch_shapes=[pltpu.VMEM((TM_MOE, D_MODEL), F32)],
        ),
        out_shape=jax.ShapeDtypeStruct((ROWS_SORTED, D_MODEL), BF16),
        compiler_params=pltpu.CompilerParams(
            dimension_semantics=("arbitrary", "arbitrary"), vmem_limit_bytes=VMEM_LIMIT_BYTES),
        name="moe_grouped_ffn",
    )(tile_expert, tile_rows, n_active, xs, w_gate, w_up, w_down)


def _combine_kernel(seg_ref, x_ref, meta_ref, ys_ref, g_ref, b_ref, o_ref, win_ref, sem):
    i = pl.program_id(0)
    slot = i % 2

    def chunk_copy(which):
        def make(local, sorted_row, size):
            return pltpu.make_async_copy(ys_ref.at[pl.ds(sorted_row, size)],
                                         win_ref.at[which, pl.ds(local, size)], sem.at[which])
        return make

    def fetch(tile, which):
        _for_each_segment_chunk(seg_ref, tile, lambda *chunk: chunk_copy(which)(*chunk).start())

    @pl.when(i == 0)
    def _():
        win_ref[...] = jnp.zeros(win_ref.shape, BF16)
        fetch(0, 0)

    @pl.when(i + 1 < N_TOKEN_TILES)
    def _():
        fetch(i + 1, 1 - slot)

    _for_each_segment_chunk(seg_ref, i, lambda *chunk: chunk_copy(slot)(*chunk).wait())

    meta = meta_ref[...]
    row = lax.broadcasted_iota(jnp.int32, (SLOT_ROWS, TS_MIX), 0).astype(F32)
    gate = (jnp.where(row == meta[META_P1:META_P1 + 1, :], meta[META_W1:META_W1 + 1, :], 0.0)
            + jnp.where(row == meta[META_P2:META_P2 + 1, :], meta[META_W2:META_W2 + 1, :], 0.0))
    moe = lax.dot_general(gate.astype(BF16), win_ref[slot], (((0,), (0,)), ((), ())),
                          preferred_element_type=F32)
    o_ref[...] = _layer_norm(ALPHA * x_ref[...] + moe, g_ref[...], b_ref[...])


def _combine(x, meta, ys, seg, ln_g, ln_b):
    row_spec = pl.BlockSpec((TS_MIX, D_MODEL), lambda i, *_: (i, 0))
    return pl.pallas_call(
        _combine_kernel,
        grid_spec=pltpu.PrefetchScalarGridSpec(
            num_scalar_prefetch=1,
            grid=(N_TOKEN_TILES,),
            in_specs=[
                row_spec,
                pl.BlockSpec((SUBLANES, TS_MIX), lambda i, *_: (0, i)),
                pl.BlockSpec(memory_space=pl.ANY),
                pl.BlockSpec((1, D_MODEL), lambda i, *_: (0, 0)),
                pl.BlockSpec((1, D_MODEL), lambda i, *_: (0, 0)),
            ],
            out_specs=row_spec,
            scratch_shapes=[
                pltpu.VMEM((2, SLOT_ROWS, D_MODEL), BF16),
                pltpu.SemaphoreType.DMA((2,)),
            ],
        ),
        out_shape=jax.ShapeDtypeStruct((TOKENS, D_MODEL), F32),
        compiler_params=pltpu.CompilerParams(
            dimension_semantics=("arbitrary",), vmem_limit_bytes=VMEM_LIMIT_BYTES),
        name="moe_combine",
    )(seg, x, meta, ys, ln_g[None], ln_b[None])


def _routing_tables(tile_counts):
    n = tile_counts[:, :, 0].astype(jnp.int32)
    counts = jnp.sum(n, axis=0)
    tiles = (counts + TM_MOE - 1) // TM_MOE
    tile_end = jnp.cumsum(tiles)
    tile_start = tile_end - tiles
    offset = tile_start * TM_MOE
    local_start = jnp.cumsum(n, axis=1) - n
    sorted_start = offset[None, :] + jnp.cumsum(n, axis=0) - n
    seg = jnp.concatenate([n.reshape(-1), local_start.reshape(-1), sorted_start.reshape(-1)])
    n_active = tile_end[-1:]
    tile_id = jnp.minimum(jnp.arange(N_TILES_MOE, dtype=jnp.int32), n_active[0] - 1)
    tile_expert = jnp.sum(tile_id[:, None] >= tile_end[None, :], axis=1).astype(jnp.int32)
    tile_rows = jnp.clip(counts[tile_expert] - (tile_id - tile_start[tile_expert]) * TM_MOE, 0, TM_MOE)
    pad_tile = jnp.concatenate([jnp.where(tiles > 0, tile_end - 1, -1), n_active]).astype(jnp.int32)
    return (seg.astype(jnp.int32), pad_tile, tile_expert, tile_rows.astype(jnp.int32),
            n_active.astype(jnp.int32))


def _even_layer(x, w_in, conv_a_w, conv_a_b, norm_a_g, norm_a_b, pool_w, pool_scale, w_out,
                ln1_g, ln1_b, ffn_w_gate, ffn_w_up, ffn_w_down, ln2_g, ln2_b):
    x = _even_mixer(x, w_in, conv_a_w, conv_a_b, norm_a_g, norm_a_b, pool_w, pool_scale, w_out,
                    ln1_g, ln1_b)
    x = _dense_ffn(x.reshape(TOKENS, D_MODEL), ffn_w_gate, ffn_w_up, ffn_w_down, ln2_g, ln2_b)
    return x.reshape(BATCH, SEQ, D_MODEL)


def _odd_layer(x, w_in, conv_c_w, sgu_norm_g, sgu_norm_b, sgu_w, sgu_b, w_out, ln1_g, ln1_b,
               router, moe_w_gate, moe_w_up, moe_w_down, ln2_g, ln2_b):
    x, meta, tile_counts = _odd_mixer(x, w_in, conv_c_w, sgu_norm_g, sgu_norm_b, sgu_w, sgu_b, w_out,
                                      ln1_g, ln1_b, router)
    x = x.reshape(TOKENS, D_MODEL)
    seg, pad_tile, tile_expert, tile_rows, n_active = _routing_tables(tile_counts)
    xs = _dispatch(x, meta, seg, pad_tile)
    ys = _grouped_ffn(xs, moe_w_gate, moe_w_up, moe_w_down, tile_expert, tile_rows, n_active)
    x = _combine(x, meta, ys, seg, ln2_g, ln2_b)
    return x.reshape(BATCH, SEQ, D_MODEL)


def kernel(x, even_w_in, even_conv_a_w, even_conv_a_b, even_norm_a_g, even_norm_a_b, even_pool_w, even_pool_scale, even_w_out, even_ln1_g, even_ln1_b, even_ffn_w_gate, even_ffn_w_up, even_ffn_w_down, even_ln2_g, even_ln2_b, odd_w_in, odd_conv_c_w, odd_sgu_norm_g, odd_sgu_norm_b, odd_sgu_w, odd_sgu_b, odd_w_out, odd_ln1_g, odd_ln1_b, odd_router, odd_moe_w_gate, odd_moe_w_up, odd_moe_w_down, odd_ln2_g, odd_ln2_b):
    for layer in range(DEPTH):
        i = layer // 2
        if layer % 2 == 0:
            x = _even_layer(x, even_w_in[i], even_conv_a_w[i], even_conv_a_b[i], even_norm_a_g[i],
                            even_norm_a_b[i], even_pool_w[i], even_pool_scale[i], even_w_out[i],
                            even_ln1_g[i], even_ln1_b[i], even_ffn_w_gate[i], even_ffn_w_up[i],
                            even_ffn_w_down[i], even_ln2_g[i], even_ln2_b[i])
        else:
            x = _odd_layer(x, odd_w_in[i], odd_conv_c_w[i], odd_sgu_norm_g[i], odd_sgu_norm_b[i],
                           odd_sgu_w[i], odd_sgu_b[i], odd_w_out[i], odd_ln1_g[i], odd_ln1_b[i],
                           odd_router[i], odd_moe_w_gate[i], odd_moe_w_up[i], odd_moe_w_down[i],
                           odd_ln2_g[i], odd_ln2_b[i])
    return x
```

```python
import jax
import jax.numpy as jnp
from jax import lax
from jax.experimental import pallas as pl
from jax.experimental.pallas import tpu as pltpu

F32 = jnp.float32
BF16 = jnp.bfloat16

D_MODEL = 1024
BATCH = 4
SEQ = 4096
TOKENS = BATCH * SEQ
DEPTH = 2

D_A = 512
CONV_A_WIDTH = 31
D_B = 512
POOL_WINDOWS = (2, 4, 8, 16)
POOL_GROUP_DIM = 128
IN_EVEN = 2 * D_A + D_B

D_C = 512
CONV_C_WIDTH = 3
D_D = 512
SGU_BLOCK = 128
SGU_HEADS = 4
SGU_HEAD_DIM = 128
IN_ODD = 3 * D_C + 2 * D_D

D_FF_DENSE = 2816
N_EXPERTS = 8
D_FF_EXPERT = 3584

ALPHA = (2 * DEPTH) ** 0.25
LN_EPS = 1e-5

SUBLANES = 8
LANES = 128
VMEM_LIMIT_BYTES = 56 * 1024 * 1024

TS_MIX = 512
EVEN_BANDS = 2
MIX_CHAINS = 2
ROW_SET_STRIDE = 4
CONV_SETS = 8
HALO_A = 32
HALO_B = 16
HALO_C = 8
TM_FFN = 1024
FF_CHUNK = 768
TM_MOE = 1024
SUB_MOE = 128
TF_MOE = 512
N_TOKEN_TILES = TOKENS // TS_MIX
N_SEGMENTS = N_TOKEN_TILES * N_EXPERTS
RUN_ALIGN = 16
SLOT_ROWS = 2 * TS_MIX + N_EXPERTS * RUN_ALIGN
MAX_SORTED_ROWS = 2 * TOKENS + N_SEGMENTS * (RUN_ALIGN - 1)
N_TILES_MOE = (MAX_SORTED_ROWS + N_EXPERTS * (TM_MOE - 1)) // TM_MOE
ROWS_SORTED = N_TILES_MOE * TM_MOE


def _layer_norm(x, g, b):
    mu = jnp.mean(x, axis=-1, keepdims=True)
    xc = x - mu
    var = jnp.mean(xc * xc, axis=-1, keepdims=True)
    return xc * lax.rsqrt(var + LN_EPS) * g + b


def _silu(x):
    return x * jax.nn.sigmoid(x)


def _dot(a, b):
    return jnp.dot(a, b, preferred_element_type=F32)


def _even_mixer_kernel(x_ref, win_ref, cw_ref, cb_ref, nag_ref, nab_ref, pw_ref, ps_ref,
                       wout_ref, g1_ref, b1_ref, o_ref, exta_ref, extb_ref, mixf_ref, mixin_ref):
    s = pl.program_id(1)
    ts = TS_MIX

    n_a, n_b = D_A // LANES, D_B // LANES

    @pl.when(s == 0)
    def _():
        exta_ref[:, 0:HALO_A, :] = jnp.zeros((n_a, HALO_A, LANES), F32)
        extb_ref[:, 0:HALO_B, :] = jnp.zeros((n_b, HALO_B, LANES), F32)

    def row_set(first):
        return pl.ds(first, SUBLANES, stride=ROW_SET_STRIDE)

    set_firsts = [blk * SUBLANES * ROW_SET_STRIDE + r
                  for blk in range(ts // (SUBLANES * ROW_SET_STRIDE)) for r in range(ROW_SET_STRIDE)]
    set_step = ROW_SET_STRIDE * lax.broadcasted_iota(jnp.int32, (SUBLANES, 1), 0)

    def pool_sets(firsts):
        for first in firsts:
            row_pos = s * ts + first + set_step
            for g, win in enumerate(POOL_WINDOWS):
                frame = extb_ref[g, row_set(HALO_B + first), :]
                wsum = frame
                for i in range(1, win):
                    wsum = wsum + extb_ref[g, row_set(HALO_B + first - i), :]
                if first >= win - 1:
                    mean = wsum * (1.0 / win)
                else:
                    mean = wsum / jnp.minimum(row_pos + 1, win).astype(F32)
                mixf_ref[n_a + g, row_set(first), :] = mean - frame

    x = x_ref[...]
    xb = x.astype(BF16)
    hb = _dot(xb, win_ref[:, 2 * D_A:])
    for cb in range(n_b):
        extb_ref[cb, HALO_B:HALO_B + ts, :] = hb[:, cb * LANES:(cb + 1) * LANES]
    parts = 2
    cols = D_A // parts
    sets_per_part = len(set_firsts) // parts
    for part in range(parts):
        pool_sets(set_firsts[part * sets_per_part:(part + 1) * sets_per_part])
        val = _dot(xb, win_ref[:, part * cols:(part + 1) * cols])
        gate = _dot(xb, win_ref[:, D_A + part * cols:D_A + (part + 1) * cols])
        glu = val * jax.nn.sigmoid(gate)
        for cb in range(cols // LANES):
            exta_ref[part * (cols // LANES) + cb, HALO_A:HALO_A + ts, :] = glu[:, cb * LANES:(cb + 1) * LANES]

    for g in range(n_b):
        lo, hi = g * POOL_GROUP_DIM, (g + 1) * POOL_GROUP_DIM
        mixed = _dot(mixf_ref[n_a + g].astype(BF16), pw_ref[g]) * ps_ref[:, lo:hi]
        mixin_ref[:, D_A + lo:D_A + hi] = mixed.astype(BF16)
    mix_pool = _dot(mixin_ref[:, D_A:], wout_ref[D_A:, :])

    band_rows = ts // EVEN_BANDS
    sets_per_band = len(set_firsts) // EVEN_BANDS
    norm_rows = 128
    for band in range(EVEN_BANDS):
        band_sets = set_firsts[band * sets_per_band:(band + 1) * sets_per_band]
        for cb in range(n_a):
            lanes = slice(cb * LANES, (cb + 1) * LANES)
            for group in range(0, len(band_sets), CONV_SETS):
                firsts = band_sets[group:group + CONV_SETS]
                accs = [jnp.broadcast_to(cb_ref[:, lanes], (SUBLANES, LANES))] * len(firsts)
                for k in range(CONV_A_WIDTH):
                    wk = jnp.broadcast_to(cw_ref[k:k + 1, lanes], (SUBLANES, LANES))
                    back = CONV_A_WIDTH - 1 - k
                    for n, first in enumerate(firsts):
                        accs[n] = accs[n] + wk * exta_ref[cb, row_set(HALO_A + first - back), :]
                for n, first in enumerate(firsts):
                    mixf_ref[cb, row_set(first), :] = accs[n]

        b0, b1 = band * band_rows, (band + 1) * band_rows
        for r0 in range(b0, b1, norm_rows):
            conv = jnp.concatenate([mixf_ref[cb, r0:r0 + norm_rows, :] for cb in range(n_a)], axis=1)
            y = _silu(_layer_norm(conv, nag_ref[...], nab_ref[...]))
            mixin_ref[r0:r0 + norm_rows, 0:D_A] = y.astype(BF16)
        mix = _dot(mixin_ref[b0:b1, 0:D_A], wout_ref[0:D_A, :]) + mix_pool[b0:b1, :]
        o_ref[b0:b1, :] = _layer_norm(ALPHA * x[b0:b1, :] + mix, g1_ref[...], b1_ref[...])

    exta_ref[:, 0:HALO_A, :] = exta_ref[:, ts:ts + HALO_A, :]
    extb_ref[:, 0:HALO_B, :] = extb_ref[:, ts:ts + HALO_B, :]


def _const_spec(shape):
    return pl.BlockSpec(shape, lambda *_: (0,) * len(shape))


def _even_mixer(x, w_in, conv_w, conv_b, norm_g, norm_b, pool_w, pool_scale, w_out, ln_g, ln_b):
    ts = TS_MIX
    x_spec = pl.BlockSpec((None, ts, D_MODEL), lambda b, s: (b, s, 0))
    return pl.pallas_call(
        _even_mixer_kernel,
        grid=(BATCH, SEQ // ts),
        in_specs=[
            x_spec,
            _const_spec((D_MODEL, IN_EVEN)),
            _const_spec((CONV_A_WIDTH, D_A)),
            _const_spec((1, D_A)),
            _const_spec((1, D_A)),
            _const_spec((1, D_A)),
            _const_spec((len(POOL_WINDOWS), POOL_GROUP_DIM, POOL_GROUP_DIM)),
            _const_spec((1, D_B)),
            _const_spec((D_A + D_B, D_MODEL)),
            _const_spec((1, D_MODEL)),
            _const_spec((1, D_MODEL)),
        ],
        out_specs=x_spec,
        out_shape=jax.ShapeDtypeStruct((BATCH, SEQ, D_MODEL), F32),
        scratch_shapes=[
            pltpu.VMEM((D_A // LANES, HALO_A + ts, LANES), F32),
            pltpu.VMEM((D_B // LANES, HALO_B + ts, LANES), F32),
            pltpu.VMEM(((D_A + D_B) // LANES, ts, LANES), F32),
            pltpu.VMEM((ts, D_A + D_B), BF16),
        ],
        compiler_params=pltpu.CompilerParams(
            dimension_semantics=("arbitrary", "arbitrary"), vmem_limit_bytes=VMEM_LIMIT_BYTES),
        name="even_mixer",
    )(x, w_in.astype(BF16), conv_w, conv_b[None], norm_g[None], norm_b[None],
      pool_w.astype(BF16), pool_scale[None], w_out.astype(BF16), ln_g[None], ln_b[None])


def _ff_chunks(total, chunk):
    bounds = list(range(0, total, chunk)) + [total]
    return list(zip(bounds[:-1], bounds[1:]))


def _dense_ffn_kernel(x_ref, wg_ref, wu_ref, wd_ref, g_ref, b_ref, o_ref, acc_ref):
    x = x_ref[...]
    xb = x.astype(BF16)
    for n, (lo, hi) in enumerate(_ff_chunks(D_FF_DENSE, FF_CHUNK)):
        a = _silu(_dot(xb, wg_ref[:, lo:hi])) * _dot(xb, wu_ref[:, lo:hi])
        part = _dot(a.astype(BF16), wd_ref[lo:hi, :])
        if n == 0:
            acc_ref[...] = part
        else:
            acc_ref[...] += part
    o_ref[...] = _layer_norm(ALPHA * x + acc_ref[...], g_ref[...], b_ref[...])


def _dense_ffn(x, w_gate, w_up, w_down, ln_g, ln_b):
    tm = TM_FFN
    row_spec = pl.BlockSpec((tm, D_MODEL), lambda i: (i, 0))
    resident = pl.BlockSpec(memory_space=pltpu.VMEM)
    return pl.pallas_call(
        _dense_ffn_kernel,
        grid=(TOKENS // tm,),
        in_specs=[row_spec, resident, resident, resident,
                  _const_spec((1, D_MODEL)), _const_spec((1, D_MODEL))],
        out_specs=row_spec,
        out_shape=jax.ShapeDtypeStruct((TOKENS, D_MODEL), F32),
        scratch_shapes=[pltpu.VMEM((tm, D_MODEL), F32)],
        compiler_params=pltpu.CompilerParams(
            dimension_semantics=("arbitrary",), vmem_limit_bytes=VMEM_LIMIT_BYTES),
        name="dense_ffn",
    )(x, w_gate.astype(BF16), w_up.astype(BF16), w_down.astype(BF16), ln_g[None], ln_b[None])


META_E1, META_E2, META_W1, META_W2, META_P1, META_P2 = range(6)
ROUTER_ROWS = 16


def _odd_mixer_kernel(x_ref, win_ref, ccw_ref, sng_ref, snb_ref, sw_ref, sb_ref, wout_ref,
                      g1_ref, b1_ref, rtr_ref, o_ref, meta_ref, cnt_ref,
                      extc_ref, mixin_ref):
    s = pl.program_id(1)
    ts = TS_MIX

    @pl.when(s == 0)
    def _():
        extc_ref[0:HALO_C, :] = jnp.zeros((HALO_C, D_C), F32)

    tri = (lax.broadcasted_iota(jnp.int32, (SGU_BLOCK, SGU_BLOCK), 0)
           >= lax.broadcasted_iota(jnp.int32, (SGU_BLOCK, SGU_BLOCK), 1))
    sgu_w = [jnp.where(tri, sw_ref[hd], 0.0).astype(BF16) for hd in range(SGU_HEADS)]

    chain_rows = ts // MIX_CHAINS
    blocks = chain_rows // SGU_BLOCK
    x_parts = [x_ref[c * chain_rows:(c + 1) * chain_rows, :] for c in range(MIX_CHAINS)]
    xb_parts = [x.astype(BF16) for x in x_parts]

    def project(c, lo, hi):
        return _dot(xb_parts[c], win_ref[:, lo:hi])

    def short_conv(c, conv_in):
        c0, c1 = c * chain_rows, (c + 1) * chain_rows
        extc_ref[HALO_C + c0:HALO_C + c1, :] = conv_in[:, D_C:2 * D_C] * conv_in[:, 2 * D_C:3 * D_C]
        conv = ccw_ref[0:1, :] * extc_ref[HALO_C - 2 + c0:HALO_C - 2 + c1, :]
        conv = conv + ccw_ref[1:2, :] * extc_ref[HALO_C - 1 + c0:HALO_C - 1 + c1, :]
        conv = conv + ccw_ref[2:3, :] * extc_ref[HALO_C + c0:HALO_C + c1, :]
        mixin_ref[c0:c1, 0:D_C] = (conv_in[:, 0:D_C] * conv).astype(BF16)

    def spatial_gate(c, sgu_in):
        c0 = c * chain_rows
        z = jax.nn.gelu(sgu_in)
        u = z[:, :D_D]
        v = _layer_norm(z[:, D_D:], sng_ref[...], snb_ref[...]).astype(BF16)
        for hd in range(SGU_HEADS):
            lo, hi = hd * SGU_HEAD_DIM, (hd + 1) * SGU_HEAD_DIM
            v_blocks = jnp.concatenate(
                [v[blk * SGU_BLOCK:(blk + 1) * SGU_BLOCK, lo:hi] for blk in range(blocks)], axis=1)
            mixed = _dot(sgu_w[hd], v_blocks) + sb_ref[:, hd:hd + 1]
            for blk in range(blocks):
                r0, r1 = blk * SGU_BLOCK, (blk + 1) * SGU_BLOCK
                gated = u[r0:r1, lo:hi] * mixed[:, blk * SGU_HEAD_DIM:(blk + 1) * SGU_HEAD_DIM]
                mixin_ref[c0 + r0:c0 + r1, D_C + lo:D_C + hi] = gated.astype(BF16)

    def finish(c):
        c0, c1 = c * chain_rows, (c + 1) * chain_rows
        mix = _dot(mixin_ref[c0:c1, :], wout_ref[...])
        x1_c = _layer_norm(ALPHA * x_parts[c] + mix, g1_ref[...], b1_ref[...])
        o_ref[c0:c1, :] = x1_c
        return x1_c

    conv_cols, sgu_cols = (0, 3 * D_C), (3 * D_C, IN_ODD)
    conv_in = project(0, *conv_cols)
    sgu_in = project(0, *sgu_cols)
    x1_parts = []
    for c in range(MIX_CHAINS):
        following = c + 1 < MIX_CHAINS
        next_conv_in = project(c + 1, *conv_cols) if following else None
        short_conv(c, conv_in)
        next_sgu_in = project(c + 1, *sgu_cols) if following else None
        spatial_gate(c, sgu_in)
        x1_parts.append(finish(c))
        conv_in, sgu_in = next_conv_in, next_sgu_in
    x1 = jnp.concatenate(x1_parts, axis=0)

    extc_ref[0:HALO_C, :] = extc_ref[ts:ts + HALO_C, :]

    x_hi = x1.astype(BF16)
    x_lo = (x1 - x_hi.astype(F32)).astype(BF16)
    nt = (((1,), (1,)), ((), ()))
    by_hi = lax.dot_general(rtr_ref[...], x_hi, nt, preferred_element_type=F32)
    by_lo = lax.dot_general(rtr_ref[0:ROUTER_ROWS, :], x_lo, nt, preferred_element_type=F32)
    logits = (by_hi[0:N_EXPERTS, :] + by_hi[ROUTER_ROWS:ROUTER_ROWS + N_EXPERTS, :]
              + by_lo[0:N_EXPERTS, :])

    expert = lax.broadcasted_iota(jnp.int32, (N_EXPERTS, ts), 0).astype(F32)
    neg_inf = jnp.float32(-jnp.inf)
    m1 = jnp.max(logits, axis=0, keepdims=True)
    e1 = jnp.min(jnp.where(logits == m1, expert, float(N_EXPERTS)), axis=0, keepdims=True)
    rest = jnp.where(expert == e1, neg_inf, logits)
    m2 = jnp.max(rest, axis=0, keepdims=True)
    e2 = jnp.min(jnp.where(rest == m2, expert, float(N_EXPERTS)), axis=0, keepdims=True)
    ex = jnp.exp(m2 - m1)
    den = 1.0 + ex
    w1 = 1.0 / den
    w2 = ex / den

    onehot = jnp.where(jnp.logical_or(expert == e1, expert == e2), 1.0, 0.0)
    onehot_b = jnp.concatenate([onehot, jnp.zeros_like(onehot)], axis=0).astype(BF16)
    before = (lax.broadcasted_iota(jnp.int32, (ts, ts), 0)
              < lax.broadcasted_iota(jnp.int32, (ts, ts), 1))
    earlier = _dot(onehot_b, jnp.where(before, 1.0, 0.0).astype(BF16))[0:N_EXPERTS, :]
    run = jnp.sum(onehot, axis=1, keepdims=True)
    run = jnp.floor((run + (RUN_ALIGN - 1)) * (1.0 / RUN_ALIGN)) * RUN_ALIGN
    expert_col = expert[:, 0:1]
    lower_rows = jnp.zeros((N_EXPERTS, 1), F32)
    for e in range(N_EXPERTS - 1):
        lower_rows = lower_rows + jnp.where(expert_col > e, run[e:e + 1, :], 0.0)
    place = earlier + lower_rows
    p1 = jnp.sum(jnp.where(expert == e1, place, 0.0), axis=0, keepdims=True)
    p2 = jnp.sum(jnp.where(expert == e2, place, 0.0), axis=0, keepdims=True)

    records = {META_E1: e1, META_E2: e2, META_W1: w1, META_W2: w2, META_P1: p1, META_P2: p2}
    meta_ref[...] = jnp.concatenate(
        [records.get(row, jnp.zeros((1, ts), F32)) for row in range(SUBLANES)], axis=0)
    cnt_ref[...] = jnp.broadcast_to(run, (N_EXPERTS, LANES))


def _odd_mixer(x, w_in, conv_w, sgu_norm_g, sgu_norm_b, sgu_w, sgu_b, w_out, ln_g, ln_b, router):
    ts = TS_MIX
    x_spec = pl.BlockSpec((None, ts, D_MODEL), lambda b, s: (b, s, 0))
    router_pad = jnp.zeros((ROUTER_ROWS, D_MODEL), F32).at[:N_EXPERTS, :].set(router.T)
    router_hi = router_pad.astype(BF16)
    router_lo = (router_pad - router_hi.astype(F32)).astype(BF16)
    router_parts = jnp.concatenate([router_hi, router_lo], axis=0)
    return pl.pallas_call(
        _odd_mixer_kernel,
        grid=(BATCH, SEQ // ts),
        in_specs=[
            x_spec,
            _const_spec((D_MODEL, IN_ODD)),
            _const_spec((CONV_C_WIDTH, D_C)),
            _const_spec((1, D_D)),
            _const_spec((1, D_D)),
            _const_spec((SGU_HEADS, SGU_BLOCK, SGU_BLOCK)),
            _const_spec((SGU_BLOCK, SGU_HEADS)),
            _const_spec((D_C + D_D, D_MODEL)),
            _const_spec((1, D_MODEL)),
            _const_spec((1, D_MODEL)),
            _const_spec((2 * ROUTER_ROWS, D_MODEL)),
        ],
        out_specs=[
            x_spec,
            pl.BlockSpec((SUBLANES, ts), lambda b, s: (0, b * (SEQ // ts) + s)),
            pl.BlockSpec((None, N_EXPERTS, LANES), lambda b, s: (b * (SEQ // ts) + s, 0, 0)),
        ],
        out_shape=[
            jax.ShapeDtypeStruct((BATCH, SEQ, D_MODEL), F32),
            jax.ShapeDtypeStruct((SUBLANES, TOKENS), F32),
            jax.ShapeDtypeStruct((N_TOKEN_TILES, N_EXPERTS, LANES), F32),
        ],
        scratch_shapes=[
            pltpu.VMEM((HALO_C + ts, D_C), F32),
            pltpu.VMEM((ts, D_C + D_D), BF16),
        ],
        compiler_params=pltpu.CompilerParams(
            dimension_semantics=("arbitrary", "arbitrary"), vmem_limit_bytes=VMEM_LIMIT_BYTES),
        name="odd_mixer",
    )(x, w_in.astype(BF16), conv_w, sgu_norm_g[None], sgu_norm_b[None], sgu_w, sgu_b.T,
      w_out.astype(BF16), ln_g[None], ln_b[None], router_parts)


SEGMENT_BITS = range(RUN_ALIGN.bit_length() - 1, TS_MIX.bit_length())


def _for_each_segment_chunk(seg_ref, tile, visit):
    for e in range(N_EXPERTS):
        k = tile * N_EXPERTS + e
        n = seg_ref[k]
        local = seg_ref[N_SEGMENTS + k]
        sorted_row = seg_ref[2 * N_SEGMENTS + k]
        for bit in reversed(SEGMENT_BITS):
            @pl.when(((n >> bit) & 1) == 1)
            def _():
                done = (n >> (bit + 1)) << (bit + 1)
                visit(pl.multiple_of(local + done, RUN_ALIGN), pl.multiple_of(sorted_row + done, RUN_ALIGN),
                      1 << bit)


def _dispatch_kernel(seg_ref, pad_tile_ref, x_ref, meta_ref, xs_ref, zero_ref, perm_ref, zero_sem, perm_sem):
    i = pl.program_id(0)
    slot = i % 2

    def chunk_copy(which):
        def make(local, sorted_row, size):
            return pltpu.make_async_copy(perm_ref.at[which, pl.ds(local, size)],
                                         xs_ref.at[pl.ds(sorted_row, size)], perm_sem.at[which])
        return make

    def wait_tile(tile, which):
        _for_each_segment_chunk(seg_ref, tile, lambda *chunk: chunk_copy(which)(*chunk).wait())

    @pl.when(i == 0)
    def _():
        zero_ref[...] = jnp.zeros(zero_ref.shape, BF16)

        def clear_copy(t):
            start = pl.multiple_of(t * TM_MOE, TM_MOE)
            return pltpu.make_async_copy(zero_ref, xs_ref.at[pl.ds(start, TM_MOE)], zero_sem)

        def for_each_cleared_tile(visit):
            for e in range(N_EXPERTS):
                @pl.when(pad_tile_ref[e] >= 0)
                def _():
                    visit(pad_tile_ref[e])

            def tail(t, carry):
                visit(t)
                return carry

            lax.fori_loop(pad_tile_ref[N_EXPERTS], N_TILES_MOE, tail, 0)

        for_each_cleared_tile(lambda t: clear_copy(t).start())
        for_each_cleared_tile(lambda t: clear_copy(t).wait())

    meta = meta_ref[...]
    row = lax.broadcasted_iota(jnp.int32, (SLOT_ROWS, TS_MIX), 0).astype(F32)
    hit = jnp.logical_or(row == meta[META_P1:META_P1 + 1, :], row == meta[META_P2:META_P2 + 1, :])
    onehot = jnp.where(hit, 1.0, 0.0).astype(BF16)
    permuted = _dot(onehot, x_ref[...].astype(BF16))

    @pl.when(i >= 2)
    def _():
        wait_tile(i - 2, slot)

    perm_ref[slot] = permuted.astype(BF16)
    _for_each_segment_chunk(seg_ref, i, lambda *chunk: chunk_copy(slot)(*chunk).start())

    @pl.when(i == N_TOKEN_TILES - 1)
    def _():
        wait_tile(i - 1, 1 - slot)
        wait_tile(i, slot)


def _dispatch(x, meta, seg, pad_tile):
    return pl.pallas_call(
        _dispatch_kernel,
        grid_spec=pltpu.PrefetchScalarGridSpec(
            num_scalar_prefetch=2,
            grid=(N_TOKEN_TILES,),
            in_specs=[pl.BlockSpec((TS_MIX, D_MODEL), lambda i, *_: (i, 0)),
                      pl.BlockSpec((SUBLANES, TS_MIX), lambda i, *_: (0, i))],
            out_specs=pl.BlockSpec(memory_space=pl.ANY),
            scratch_shapes=[
                pltpu.VMEM((TM_MOE, D_MODEL), BF16),
                pltpu.VMEM((2, SLOT_ROWS, D_MODEL), BF16),
                pltpu.SemaphoreType.DMA(()),
                pltpu.SemaphoreType.DMA((2,)),
            ],
        ),
        out_shape=jax.ShapeDtypeStruct((ROWS_SORTED, D_MODEL), BF16),
        compiler_params=pltpu.CompilerParams(
            dimension_semantics=("arbitrary",), vmem_limit_bytes=VMEM_LIMIT_BYTES),
        name="moe_dispatch",
    )(seg, pad_tile, x, meta)


def _grouped_ffn_kernel(tile_expert_ref, tile_rows_ref, n_active_ref, xs_ref, wg_ref, wu_ref, wd_ref,
                        ys_ref, acc_ref):
    i = pl.program_id(0)
    j = pl.program_id(1)

    active = i < n_active_ref[0]
    rows = tile_rows_ref[i]

    uncovered = jnp.logical_or(jnp.logical_not(active), rows <= TM_MOE - SUB_MOE)

    @pl.when(jnp.logical_and(j == 0, jnp.logical_or(i == 0, uncovered)))
    def _():
        acc_ref[...] = jnp.zeros((TM_MOE, D_MODEL), F32)
        ys_ref[...] = jnp.zeros((TM_MOE, D_MODEL), BF16)

    def accumulate(n_rows):
        xb = xs_ref[0:n_rows, :]
        a = _silu(_dot(xb, wg_ref[...].astype(BF16))) * _dot(xb, wu_ref[...].astype(BF16))
        part = _dot(a.astype(BF16), wd_ref[...].astype(BF16))
        total = jnp.where(j == 0, part, acc_ref[0:n_rows, :] + part)
        acc_ref[0:n_rows, :] = total
        ys_ref[0:n_rows, :] = total.astype(BF16)

    for n_sub in range(1, TM_MOE // SUB_MOE + 1):
        lo, hi = (n_sub - 1) * SUB_MOE, n_sub * SUB_MOE

        @pl.when(jnp.logical_and(active, jnp.logical_and(rows > lo, rows <= hi)))
        def _():
            accumulate(hi)


def _grouped_ffn(xs, w_gate, w_up, w_down, tile_expert, tile_rows, n_active):
    nj = D_FF_EXPERT // TF_MOE

    def row_map(i, j, te, tr, na):
        return (jnp.minimum(i, na[0] - 1), 0)

    def frozen_j(i, j, na):
        return jnp.where(i < na[0], j, nj - 1)

    def up_map(i, j, te, tr, na):
        return (te[i], 0, frozen_j(i, j, na))

    def down_map(i, j, te, tr, na):
        return (te[i], frozen_j(i, j, na), 0)

    return pl.pallas_call(
        _grouped_ffn_kernel,
        grid_spec=pltpu.PrefetchScalarGridSpec(
            num_scalar_prefetch=3,
            grid=(N_TILES_MOE, nj),
            in_specs=[
                pl.BlockSpec((TM_MOE, D_MODEL), row_map),
                pl.BlockSpec((None, D_MODEL, TF_MOE), up_map),
                pl.BlockSpec((None, D_MODEL, TF_MOE), up_map),
                pl.BlockSpec((None, TF_MOE, D_MODEL), down_map),
            ],
            out_specs=pl.BlockSpec((TM_MOE, D_MODEL), lambda i, j, *_: (i, 0)),
            scratch_shapes=[pltpu.VMEM((TM_MOE, D_MODEL), F32)],
        ),
        out_shape=jax.ShapeDtypeStruct((ROWS_SORTED, D_MODEL), BF16),
        compiler_params=pltpu.CompilerParams(
            dimension_semantics=("arbitrary", "arbitrary"), vmem_limit_bytes=VMEM_LIMIT_BYTES),
        name="moe_grouped_ffn",
    )(tile_expert, tile_rows, n_active, xs, w_gate, w_up, w_down)


def _combine_kernel(seg_ref, x_ref, meta_ref, ys_ref, g_ref, b_ref, o_ref, win_ref, sem):
    i = pl.program_id(0)
    slot = i % 2

    def chunk_copy(which):
        def make(local, sorted_row, size):
            return pltpu.make_async_copy(ys_ref.at[pl.ds(sorted_row, size)],
                                         win_ref.at[which, pl.ds(local, size)], sem.at[which])
        return make

    def fetch(tile, which):
        _for_each_segment_chunk(seg_ref, tile, lambda *chunk: chunk_copy(which)(*chunk).start())

    @pl.when(i == 0)
    def _():
        win_ref[...] = jnp.zeros(win_ref.shape, BF16)
        fetch(0, 0)

    @pl.when(i + 1 < N_TOKEN_TILES)
    def _():
        fetch(i + 1, 1 - slot)

    _for_each_segment_chunk(seg_ref, i, lambda *chunk: chunk_copy(slot)(*chunk).wait())

    meta = meta_ref[...]
    row = lax.broadcasted_iota(jnp.int32, (SLOT_ROWS, TS_MIX), 0).astype(F32)
    gate = (jnp.where(row == meta[META_P1:META_P1 + 1, :], meta[META_W1:META_W1 + 1, :], 0.0)
            + jnp.where(row == meta[META_P2:META_P2 + 1, :], meta[META_W2:META_W2 + 1, :], 0.0))
    moe = lax.dot_general(gate.astype(BF16), win_ref[slot], (((0,), (0,)), ((), ())),
                          preferred_element_type=F32)
    o_ref[...] = _layer_norm(ALPHA * x_ref[...] + moe, g_ref[...], b_ref[...])


def _combine(x, meta, ys, seg, ln_g, ln_b):
    row_spec = pl.BlockSpec((TS_MIX, D_MODEL), lambda i, *_: (i, 0))
    return pl.pallas_call(
        _combine_kernel,
        grid_spec=pltpu.PrefetchScalarGridSpec(
            num_scalar_prefetch=1,
            grid=(N_TOKEN_TILES,),
            in_specs=[
                row_spec,
                pl.BlockSpec((SUBLANES, TS_MIX), lambda i, *_: (0, i)),
                pl.BlockSpec(memory_space=pl.ANY),
                pl.BlockSpec((1, D_MODEL), lambda i, *_: (0, 0)),
                pl.BlockSpec((1, D_MODEL), lambda i, *_: (0, 0)),
            ],
            out_specs=row_spec,
            scratch_shapes=[
                pltpu.VMEM((2, SLOT_ROWS, D_MODEL), BF16),
                pltpu.SemaphoreType.DMA((2,)),
            ],
        ),
        out_shape=jax.ShapeDtypeStruct((TOKENS, D_MODEL), F32),
        compiler_params=pltpu.CompilerParams(
            dimension_semantics=("arbitrary",), vmem_limit_bytes=VMEM_LIMIT_BYTES),
        name="moe_combine",
    )(seg, x, meta, ys, ln_g[None], ln_b[None])


def _routing_tables(tile_counts):
    n = tile_counts[:, :, 0].astype(jnp.int32)
    counts = jnp.sum(n, axis=0)
    tiles = (counts + TM_MOE - 1) // TM_MOE
    tile_end = jnp.cumsum(tiles)
    tile_start = tile_end - tiles
    offset = tile_start * TM_MOE
    local_start = jnp.cumsum(n, axis=1) - n
    sorted_start = offset[None, :] + jnp.cumsum(n, axis=0) - n
    seg = jnp.concatenate([n.reshape(-1), local_start.reshape(-1), sorted_start.reshape(-1)])
    n_active = tile_end[-1:]
    tile_id = jnp.minimum(jnp.arange(N_TILES_MOE, dtype=jnp.int32), n_active[0] - 1)
    tile_expert = jnp.sum(tile_id[:, None] >= tile_end[None, :], axis=1).astype(jnp.int32)
    tile_rows = jnp.clip(counts[tile_expert] - (tile_id - tile_start[tile_expert]) * TM_MOE, 0, TM_MOE)
    pad_tile = jnp.concatenate([jnp.where(tiles > 0, tile_end - 1, -1), n_active]).astype(jnp.int32)
    return (seg.astype(jnp.int32), pad_tile, tile_expert, tile_rows.astype(jnp.int32),
            n_active.astype(jnp.int32))


def _even_layer(x, w_in, conv_a_w, conv_a_b, norm_a_g, norm_a_b, pool_w, pool_scale, w_out,
                ln1_g, ln1_b, ffn_w_gate, ffn_w_up, ffn_w_down, ln2_g, ln2_b):
    x = _even_mixer(x, w_in, conv_a_w, conv_a_b, norm_a_g, norm_a_b, pool_w, pool_scale, w_out,
                    ln1_g, ln1_b)
    x = _dense_ffn(x.reshape(TOKENS, D_MODEL), ffn_w_gate, ffn_w_up, ffn_w_down, ln2_g, ln2_b)
    return x.reshape(BATCH, SEQ, D_MODEL)


def _odd_layer(x, w_in, conv_c_w, sgu_norm_g, sgu_norm_b, sgu_w, sgu_b, w_out, ln1_g, ln1_b,
               router, moe_w_gate, moe_w_up, moe_w_down, ln2_g, ln2_b):
    x, meta, tile_counts = _odd_mixer(x, w_in, conv_c_w, sgu_norm_g, sgu_norm_b, sgu_w, sgu_b, w_out,
                                      ln1_g, ln1_b, router)
    x = x.reshape(TOKENS, D_MODEL)
    seg, pad_tile, tile_expert, tile_rows, n_active = _routing_tables(tile_counts)
    xs = _dispatch(x, meta, seg, pad_tile)
    ys = _grouped_ffn(xs, moe_w_gate, moe_w_up, moe_w_down, tile_expert, tile_rows, n_active)
    x = _combine(x, meta, ys, seg, ln2_g, ln2_b)
    return x.reshape(BATCH, SEQ, D_MODEL)


def kernel(x, even_w_in, even_conv_a_w, even_conv_a_b, even_norm_a_g, even_norm_a_b, even_pool_w, even_pool_scale, even_w_out, even_ln1_g, even_ln1_b, even_ffn_w_gate, even_ffn_w_up, even_ffn_w_down, even_ln2_g, even_ln2_b, odd_w_in, odd_conv_c_w, odd_sgu_norm_g, odd_sgu_norm_b, odd_sgu_w, odd_sgu_b, odd_w_out, odd_ln1_g, odd_ln1_b, odd_router, odd_moe_w_gate, odd_moe_w_up, odd_moe_w_down, odd_ln2_g, odd_ln2_b):
    for layer in range(DEPTH):
        i = layer // 2
        if layer % 2 == 0:
            x = _even_layer(x, even_w_in[i], even_conv_a_w[i], even_conv_a_b[i], even_norm_a_g[i],
                            even_norm_a_b[i], even_pool_w[i], even_pool_scale[i], even_w_out[i],
                            even_ln1_g[i], even_ln1_b[i], even_ffn_w_gate[i], even_ffn_w_up[i],
                            even_ffn_w_down[i], even_ln2_g[i], even_ln2_b[i])
        else:
            x = _odd_layer(x, odd_w_in[i], odd_conv_c_w[i], odd_sgu_norm_g[i], odd_sgu_norm_b[i],
                           odd_sgu_w[i], odd_sgu_b[i], odd_w_out[i], odd_ln1_g[i], odd_ln1_b[i],
                           odd_router[i], odd_moe_w_gate[i], odd_moe_w_up[i], odd_moe_w_down[i],
                           odd_ln2_g[i], odd_ln2_b[i])
    return x
```

```python
import jax
import jax.numpy as jnp
from jax import lax
from jax.experimental import pallas as pl
from jax.experimental.pallas import tpu as pltpu

F32 = jnp.float32
BF16 = jnp.bfloat16

D_MODEL = 1024
BATCH = 4
SEQ = 4096
TOKENS = BATCH * SEQ
DEPTH = 2

D_A = 512
CONV_A_WIDTH = 31
D_B = 512
POOL_WINDOWS = (2, 4, 8, 16)
POOL_GROUP_DIM = 128
IN_EVEN = 2 * D_A + D_B

D_C = 512
CONV_C_WIDTH = 3
D_D = 512
SGU_BLOCK = 128
SGU_HEADS = 4
SGU_HEAD_DIM = 128
IN_ODD = 3 * D_C + 2 * D_D

D_FF_DENSE = 2816
N_EXPERTS = 8
D_FF_EXPERT = 3584

ALPHA = (2 * DEPTH) ** 0.25
LN_EPS = 1e-5

SUBLANES = 8
LANES = 128
VMEM_LIMIT_BYTES = 56 * 1024 * 1024

TS_MIX = 512
EVEN_BANDS = 2
COMBINE_BANDS = 4
FFN_BANDS = 4
MIX_CHAINS = 2
ROW_SET_STRIDE = 4
CONV_SETS = 8
HALO_A = 32
HALO_B = 16
HALO_C = 8
TM_FFN = 1024
FF_CHUNK = 1024
TM_MOE = 1024
SUB_MOE = 128
TF_MOE = 512
N_TOKEN_TILES = TOKENS // TS_MIX
N_SEGMENTS = N_TOKEN_TILES * N_EXPERTS
RUN_ALIGN = 16
SLOT_ROWS = 2 * TS_MIX + N_EXPERTS * RUN_ALIGN
MAX_SORTED_ROWS = 2 * TOKENS + N_SEGMENTS * (RUN_ALIGN - 1)
N_TILES_MOE = (MAX_SORTED_ROWS + N_EXPERTS * (TM_MOE - 1)) // TM_MOE
ROWS_SORTED = N_TILES_MOE * TM_MOE


def _layer_norm(x, g, b):
    mu = jnp.mean(x, axis=-1, keepdims=True)
    xc = x - mu
    var = jnp.mean(xc * xc, axis=-1, keepdims=True)
    return xc * lax.rsqrt(var + LN_EPS) * g + b


def _silu(x):
    return x * jax.nn.sigmoid(x)


def _dot(a, b):
    return jnp.dot(a, b, preferred_element_type=F32)


def _even_mixer_kernel(x_ref, win_ref, cw_ref, cb_ref, nag_ref, nab_ref, pw_ref, ps_ref,
                       wout_ref, g1_ref, b1_ref, o_ref, exta_ref, extb_ref, mixf_ref, mixin_ref):
    s = pl.program_id(1)
    ts = TS_MIX

    n_a, n_b = D_A // LANES, D_B // LANES

    @pl.when(s == 0)
    def _():
        exta_ref[:, 0:HALO_A, :] = jnp.zeros((n_a, HALO_A, LANES), F32)
        extb_ref[:, 0:HALO_B, :] = jnp.zeros((n_b, HALO_B, LANES), F32)

    def row_set(first):
        return pl.ds(first, SUBLANES, stride=ROW_SET_STRIDE)

    set_firsts = [blk * SUBLANES * ROW_SET_STRIDE + r
                  for blk in range(ts // (SUBLANES * ROW_SET_STRIDE)) for r in range(ROW_SET_STRIDE)]
    set_step = ROW_SET_STRIDE * lax.broadcasted_iota(jnp.int32, (SUBLANES, 1), 0)

    def pool_sets(firsts):
        for first in firsts:
            row_pos = s * ts + first + set_step
            for g, win in enumerate(POOL_WINDOWS):
                frame = extb_ref[g, row_set(HALO_B + first), :]
                wsum = frame
                for i in range(1, win):
                    wsum = wsum + extb_ref[g, row_set(HALO_B + first - i), :]
                if first >= win - 1:
                    mean = wsum * (1.0 / win)
                else:
                    mean = wsum / jnp.minimum(row_pos + 1, win).astype(F32)
                mixf_ref[n_a + g, row_set(first), :] = mean - frame

    x = x_ref[...]
    xb = x.astype(BF16)
    hb = _dot(xb, win_ref[:, 2 * D_A:])
    for cb in range(n_b):
        extb_ref[cb, HALO_B:HALO_B + ts, :] = hb[:, cb * LANES:(cb + 1) * LANES]
    parts = 2
    cols = D_A // parts
    sets_per_part = len(set_firsts) // parts
    for part in range(parts):
        pool_sets(set_firsts[part * sets_per_part:(part + 1) * sets_per_part])
        val = _dot(xb, win_ref[:, part * cols:(part + 1) * cols])
        gate = _dot(xb, win_ref[:, D_A + part * cols:D_A + (part + 1) * cols])
        glu = val * jax.nn.sigmoid(gate)
        for cb in range(cols // LANES):
            exta_ref[part * (cols // LANES) + cb, HALO_A:HALO_A + ts, :] = glu[:, cb * LANES:(cb + 1) * LANES]

    for g in range(n_b):
        lo, hi = g * POOL_GROUP_DIM, (g + 1) * POOL_GROUP_DIM
        mixed = _dot(mixf_ref[n_a + g].astype(BF16), pw_ref[g]) * ps_ref[:, lo:hi]
        mixin_ref[:, D_A + lo:D_A + hi] = mixed.astype(BF16)
    mix_pool = _dot(mixin_ref[:, D_A:], wout_ref[D_A:, :])

    band_rows = ts // EVEN_BANDS
    sets_per_band = len(set_firsts) // EVEN_BANDS
    norm_rows = 128
    for band in range(EVEN_BANDS):
        band_sets = set_firsts[band * sets_per_band:(band + 1) * sets_per_band]
        for cb in range(n_a):
            lanes = slice(cb * LANES, (cb + 1) * LANES)
            for group in range(0, len(band_sets), CONV_SETS):
                firsts = band_sets[group:group + CONV_SETS]
                accs = [jnp.broadcast_to(cb_ref[:, lanes], (SUBLANES, LANES))] * len(firsts)
                for k in range(CONV_A_WIDTH):
                    wk = jnp.broadcast_to(cw_ref[k:k + 1, lanes], (SUBLANES, LANES))
                    back = CONV_A_WIDTH - 1 - k
                    for n, first in enumerate(firsts):
                        accs[n] = accs[n] + wk * exta_ref[cb, row_set(HALO_A + first - back), :]
                for n, first in enumerate(firsts):
                    mixf_ref[cb, row_set(first), :] = accs[n]

        b0, b1 = band * band_rows, (band + 1) * band_rows
        for r0 in range(b0, b1, norm_rows):
            conv = jnp.concatenate([mixf_ref[cb, r0:r0 + norm_rows, :] for cb in range(n_a)], axis=1)
            y = _silu(_layer_norm(conv, nag_ref[...], nab_ref[...]))
            mixin_ref[r0:r0 + norm_rows, 0:D_A] = y.astype(BF16)
        mix = _dot(mixin_ref[b0:b1, 0:D_A], wout_ref[0:D_A, :]) + mix_pool[b0:b1, :]
        o_ref[b0:b1, :] = _layer_norm(ALPHA * x[b0:b1, :] + mix, g1_ref[...], b1_ref[...])

    exta_ref[:, 0:HALO_A, :] = exta_ref[:, ts:ts + HALO_A, :]
    extb_ref[:, 0:HALO_B, :] = extb_ref[:, ts:ts + HALO_B, :]


def _const_spec(shape):
    return pl.BlockSpec(shape, lambda *_: (0,) * len(shape))


def _even_mixer(x, w_in, conv_w, conv_b, norm_g, norm_b, pool_w, pool_scale, w_out, ln_g, ln_b):
    ts = TS_MIX
    x_spec = pl.BlockSpec((None, ts, D_MODEL), lambda b, s: (b, s, 0))
    return pl.pallas_call(
        _even_mixer_kernel,
        grid=(BATCH, SEQ // ts),
        in_specs=[
            x_spec,
            _const_spec((D_MODEL, IN_EVEN)),
            _const_spec((CONV_A_WIDTH, D_A)),
            _const_spec((1, D_A)),
            _const_spec((1, D_A)),
            _const_spec((1, D_A)),
            _const_spec((len(POOL_WINDOWS), POOL_GROUP_DIM, POOL_GROUP_DIM)),
            _const_spec((1, D_B)),
            _const_spec((D_A + D_B, D_MODEL)),
            _const_spec((1, D_MODEL)),
            _const_spec((1, D_MODEL)),
        ],
        out_specs=x_spec,
        out_shape=jax.ShapeDtypeStruct((BATCH, SEQ, D_MODEL), F32),
        scratch_shapes=[
            pltpu.VMEM((D_A // LANES, HALO_A + ts, LANES), F32),
            pltpu.VMEM((D_B // LANES, HALO_B + ts, LANES), F32),
            pltpu.VMEM(((D_A + D_B) // LANES, ts, LANES), F32),
            pltpu.VMEM((ts, D_A + D_B), BF16),
        ],
        compiler_params=pltpu.CompilerParams(
            dimension_semantics=("arbitrary", "arbitrary"), vmem_limit_bytes=VMEM_LIMIT_BYTES),
        name="even_mixer",
    )(x, w_in.astype(BF16), conv_w, conv_b[None], norm_g[None], norm_b[None],
      pool_w.astype(BF16), pool_scale[None], w_out.astype(BF16), ln_g[None], ln_b[None])


def _ff_chunks(total, chunk):
    bounds = list(range(0, total, chunk)) + [total]
    return list(zip(bounds[:-1], bounds[1:]))


def _dense_ffn_kernel(x_ref, wg_ref, wu_ref, wd_ref, g_ref, b_ref, o_ref, acc_ref):
    x = x_ref[...]
    xb = x.astype(BF16)
    chunks = _ff_chunks(D_FF_DENSE, FF_CHUNK)
    for n, (lo, hi) in enumerate(chunks[:-1]):
        a = _silu(_dot(xb, wg_ref[:, lo:hi])) * _dot(xb, wu_ref[:, lo:hi])
        part = _dot(a.astype(BF16), wd_ref[lo:hi, :])
        if n == 0:
            acc_ref[...] = part
        else:
            acc_ref[...] += part
    lo, hi = chunks[-1]
    a = (_silu(_dot(xb, wg_ref[:, lo:hi])) * _dot(xb, wu_ref[:, lo:hi])).astype(BF16)
    band = TM_FFN // FFN_BANDS
    for b0 in range(0, TM_FFN, band):
        total = acc_ref[b0:b0 + band, :] + _dot(a[b0:b0 + band, :], wd_ref[lo:hi, :])
        o_ref[b0:b0 + band, :] = _layer_norm(ALPHA * x[b0:b0 + band, :] + total, g_ref[...], b_ref[...])


def _dense_ffn(x, w_gate, w_up, w_down, ln_g, ln_b):
    tm = TM_FFN
    row_spec = pl.BlockSpec((tm, D_MODEL), lambda i: (i, 0))
    resident = pl.BlockSpec(memory_space=pltpu.VMEM)
    return pl.pallas_call(
        _dense_ffn_kernel,
        grid=(TOKENS // tm,),
        in_specs=[row_spec, resident, resident, resident,
                  _const_spec((1, D_MODEL)), _const_spec((1, D_MODEL))],
        out_specs=row_spec,
        out_shape=jax.ShapeDtypeStruct((TOKENS, D_MODEL), F32),
        scratch_shapes=[pltpu.VMEM((tm, D_MODEL), F32)],
        compiler_params=pltpu.CompilerParams(
            dimension_semantics=("arbitrary",), vmem_limit_bytes=VMEM_LIMIT_BYTES),
        name="dense_ffn",
    )(x, w_gate.astype(BF16), w_up.astype(BF16), w_down.astype(BF16), ln_g[None], ln_b[None])


META_E1, META_E2, META_W1, META_W2, META_P1, META_P2 = range(6)
ROUTER_ROWS = 16


def _odd_mixer_kernel(x_ref, win_ref, ccw_ref, sng_ref, snb_ref, sw_ref, sb_ref, wout_ref,
                      g1_ref, b1_ref, rtr_ref, o_ref, meta_ref, cnt_ref,
                      extc_ref, mixin_ref):
    s = pl.program_id(1)
    ts = TS_MIX

    @pl.when(s == 0)
    def _():
        extc_ref[0:HALO_C, :] = jnp.zeros((HALO_C, D_C), F32)

    tri = (lax.broadcasted_iota(jnp.int32, (SGU_BLOCK, SGU_BLOCK), 0)
           >= lax.broadcasted_iota(jnp.int32, (SGU_BLOCK, SGU_BLOCK), 1))
    sgu_w = [jnp.where(tri, sw_ref[hd], 0.0).astype(BF16) for hd in range(SGU_HEADS)]

    chain_rows = ts // MIX_CHAINS
    blocks = chain_rows // SGU_BLOCK
    x_parts = [x_ref[c * chain_rows:(c + 1) * chain_rows, :] for c in range(MIX_CHAINS)]
    xb_parts = [x.astype(BF16) for x in x_parts]

    def project(c, lo, hi):
        return _dot(xb_parts[c], win_ref[:, lo:hi])

    def short_conv(c, conv_in):
        c0, c1 = c * chain_rows, (c + 1) * chain_rows
        extc_ref[HALO_C + c0:HALO_C + c1, :] = conv_in[:, D_C:2 * D_C] * conv_in[:, 2 * D_C:3 * D_C]
        conv = ccw_ref[0:1, :] * extc_ref[HALO_C - 2 + c0:HALO_C - 2 + c1, :]
        conv = conv + ccw_ref[1:2, :] * extc_ref[HALO_C - 1 + c0:HALO_C - 1 + c1, :]
        conv = conv + ccw_ref[2:3, :] * extc_ref[HALO_C + c0:HALO_C + c1, :]
        mixin_ref[c0:c1, 0:D_C] = (conv_in[:, 0:D_C] * conv).astype(BF16)

    def spatial_gate(c, sgu_in):
        c0 = c * chain_rows
        z = jax.nn.gelu(sgu_in)
        u = z[:, :D_D]
        v = _layer_norm(z[:, D_D:], sng_ref[...], snb_ref[...]).astype(BF16)
        for hd in range(SGU_HEADS):
            lo, hi = hd * SGU_HEAD_DIM, (hd + 1) * SGU_HEAD_DIM
            v_blocks = jnp.concatenate(
                [v[blk * SGU_BLOCK:(blk + 1) * SGU_BLOCK, lo:hi] for blk in range(blocks)], axis=1)
            mixed = _dot(sgu_w[hd], v_blocks) + sb_ref[:, hd:hd + 1]
            for blk in range(blocks):
                r0, r1 = blk * SGU_BLOCK, (blk + 1) * SGU_BLOCK
                gated = u[r0:r1, lo:hi] * mixed[:, blk * SGU_HEAD_DIM:(blk + 1) * SGU_HEAD_DIM]
                mixin_ref[c0 + r0:c0 + r1, D_C + lo:D_C + hi] = gated.astype(BF16)

    def finish(c):
        c0, c1 = c * chain_rows, (c + 1) * chain_rows
        mix = _dot(mixin_ref[c0:c1, :], wout_ref[...])
        x1_c = _layer_norm(ALPHA * x_parts[c] + mix, g1_ref[...], b1_ref[...])
        o_ref[c0:c1, :] = x1_c
        return x1_c

    conv_cols, sgu_cols = (0, 3 * D_C), (3 * D_C, IN_ODD)
    conv_in = project(0, *conv_cols)
    sgu_in = project(0, *sgu_cols)
    x1_parts = []
    for c in range(MIX_CHAINS):
        following = c + 1 < MIX_CHAINS
        next_conv_in = project(c + 1, *conv_cols) if following else None
        short_conv(c, conv_in)
        next_sgu_in = project(c + 1, *sgu_cols) if following else None
        spatial_gate(c, sgu_in)
        x1_parts.append(finish(c))
        conv_in, sgu_in = next_conv_in, next_sgu_in
    x1 = jnp.concatenate(x1_parts, axis=0)

    extc_ref[0:HALO_C, :] = extc_ref[ts:ts + HALO_C, :]

    x_hi = x1.astype(BF16)
    x_lo = (x1 - x_hi.astype(F32)).astype(BF16)
    nt = (((1,), (1,)), ((), ()))
    by_hi = lax.dot_general(rtr_ref[...], x_hi, nt, preferred_element_type=F32)
    by_lo = lax.dot_general(rtr_ref[0:ROUTER_ROWS, :], x_lo, nt, preferred_element_type=F32)
    logits = (by_hi[0:N_EXPERTS, :] + by_hi[ROUTER_ROWS:ROUTER_ROWS + N_EXPERTS, :]
              + by_lo[0:N_EXPERTS, :])

    expert = lax.broadcasted_iota(jnp.int32, (N_EXPERTS, ts), 0).astype(F32)
    neg_inf = jnp.float32(-jnp.inf)
    m1 = jnp.max(logits, axis=0, keepdims=True)
    e1 = jnp.min(jnp.where(logits == m1, expert, float(N_EXPERTS)), axis=0, keepdims=True)
    rest = jnp.where(expert == e1, neg_inf, logits)
    m2 = jnp.max(rest, axis=0, keepdims=True)
    e2 = jnp.min(jnp.where(rest == m2, expert, float(N_EXPERTS)), axis=0, keepdims=True)
    ex = jnp.exp(m2 - m1)
    den = 1.0 + ex
    w1 = 1.0 / den
    w2 = ex / den

    onehot = jnp.where(jnp.logical_or(expert == e1, expert == e2), 1.0, 0.0)
    onehot_b = jnp.concatenate([onehot, jnp.zeros_like(onehot)], axis=0).astype(BF16)
    before = (lax.broadcasted_iota(jnp.int32, (ts, ts), 0)
              < lax.broadcasted_iota(jnp.int32, (ts, ts), 1))
    earlier = _dot(onehot_b, jnp.where(before, 1.0, 0.0).astype(BF16))[0:N_EXPERTS, :]
    run = jnp.sum(onehot, axis=1, keepdims=True)
    run = jnp.floor((run + (RUN_ALIGN - 1)) * (1.0 / RUN_ALIGN)) * RUN_ALIGN
    expert_col = expert[:, 0:1]
    lower_rows = jnp.zeros((N_EXPERTS, 1), F32)
    for e in range(N_EXPERTS - 1):
        lower_rows = lower_rows + jnp.where(expert_col > e, run[e:e + 1, :], 0.0)
    place = earlier + lower_rows
    p1 = jnp.sum(jnp.where(expert == e1, place, 0.0), axis=0, keepdims=True)
    p2 = jnp.sum(jnp.where(expert == e2, place, 0.0), axis=0, keepdims=True)

    records = {META_E1: e1, META_E2: e2, META_W1: w1, META_W2: w2, META_P1: p1, META_P2: p2}
    meta_ref[...] = jnp.concatenate(
        [records.get(row, jnp.zeros((1, ts), F32)) for row in range(SUBLANES)], axis=0)
    cnt_ref[...] = jnp.broadcast_to(run, (N_EXPERTS, LANES))


def _odd_mixer(x, w_in, conv_w, sgu_norm_g, sgu_norm_b, sgu_w, sgu_b, w_out, ln_g, ln_b, router):
    ts = TS_MIX
    x_spec = pl.BlockSpec((None, ts, D_MODEL), lambda b, s: (b, s, 0))
    router_pad = jnp.zeros((ROUTER_ROWS, D_MODEL), F32).at[:N_EXPERTS, :].set(router.T)
    router_hi = router_pad.astype(BF16)
    router_lo = (router_pad - router_hi.astype(F32)).astype(BF16)
    router_parts = jnp.concatenate([router_hi, router_lo], axis=0)
    return pl.pallas_call(
        _odd_mixer_kernel,
        grid=(BATCH, SEQ // ts),
        in_specs=[
            x_spec,
            _const_spec((D_MODEL, IN_ODD)),
            _const_spec((CONV_C_WIDTH, D_C)),
            _const_spec((1, D_D)),
            _const_spec((1, D_D)),
            _const_spec((SGU_HEADS, SGU_BLOCK, SGU_BLOCK)),
            _const_spec((SGU_BLOCK, SGU_HEADS)),
            _const_spec((D_C + D_D, D_MODEL)),
            _const_spec((1, D_MODEL)),
            _const_spec((1, D_MODEL)),
            _const_spec((2 * ROUTER_ROWS, D_MODEL)),
        ],
        out_specs=[
            x_spec,
            pl.BlockSpec((SUBLANES, ts), lambda b, s: (0, b * (SEQ // ts) + s)),
            pl.BlockSpec((None, N_EXPERTS, LANES), lambda b, s: (b * (SEQ // ts) + s, 0, 0)),
        ],
        out_shape=[
            jax.ShapeDtypeStruct((BATCH, SEQ, D_MODEL), F32),
            jax.ShapeDtypeStruct((SUBLANES, TOKENS), F32),
            jax.ShapeDtypeStruct((N_TOKEN_TILES, N_EXPERTS, LANES), F32),
        ],
        scratch_shapes=[
            pltpu.VMEM((HALO_C + ts, D_C), F32),
            pltpu.VMEM((ts, D_C + D_D), BF16),
        ],
        compiler_params=pltpu.CompilerParams(
            dimension_semantics=("arbitrary", "arbitrary"), vmem_limit_bytes=VMEM_LIMIT_BYTES),
        name="odd_mixer",
    )(x, w_in.astype(BF16), conv_w, sgu_norm_g[None], sgu_norm_b[None], sgu_w, sgu_b.T,
      w_out.astype(BF16), ln_g[None], ln_b[None], router_parts)


SEGMENT_BITS = range(RUN_ALIGN.bit_length() - 1, TS_MIX.bit_length())


def _for_each_segment_chunk(seg_ref, tile, visit):
    for e in range(N_EXPERTS):
        k = tile * N_EXPERTS + e
        n = seg_ref[k]
        local = seg_ref[N_SEGMENTS + k]
        sorted_row = seg_ref[2 * N_SEGMENTS + k]
        for bit in reversed(SEGMENT_BITS):
            @pl.when(((n >> bit) & 1) == 1)
            def _():
                done = (n >> (bit + 1)) << (bit + 1)
                visit(pl.multiple_of(local + done, RUN_ALIGN), pl.multiple_of(sorted_row + done, RUN_ALIGN),
                      1 << bit)


def _dispatch_kernel(seg_ref, pad_tile_ref, x_ref, meta_ref, xs_ref, zero_ref, perm_ref, zero_sem, perm_sem):
    i = pl.program_id(0)
    slot = i % 2

    def chunk_copy(which):
        def make(local, sorted_row, size):
            return pltpu.make_async_copy(perm_ref.at[which, pl.ds(local, size)],
                                         xs_ref.at[pl.ds(sorted_row, size)], perm_sem.at[which])
        return make

    def wait_tile(tile, which):
        _for_each_segment_chunk(seg_ref, tile, lambda *chunk: chunk_copy(which)(*chunk).wait())

    @pl.when(i == 0)
    def _():
        zero_ref[...] = jnp.zeros(zero_ref.shape, BF16)

        def clear_copy(t):
            start = pl.multiple_of(t * TM_MOE, TM_MOE)
            return pltpu.make_async_copy(zero_ref, xs_ref.at[pl.ds(start, TM_MOE)], zero_sem)

        def for_each_cleared_tile(visit):
            for e in range(N_EXPERTS):
                @pl.when(pad_tile_ref[e] >= 0)
                def _():
                    visit(pad_tile_ref[e])

            def tail(t, carry):
                visit(t)
                return carry

            lax.fori_loop(pad_tile_ref[N_EXPERTS], N_TILES_MOE, tail, 0)

        for_each_cleared_tile(lambda t: clear_copy(t).start())
        for_each_cleared_tile(lambda t: clear_copy(t).wait())

    meta = meta_ref[...]
    row = lax.broadcasted_iota(jnp.int32, (SLOT_ROWS, TS_MIX), 0).astype(F32)
    hit = jnp.logical_or(row == meta[META_P1:META_P1 + 1, :], row == meta[META_P2:META_P2 + 1, :])
    onehot = jnp.where(hit, 1.0, 0.0).astype(BF16)
    permuted = _dot(onehot, x_ref[...].astype(BF16))

    @pl.when(i >= 2)
    def _():
        wait_tile(i - 2, slot)

    perm_ref[slot] = permuted.astype(BF16)
    _for_each_segment_chunk(seg_ref, i, lambda *chunk: chunk_copy(slot)(*chunk).start())

    @pl.when(i == N_TOKEN_TILES - 1)
    def _():
        wait_tile(i - 1, 1 - slot)
        wait_tile(i, slot)


def _dispatch(x, meta, seg, pad_tile):
    return pl.pallas_call(
        _dispatch_kernel,
        grid_spec=pltpu.PrefetchScalarGridSpec(
            num_scalar_prefetch=2,
            grid=(N_TOKEN_TILES,),
            in_specs=[pl.BlockSpec((TS_MIX, D_MODEL), lambda i, *_: (i, 0)),
                      pl.BlockSpec((SUBLANES, TS_MIX), lambda i, *_: (0, i))],
            out_specs=pl.BlockSpec(memory_space=pl.ANY),
            scratch_shapes=[
                pltpu.VMEM((TM_MOE, D_MODEL), BF16),
                pltpu.VMEM((2, SLOT_ROWS, D_MODEL), BF16),
                pltpu.SemaphoreType.DMA(()),
                pltpu.SemaphoreType.DMA((2,)),
            ],
        ),
        out_shape=jax.ShapeDtypeStruct((ROWS_SORTED, D_MODEL), BF16),
        compiler_params=pltpu.CompilerParams(
            dimension_semantics=("arbitrary",), vmem_limit_bytes=VMEM_LIMIT_BYTES),
        name="moe_dispatch",
    )(seg, pad_tile, x, meta)


def _grouped_ffn_kernel(tile_expert_ref, tile_rows_ref, n_active_ref, xs_ref, wg_ref, wu_ref, wd_ref,
                        ys_ref, acc_ref):
    i = pl.program_id(0)
    j = pl.program_id(1)

    active = i < n_active_ref[0]
    rows = tile_rows_ref[i]

    uncovered = jnp.logical_or(jnp.logical_not(active), rows <= TM_MOE - SUB_MOE)

    @pl.when(jnp.logical_and(j == 0, jnp.logical_or(i == 0, uncovered)))
    def _():
        acc_ref[...] = jnp.zeros((TM_MOE, D_MODEL), F32)
        ys_ref[...] = jnp.zeros((TM_MOE, D_MODEL), BF16)

    def accumulate(n_rows):
        xb = xs_ref[0:n_rows, :]
        a = _silu(_dot(xb, wg_ref[...].astype(BF16))) * _dot(xb, wu_ref[...].astype(BF16))
        part = _dot(a.astype(BF16), wd_ref[...].astype(BF16))
        total = jnp.where(j == 0, part, acc_ref[0:n_rows, :] + part)
        acc_ref[0:n_rows, :] = total
        ys_ref[0:n_rows, :] = total.astype(BF16)

    for n_sub in range(1, TM_MOE // SUB_MOE + 1):
        lo, hi = (n_sub - 1) * SUB_MOE, n_sub * SUB_MOE

        @pl.when(jnp.logical_and(active, jnp.logical_and(rows > lo, rows <= hi)))
        def _():
            accumulate(hi)


def _grouped_ffn(xs, w_gate, w_up, w_down, tile_expert, tile_rows, n_active):
    nj = D_FF_EXPERT // TF_MOE

    def row_map(i, j, te, tr, na):
        return (jnp.minimum(i, na[0] - 1), 0)

    def frozen_j(i, j, na):
        return jnp.where(i < na[0], j, nj - 1)

    def up_map(i, j, te, tr, na):
        return (te[i], 0, frozen_j(i, j, na))

    def down_map(i, j, te, tr, na):
        return (te[i], frozen_j(i, j, na), 0)

    return pl.pallas_call(
        _grouped_ffn_kernel,
        grid_spec=pltpu.PrefetchScalarGridSpec(
            num_scalar_prefetch=3,
            grid=(N_TILES_MOE, nj),
            in_specs=[
                pl.BlockSpec((TM_MOE, D_MODEL), row_map),
                pl.BlockSpec((None, D_MODEL, TF_MOE), up_map),
                pl.BlockSpec((None, D_MODEL, TF_MOE), up_map),
                pl.BlockSpec((None, TF_MOE, D_MODEL), down_map),
            ],
            out_specs=pl.BlockSpec((TM_MOE, D_MODEL), lambda i, j, *_: (i, 0)),
            scratch_shapes=[pltpu.VMEM((TM_MOE, D_MODEL), F32)],
        ),
        out_shape=jax.ShapeDtypeStruct((ROWS_SORTED, D_MODEL), BF16),
        compiler_params=pltpu.CompilerParams(
            dimension_semantics=("arbitrary", "arbitrary"), vmem_limit_bytes=VMEM_LIMIT_BYTES),
        name="moe_grouped_ffn",
    )(tile_expert, tile_rows, n_active, xs, w_gate, w_up, w_down)


def _combine_kernel(seg_ref, x_ref, meta_ref, ys_ref, g_ref, b_ref, o_ref, win_ref, sem):
    i = pl.program_id(0)
    slot = i % 2

    def chunk_copy(which):
        def make(local, sorted_row, size):
            return pltpu.make_async_copy(ys_ref.at[pl.ds(sorted_row, size)],
                                         win_ref.at[which, pl.ds(local, size)], sem.at[which])
        return make

    def fetch(tile, which):
        _for_each_segment_chunk(seg_ref, tile, lambda *chunk: chunk_copy(which)(*chunk).start())

    @pl.when(i == 0)
    def _():
        win_ref[...] = jnp.zeros(win_ref.shape, BF16)
        fetch(0, 0)

    @pl.when(i + 1 < N_TOKEN_TILES)
    def _():
        fetch(i + 1, 1 - slot)

    _for_each_segment_chunk(seg_ref, i, lambda *chunk: chunk_copy(slot)(*chunk).wait())

    meta = meta_ref[...]
    row = lax.broadcasted_iota(jnp.int32, (SLOT_ROWS, TS_MIX), 0).astype(F32)
    gate = (jnp.where(row == meta[META_P1:META_P1 + 1, :], meta[META_W1:META_W1 + 1, :], 0.0)
            + jnp.where(row == meta[META_P2:META_P2 + 1, :], meta[META_W2:META_W2 + 1, :], 0.0))
    gate = gate.astype(BF16)
    band = TS_MIX // COMBINE_BANDS
    for b0 in range(0, TS_MIX, band):
        moe = lax.dot_general(gate[:, b0:b0 + band], win_ref[slot], (((0,), (0,)), ((), ())),
                              preferred_element_type=F32)
        o_ref[b0:b0 + band, :] = _layer_norm(ALPHA * x_ref[b0:b0 + band, :] + moe, g_ref[...], b_ref[...])


def _combine(x, meta, ys, seg, ln_g, ln_b):
    row_spec = pl.BlockSpec((TS_MIX, D_MODEL), lambda i, *_: (i, 0))
    return pl.pallas_call(
        _combine_kernel,
        grid_spec=pltpu.PrefetchScalarGridSpec(
            num_scalar_prefetch=1,
            grid=(N_TOKEN_TILES,),
            in_specs=[
                row_spec,
                pl.BlockSpec((SUBLANES, TS_MIX), lambda i, *_: (0, i)),
                pl.BlockSpec(memory_space=pl.ANY),
                pl.BlockSpec((1, D_MODEL), lambda i, *_: (0, 0)),
                pl.BlockSpec((1, D_MODEL), lambda i, *_: (0, 0)),
            ],
            out_specs=row_spec,
            scratch_shapes=[
                pltpu.VMEM((2, SLOT_ROWS, D_MODEL), BF16),
                pltpu.SemaphoreType.DMA((2,)),
            ],
        ),
        out_shape=jax.ShapeDtypeStruct((TOKENS, D_MODEL), F32),
        compiler_params=pltpu.CompilerParams(
            dimension_semantics=("arbitrary",), vmem_limit_bytes=VMEM_LIMIT_BYTES),
        name="moe_combine",
    )(seg, x, meta, ys, ln_g[None], ln_b[None])


def _routing_tables(tile_counts):
    n = tile_counts[:, :, 0].astype(jnp.int32)
    counts = jnp.sum(n, axis=0)
    tiles = (counts + TM_MOE - 1) // TM_MOE
    tile_end = jnp.cumsum(tiles)
    tile_start = tile_end - tiles
    offset = tile_start * TM_MOE
    local_start = jnp.cumsum(n, axis=1) - n
    sorted_start = offset[None, :] + jnp.cumsum(n, axis=0) - n
    seg = jnp.concatenate([n.reshape(-1), local_start.reshape(-1), sorted_start.reshape(-1)])
    n_active = tile_end[-1:]
    tile_id = jnp.minimum(jnp.arange(N_TILES_MOE, dtype=jnp.int32), n_active[0] - 1)
    tile_expert = jnp.sum(tile_id[:, None] >= tile_end[None, :], axis=1).astype(jnp.int32)
    tile_rows = jnp.clip(counts[tile_expert] - (tile_id - tile_start[tile_expert]) * TM_MOE, 0, TM_MOE)
    pad_tile = jnp.concatenate([jnp.where(tiles > 0, tile_end - 1, -1), n_active]).astype(jnp.int32)
    return (seg.astype(jnp.int32), pad_tile, tile_expert, tile_rows.astype(jnp.int32),
            n_active.astype(jnp.int32))


def _even_layer(x, w_in, conv_a_w, conv_a_b, norm_a_g, norm_a_b, pool_w, pool_scale, w_out,
                ln1_g, ln1_b, ffn_w_gate, ffn_w_up, ffn_w_down, ln2_g, ln2_b):
    x = _even_mixer(x, w_in, conv_a_w, conv_a_b, norm_a_g, norm_a_b, pool_w, pool_scale, w_out,
                    ln1_g, ln1_b)
    x = _dense_ffn(x.reshape(TOKENS, D_MODEL), ffn_w_gate, ffn_w_up, ffn_w_down, ln2_g, ln2_b)
    return x.reshape(BATCH, SEQ, D_MODEL)


def _odd_layer(x, w_in, conv_c_w, sgu_norm_g, sgu_norm_b, sgu_w, sgu_b, w_out, ln1_g, ln1_b,
               router, moe_w_gate, moe_w_up, moe_w_down, ln2_g, ln2_b):
    x, meta, tile_counts = _odd_mixer(x, w_in, conv_c_w, sgu_norm_g, sgu_norm_b, sgu_w, sgu_b, w_out,
                                      ln1_g, ln1_b, router)
    x = x.reshape(TOKENS, D_MODEL)
    seg, pad_tile, tile_expert, tile_rows, n_active = _routing_tables(tile_counts)
    xs = _dispatch(x, meta, seg, pad_tile)
    ys = _grouped_ffn(xs, moe_w_gate, moe_w_up, moe_w_down, tile_expert, tile_rows, n_active)
    x = _combine(x, meta, ys, seg, ln2_g, ln2_b)
    return x.reshape(BATCH, SEQ, D_MODEL)


def kernel(x, even_w_in, even_conv_a_w, even_conv_a_b, even_norm_a_g, even_norm_a_b, even_pool_w, even_pool_scale, even_w_out, even_ln1_g, even_ln1_b, even_ffn_w_gate, even_ffn_w_up, even_ffn_w_down, even_ln2_g, even_ln2_b, odd_w_in, odd_conv_c_w, odd_sgu_norm_g, odd_sgu_norm_b, odd_sgu_w, odd_sgu_b, odd_w_out, odd_ln1_g, odd_ln1_b, odd_router, odd_moe_w_gate, odd_moe_w_up, odd_moe_w_down, odd_ln2_g, odd_ln2_b):
    for layer in range(DEPTH):
        i = layer // 2
        if layer % 2 == 0:
            x = _even_layer(x, even_w_in[i], even_conv_a_w[i], even_conv_a_b[i], even_norm_a_g[i],
                            even_norm_a_b[i], even_pool_w[i], even_pool_scale[i], even_w_out[i],
                            even_ln1_g[i], even_ln1_b[i], even_ffn_w_gate[i], even_ffn_w_up[i],
                            even_ffn_w_down[i], even_ln2_g[i], even_ln2_b[i])
        else:
            x = _odd_layer(x, odd_w_in[i], odd_conv_c_w[i], odd_sgu_norm_g[i], odd_sgu_norm_b[i],
                           odd_sgu_w[i], odd_sgu_b[i], odd_w_out[i], odd_ln1_g[i], odd_ln1_b[i],
                           odd_router[i], odd_moe_w_gate[i], odd_moe_w_up[i], odd_moe_w_down[i],
                           odd_ln2_g[i], odd_ln2_b[i])
    return x
```

```python
import jax
import jax.numpy as jnp
from jax import lax
from jax.experimental import pallas as pl
from jax.experimental.pallas import tpu as pltpu

F32 = jnp.float32
BF16 = jnp.bfloat16

D_MODEL = 1024
BATCH = 4
SEQ = 4096
TOKENS = BATCH * SEQ
DEPTH = 2

D_A = 512
CONV_A_WIDTH = 31
D_B = 512
POOL_WINDOWS = (2, 4, 8, 16)
POOL_GROUP_DIM = 128
IN_EVEN = 2 * D_A + D_B

D_C = 512
CONV_C_WIDTH = 3
D_D = 512
SGU_BLOCK = 128
SGU_HEADS = 4
SGU_HEAD_DIM = 128
IN_ODD = 3 * D_C + 2 * D_D

D_FF_DENSE = 2816
N_EXPERTS = 8
D_FF_EXPERT = 3584

ALPHA = (2 * DEPTH) ** 0.25
LN_EPS = 1e-5

SUBLANES = 8
LANES = 128
VMEM_LIMIT_BYTES = 56 * 1024 * 1024

TS_MIX = 512
EVEN_BANDS = 2
MIX_CHAINS = 2
ROW_SET_STRIDE = 4
CONV_SETS = 8
HALO_A = 32
HALO_B = 16
HALO_C = 8
TM_FFN = 1024
FF_CHUNK = 768
TM_MOE = 1024
SUB_MOE = 128
TF_MOE = 512
W_BUFFERS = 3
N_TOKEN_TILES = TOKENS // TS_MIX
N_SEGMENTS = N_TOKEN_TILES * N_EXPERTS
RUN_ALIGN = 16
SLOT_ROWS = 2 * TS_MIX + N_EXPERTS * RUN_ALIGN
MAX_SORTED_ROWS = 2 * TOKENS + N_SEGMENTS * (RUN_ALIGN - 1)
N_TILES_MOE = (MAX_SORTED_ROWS + N_EXPERTS * (TM_MOE - 1)) // TM_MOE
ROWS_SORTED = N_TILES_MOE * TM_MOE


def _layer_norm(x, g, b):
    mu = jnp.mean(x, axis=-1, keepdims=True)
    xc = x - mu
    var = jnp.mean(xc * xc, axis=-1, keepdims=True)
    return xc * lax.rsqrt(var + LN_EPS) * g + b


def _silu(x):
    return x * jax.nn.sigmoid(x)


def _dot(a, b):
    return jnp.dot(a, b, preferred_element_type=F32)


def _even_mixer_kernel(x_ref, win_ref, cw_ref, cb_ref, nag_ref, nab_ref, pw_ref, ps_ref,
                       wout_ref, g1_ref, b1_ref, o_ref, exta_ref, extb_ref, mixf_ref, mixin_ref):
    s = pl.program_id(1)
    ts = TS_MIX

    n_a, n_b = D_A // LANES, D_B // LANES

    @pl.when(s == 0)
    def _():
        exta_ref[:, 0:HALO_A, :] = jnp.zeros((n_a, HALO_A, LANES), F32)
        extb_ref[:, 0:HALO_B, :] = jnp.zeros((n_b, HALO_B, LANES), F32)

    def row_set(first):
        return pl.ds(first, SUBLANES, stride=ROW_SET_STRIDE)

    set_firsts = [blk * SUBLANES * ROW_SET_STRIDE + r
                  for blk in range(ts // (SUBLANES * ROW_SET_STRIDE)) for r in range(ROW_SET_STRIDE)]
    set_step = ROW_SET_STRIDE * lax.broadcasted_iota(jnp.int32, (SUBLANES, 1), 0)

    def pool_sets(firsts):
        for first in firsts:
            row_pos = s * ts + first + set_step
            for g, win in enumerate(POOL_WINDOWS):
                frame = extb_ref[g, row_set(HALO_B + first), :]
                wsum = frame
                for i in range(1, win):
                    wsum = wsum + extb_ref[g, row_set(HALO_B + first - i), :]
                if first >= win - 1:
                    mean = wsum * (1.0 / win)
                else:
                    mean = wsum / jnp.minimum(row_pos + 1, win).astype(F32)
                mixf_ref[n_a + g, row_set(first), :] = mean - frame

    x = x_ref[...]
    xb = x.astype(BF16)
    hb = _dot(xb, win_ref[:, 2 * D_A:])
    for cb in range(n_b):
        extb_ref[cb, HALO_B:HALO_B + ts, :] = hb[:, cb * LANES:(cb + 1) * LANES]
    parts = 2
    cols = D_A // parts
    sets_per_part = len(set_firsts) // parts
    for part in range(parts):
        pool_sets(set_firsts[part * sets_per_part:(part + 1) * sets_per_part])
        val = _dot(xb, win_ref[:, part * cols:(part + 1) * cols])
        gate = _dot(xb, win_ref[:, D_A + part * cols:D_A + (part + 1) * cols])
        glu = val * jax.nn.sigmoid(gate)
        for cb in range(cols // LANES):
            exta_ref[part * (cols // LANES) + cb, HALO_A:HALO_A + ts, :] = glu[:, cb * LANES:(cb + 1) * LANES]

    for g in range(n_b):
        lo, hi = g * POOL_GROUP_DIM, (g + 1) * POOL_GROUP_DIM
        mixed = _dot(mixf_ref[n_a + g].astype(BF16), pw_ref[g]) * ps_ref[:, lo:hi]
        mixin_ref[:, D_A + lo:D_A + hi] = mixed.astype(BF16)
    mix_pool = _dot(mixin_ref[:, D_A:], wout_ref[D_A:, :])

    band_rows = ts // EVEN_BANDS
    sets_per_band = len(set_firsts) // EVEN_BANDS
    norm_rows = 128
    for band in range(EVEN_BANDS):
        band_sets = set_firsts[band * sets_per_band:(band + 1) * sets_per_band]
        for cb in range(n_a):
            lanes = slice(cb * LANES, (cb + 1) * LANES)
            for group in range(0, len(band_sets), CONV_SETS):
                firsts = band_sets[group:group + CONV_SETS]
                accs = [jnp.broadcast_to(cb_ref[:, lanes], (SUBLANES, LANES))] * len(firsts)
                for k in range(CONV_A_WIDTH):
                    wk = jnp.broadcast_to(cw_ref[k:k + 1, lanes], (SUBLANES, LANES))
                    back = CONV_A_WIDTH - 1 - k
                    for n, first in enumerate(firsts):
                        accs[n] = accs[n] + wk * exta_ref[cb, row_set(HALO_A + first - back), :]
                for n, first in enumerate(firsts):
                    mixf_ref[cb, row_set(first), :] = accs[n]

        b0, b1 = band * band_rows, (band + 1) * band_rows
        for r0 in range(b0, b1, norm_rows):
            conv = jnp.concatenate([mixf_ref[cb, r0:r0 + norm_rows, :] for cb in range(n_a)], axis=1)
            y = _silu(_layer_norm(conv, nag_ref[...], nab_ref[...]))
            mixin_ref[r0:r0 + norm_rows, 0:D_A] = y.astype(BF16)
        mix = _dot(mixin_ref[b0:b1, 0:D_A], wout_ref[0:D_A, :]) + mix_pool[b0:b1, :]
        o_ref[b0:b1, :] = _layer_norm(ALPHA * x[b0:b1, :] + mix, g1_ref[...], b1_ref[...])

    exta_ref[:, 0:HALO_A, :] = exta_ref[:, ts:ts + HALO_A, :]
    extb_ref[:, 0:HALO_B, :] = extb_ref[:, ts:ts + HALO_B, :]


def _const_spec(shape):
    return pl.BlockSpec(shape, lambda *_: (0,) * len(shape))


def _even_mixer(x, w_in, conv_w, conv_b, norm_g, norm_b, pool_w, pool_scale, w_out, ln_g, ln_b):
    ts = TS_MIX
    x_spec = pl.BlockSpec((None, ts, D_MODEL), lambda b, s: (b, s, 0))
    return pl.pallas_call(
        _even_mixer_kernel,
        grid=(BATCH, SEQ // ts),
        in_specs=[
            x_spec,
            _const_spec((D_MODEL, IN_EVEN)),
            _const_spec((CONV_A_WIDTH, D_A)),
            _const_spec((1, D_A)),
            _const_spec((1, D_A)),
            _const_spec((1, D_A)),
            _const_spec((len(POOL_WINDOWS), POOL_GROUP_DIM, POOL_GROUP_DIM)),
            _const_spec((1, D_B)),
            _const_spec((D_A + D_B, D_MODEL)),
            _const_spec((1, D_MODEL)),
            _const_spec((1, D_MODEL)),
        ],
        out_specs=x_spec,
        out_shape=jax.ShapeDtypeStruct((BATCH, SEQ, D_MODEL), F32),
        scratch_shapes=[
            pltpu.VMEM((D_A // LANES, HALO_A + ts, LANES), F32),
            pltpu.VMEM((D_B // LANES, HALO_B + ts, LANES), F32),
            pltpu.VMEM(((D_A + D_B) // LANES, ts, LANES), F32),
            pltpu.VMEM((ts, D_A + D_B), BF16),
        ],
        compiler_params=pltpu.CompilerParams(
            dimension_semantics=("arbitrary", "arbitrary"), vmem_limit_bytes=VMEM_LIMIT_BYTES),
        name="even_mixer",
    )(x, w_in.astype(BF16), conv_w, conv_b[None], norm_g[None], norm_b[None],
      pool_w.astype(BF16), pool_scale[None], w_out.astype(BF16), ln_g[None], ln_b[None])


def _ff_chunks(total, chunk):
    bounds = list(range(0, total, chunk)) + [total]
    return list(zip(bounds[:-1], bounds[1:]))


def _dense_ffn_kernel(x_ref, wg_ref, wu_ref, wd_ref, g_ref, b_ref, o_ref, acc_ref):
    x = x_ref[...]
    xb = x.astype(BF16)
    for n, (lo, hi) in enumerate(_ff_chunks(D_FF_DENSE, FF_CHUNK)):
        a = _silu(_dot(xb, wg_ref[:, lo:hi])) * _dot(xb, wu_ref[:, lo:hi])
        part = _dot(a.astype(BF16), wd_ref[lo:hi, :])
        if n == 0:
            acc_ref[...] = part
        else:
            acc_ref[...] += part
    o_ref[...] = _layer_norm(ALPHA * x + acc_ref[...], g_ref[...], b_ref[...])


def _dense_ffn(x, w_gate, w_up, w_down, ln_g, ln_b):
    tm = TM_FFN
    row_spec = pl.BlockSpec((tm, D_MODEL), lambda i: (i, 0))
    resident = pl.BlockSpec(memory_space=pltpu.VMEM)
    return pl.pallas_call(
        _dense_ffn_kernel,
        grid=(TOKENS // tm,),
        in_specs=[row_spec, resident, resident, resident,
                  _const_spec((1, D_MODEL)), _const_spec((1, D_MODEL))],
        out_specs=row_spec,
        out_shape=jax.ShapeDtypeStruct((TOKENS, D_MODEL), F32),
        scratch_shapes=[pltpu.VMEM((tm, D_MODEL), F32)],
        compiler_params=pltpu.CompilerParams(
            dimension_semantics=("arbitrary",), vmem_limit_bytes=VMEM_LIMIT_BYTES),
        name="dense_ffn",
    )(x, w_gate.astype(BF16), w_up.astype(BF16), w_down.astype(BF16), ln_g[None], ln_b[None])


META_E1, META_E2, META_W1, META_W2, META_P1, META_P2 = range(6)
ROUTER_ROWS = 16


def _odd_mixer_kernel(x_ref, win_ref, ccw_ref, sng_ref, snb_ref, sw_ref, sb_ref, wout_ref,
                      g1_ref, b1_ref, rtr_ref, o_ref, meta_ref, cnt_ref,
                      extc_ref, mixin_ref):
    s = pl.program_id(1)
    ts = TS_MIX

    @pl.when(s == 0)
    def _():
        extc_ref[0:HALO_C, :] = jnp.zeros((HALO_C, D_C), F32)

    tri = (lax.broadcasted_iota(jnp.int32, (SGU_BLOCK, SGU_BLOCK), 0)
           >= lax.broadcasted_iota(jnp.int32, (SGU_BLOCK, SGU_BLOCK), 1))
    sgu_w = [jnp.where(tri, sw_ref[hd], 0.0).astype(BF16) for hd in range(SGU_HEADS)]

    chain_rows = ts // MIX_CHAINS
    blocks = chain_rows // SGU_BLOCK
    x_parts = [x_ref[c * chain_rows:(c + 1) * chain_rows, :] for c in range(MIX_CHAINS)]
    xb_parts = [x.astype(BF16) for x in x_parts]

    def project(c, lo, hi):
        return _dot(xb_parts[c], win_ref[:, lo:hi])

    def short_conv(c, conv_in):
        c0, c1 = c * chain_rows, (c + 1) * chain_rows
        extc_ref[HALO_C + c0:HALO_C + c1, :] = conv_in[:, D_C:2 * D_C] * conv_in[:, 2 * D_C:3 * D_C]
        conv = ccw_ref[0:1, :] * extc_ref[HALO_C - 2 + c0:HALO_C - 2 + c1, :]
        conv = conv + ccw_ref[1:2, :] * extc_ref[HALO_C - 1 + c0:HALO_C - 1 + c1, :]
        conv = conv + ccw_ref[2:3, :] * extc_ref[HALO_C + c0:HALO_C + c1, :]
        mixin_ref[c0:c1, 0:D_C] = (conv_in[:, 0:D_C] * conv).astype(BF16)

    def spatial_gate(c, sgu_in):
        c0 = c * chain_rows
        z = jax.nn.gelu(sgu_in)
        u = z[:, :D_D]
        v = _layer_norm(z[:, D_D:], sng_ref[...], snb_ref[...]).astype(BF16)
        for hd in range(SGU_HEADS):
            lo, hi = hd * SGU_HEAD_DIM, (hd + 1) * SGU_HEAD_DIM
            v_blocks = jnp.concatenate(
                [v[blk * SGU_BLOCK:(blk + 1) * SGU_BLOCK, lo:hi] for blk in range(blocks)], axis=1)
            mixed = _dot(sgu_w[hd], v_blocks) + sb_ref[:, hd:hd + 1]
            for blk in range(blocks):
                r0, r1 = blk * SGU_BLOCK, (blk + 1) * SGU_BLOCK
                gated = u[r0:r1, lo:hi] * mixed[:, blk * SGU_HEAD_DIM:(blk + 1) * SGU_HEAD_DIM]
                mixin_ref[c0 + r0:c0 + r1, D_C + lo:D_C + hi] = gated.astype(BF16)

    def finish(c):
        c0, c1 = c * chain_rows, (c + 1) * chain_rows
        mix = _dot(mixin_ref[c0:c1, :], wout_ref[...])
        x1_c = _layer_norm(ALPHA * x_parts[c] + mix, g1_ref[...], b1_ref[...])
        o_ref[c0:c1, :] = x1_c
        return x1_c

    conv_cols, sgu_cols = (0, 3 * D_C), (3 * D_C, IN_ODD)
    conv_in = project(0, *conv_cols)
    sgu_in = project(0, *sgu_cols)
    x1_parts = []
    for c in range(MIX_CHAINS):
        following = c + 1 < MIX_CHAINS
        next_conv_in = project(c + 1, *conv_cols) if following else None
        short_conv(c, conv_in)
        next_sgu_in = project(c + 1, *sgu_cols) if following else None
        spatial_gate(c, sgu_in)
        x1_parts.append(finish(c))
        conv_in, sgu_in = next_conv_in, next_sgu_in
    x1 = jnp.concatenate(x1_parts, axis=0)

    extc_ref[0:HALO_C, :] = extc_ref[ts:ts + HALO_C, :]

    x_hi = x1.astype(BF16)
    x_lo = (x1 - x_hi.astype(F32)).astype(BF16)
    nt = (((1,), (1,)), ((), ()))
    by_hi = lax.dot_general(rtr_ref[...], x_hi, nt, preferred_element_type=F32)
    by_lo = lax.dot_general(rtr_ref[0:ROUTER_ROWS, :], x_lo, nt, preferred_element_type=F32)
    logits = (by_hi[0:N_EXPERTS, :] + by_hi[ROUTER_ROWS:ROUTER_ROWS + N_EXPERTS, :]
              + by_lo[0:N_EXPERTS, :])

    expert = lax.broadcasted_iota(jnp.int32, (N_EXPERTS, ts), 0).astype(F32)
    neg_inf = jnp.float32(-jnp.inf)
    m1 = jnp.max(logits, axis=0, keepdims=True)
    e1 = jnp.min(jnp.where(logits == m1, expert, float(N_EXPERTS)), axis=0, keepdims=True)
    rest = jnp.where(expert == e1, neg_inf, logits)
    m2 = jnp.max(rest, axis=0, keepdims=True)
    e2 = jnp.min(jnp.where(rest == m2, expert, float(N_EXPERTS)), axis=0, keepdims=True)
    ex = jnp.exp(m2 - m1)
    den = 1.0 + ex
    w1 = 1.0 / den
    w2 = ex / den

    onehot = jnp.where(jnp.logical_or(expert == e1, expert == e2), 1.0, 0.0)
    onehot_b = jnp.concatenate([onehot, jnp.zeros_like(onehot)], axis=0).astype(BF16)
    before = (lax.broadcasted_iota(jnp.int32, (ts, ts), 0)
              < lax.broadcasted_iota(jnp.int32, (ts, ts), 1))
    earlier = _dot(onehot_b, jnp.where(before, 1.0, 0.0).astype(BF16))[0:N_EXPERTS, :]
    run = jnp.sum(onehot, axis=1, keepdims=True)
    run = jnp.floor((run + (RUN_ALIGN - 1)) * (1.0 / RUN_ALIGN)) * RUN_ALIGN
    expert_col = expert[:, 0:1]
    lower_rows = jnp.zeros((N_EXPERTS, 1), F32)
    for e in range(N_EXPERTS - 1):
        lower_rows = lower_rows + jnp.where(expert_col > e, run[e:e + 1, :], 0.0)
    place = earlier + lower_rows
    p1 = jnp.sum(jnp.where(expert == e1, place, 0.0), axis=0, keepdims=True)
    p2 = jnp.sum(jnp.where(expert == e2, place, 0.0), axis=0, keepdims=True)

    records = {META_E1: e1, META_E2: e2, META_W1: w1, META_W2: w2, META_P1: p1, META_P2: p2}
    meta_ref[...] = jnp.concatenate(
        [records.get(row, jnp.zeros((1, ts), F32)) for row in range(SUBLANES)], axis=0)
    cnt_ref[...] = jnp.broadcast_to(run, (N_EXPERTS, LANES))


def _odd_mixer(x, w_in, conv_w, sgu_norm_g, sgu_norm_b, sgu_w, sgu_b, w_out, ln_g, ln_b, router):
    ts = TS_MIX
    x_spec = pl.BlockSpec((None, ts, D_MODEL), lambda b, s: (b, s, 0))
    router_pad = jnp.zeros((ROUTER_ROWS, D_MODEL), F32).at[:N_EXPERTS, :].set(router.T)
    router_hi = router_pad.astype(BF16)
    router_lo = (router_pad - router_hi.astype(F32)).astype(BF16)
    router_parts = jnp.concatenate([router_hi, router_lo], axis=0)
    return pl.pallas_call(
        _odd_mixer_kernel,
        grid=(BATCH, SEQ // ts),
        in_specs=[
            x_spec,
            _const_spec((D_MODEL, IN_ODD)),
            _const_spec((CONV_C_WIDTH, D_C)),
            _const_spec((1, D_D)),
            _const_spec((1, D_D)),
            _const_spec((SGU_HEADS, SGU_BLOCK, SGU_BLOCK)),
            _const_spec((SGU_BLOCK, SGU_HEADS)),
            _const_spec((D_C + D_D, D_MODEL)),
            _const_spec((1, D_MODEL)),
            _const_spec((1, D_MODEL)),
            _const_spec((2 * ROUTER_ROWS, D_MODEL)),
        ],
        out_specs=[
            x_spec,
            pl.BlockSpec((SUBLANES, ts), lambda b, s: (0, b * (SEQ // ts) + s)),
            pl.BlockSpec((None, N_EXPERTS, LANES), lambda b, s: (b * (SEQ // ts) + s, 0, 0)),
        ],
        out_shape=[
            jax.ShapeDtypeStruct((BATCH, SEQ, D_MODEL), F32),
            jax.ShapeDtypeStruct((SUBLANES, TOKENS), F32),
            jax.ShapeDtypeStruct((N_TOKEN_TILES, N_EXPERTS, LANES), F32),
        ],
        scratch_shapes=[
            pltpu.VMEM((HALO_C + ts, D_C), F32),
            pltpu.VMEM((ts, D_C + D_D), BF16),
        ],
        compiler_params=pltpu.CompilerParams(
            dimension_semantics=("arbitrary", "arbitrary"), vmem_limit_bytes=VMEM_LIMIT_BYTES),
        name="odd_mixer",
    )(x, w_in.astype(BF16), conv_w, sgu_norm_g[None], sgu_norm_b[None], sgu_w, sgu_b.T,
      w_out.astype(BF16), ln_g[None], ln_b[None], router_parts)


SEGMENT_BITS = range(RUN_ALIGN.bit_length() - 1, TS_MIX.bit_length())


def _for_each_segment_chunk(seg_ref, tile, visit):
    for e in range(N_EXPERTS):
        k = tile * N_EXPERTS + e
        n = seg_ref[k]
        local = seg_ref[N_SEGMENTS + k]
        sorted_row = seg_ref[2 * N_SEGMENTS + k]
        for bit in reversed(SEGMENT_BITS):
            @pl.when(((n >> bit) & 1) == 1)
            def _():
                done = (n >> (bit + 1)) << (bit + 1)
                visit(pl.multiple_of(local + done, RUN_ALIGN), pl.multiple_of(sorted_row + done, RUN_ALIGN),
                      1 << bit)


def _dispatch_kernel(seg_ref, pad_tile_ref, x_ref, meta_ref, xs_ref, zero_ref, perm_ref, zero_sem, perm_sem):
    i = pl.program_id(0)
    slot = i % 2

    def chunk_copy(which):
        def make(local, sorted_row, size):
            return pltpu.make_async_copy(perm_ref.at[which, pl.ds(local, size)],
                                         xs_ref.at[pl.ds(sorted_row, size)], perm_sem.at[which])
        return make

    def wait_tile(tile, which):
        _for_each_segment_chunk(seg_ref, tile, lambda *chunk: chunk_copy(which)(*chunk).wait())

    @pl.when(i == 0)
    def _():
        zero_ref[...] = jnp.zeros(zero_ref.shape, BF16)

        def clear_copy(t):
            start = pl.multiple_of(t * TM_MOE, TM_MOE)
            return pltpu.make_async_copy(zero_ref, xs_ref.at[pl.ds(start, TM_MOE)], zero_sem)

        def for_each_cleared_tile(visit):
            for e in range(N_EXPERTS):
                @pl.when(pad_tile_ref[e] >= 0)
                def _():
                    visit(pad_tile_ref[e])

            def tail(t, carry):
                visit(t)
                return carry

            lax.fori_loop(pad_tile_ref[N_EXPERTS], N_TILES_MOE, tail, 0)

        for_each_cleared_tile(lambda t: clear_copy(t).start())
        for_each_cleared_tile(lambda t: clear_copy(t).wait())

    meta = meta_ref[...]
    row = lax.broadcasted_iota(jnp.int32, (SLOT_ROWS, TS_MIX), 0).astype(F32)
    hit = jnp.logical_or(row == meta[META_P1:META_P1 + 1, :], row == meta[META_P2:META_P2 + 1, :])
    onehot = jnp.where(hit, 1.0, 0.0).astype(BF16)
    permuted = _dot(onehot, x_ref[...].astype(BF16))

    @pl.when(i >= 2)
    def _():
        wait_tile(i - 2, slot)

    perm_ref[slot] = permuted.astype(BF16)
    _for_each_segment_chunk(seg_ref, i, lambda *chunk: chunk_copy(slot)(*chunk).start())

    @pl.when(i == N_TOKEN_TILES - 1)
    def _():
        wait_tile(i - 1, 1 - slot)
        wait_tile(i, slot)


def _dispatch(x, meta, seg, pad_tile):
    return pl.pallas_call(
        _dispatch_kernel,
        grid_spec=pltpu.PrefetchScalarGridSpec(
            num_scalar_prefetch=2,
            grid=(N_TOKEN_TILES,),
            in_specs=[pl.BlockSpec((TS_MIX, D_MODEL), lambda i, *_: (i, 0)),
                      pl.BlockSpec((SUBLANES, TS_MIX), lambda i, *_: (0, i))],
            out_specs=pl.BlockSpec(memory_space=pl.ANY),
            scratch_shapes=[
                pltpu.VMEM((TM_MOE, D_MODEL), BF16),
                pltpu.VMEM((2, SLOT_ROWS, D_MODEL), BF16),
                pltpu.SemaphoreType.DMA(()),
                pltpu.SemaphoreType.DMA((2,)),
            ],
        ),
        out_shape=jax.ShapeDtypeStruct((ROWS_SORTED, D_MODEL), BF16),
        compiler_params=pltpu.CompilerParams(
            dimension_semantics=("arbitrary",), vmem_limit_bytes=VMEM_LIMIT_BYTES),
        name="moe_dispatch",
    )(seg, pad_tile, x, meta)


def _grouped_ffn_kernel(tile_expert_ref, tile_rows_ref, n_active_ref, xs_ref, wg_hbm, wu_hbm, wd_hbm,
                        ys_ref, acc_ref, wg_buf, wu_buf, wd_buf, w_sem):
    i = pl.program_id(0)
    j = pl.program_id(1)

    active = i < n_active_ref[0]
    rows = tile_rows_ref[i]

    nj = D_FF_EXPERT // TF_MOE
    n_steps = n_active_ref[0] * nj
    step = i * nj + j
    slot = step % W_BUFFERS

    def weight_copies(s):
        e = tile_expert_ref[s // nj]
        col = pl.multiple_of((s % nj) * TF_MOE, TF_MOE)
        into = s % W_BUFFERS
        return (pltpu.make_async_copy(wg_hbm.at[e, :, pl.ds(col, TF_MOE)], wg_buf.at[into], w_sem.at[0, into]),
                pltpu.make_async_copy(wu_hbm.at[e, :, pl.ds(col, TF_MOE)], wu_buf.at[into], w_sem.at[1, into]),
                pltpu.make_async_copy(wd_hbm.at[e, pl.ds(col, TF_MOE), :], wd_buf.at[into], w_sem.at[2, into]))

    @pl.when(step == 0)
    def _():
        for s in range(W_BUFFERS - 1):
            @pl.when(s < n_steps)
            def _():
                for cp in weight_copies(s):
                    cp.start(priority=1)

    @pl.when(step + (W_BUFFERS - 1) < n_steps)
    def _():
        for cp in weight_copies(step + (W_BUFFERS - 1)):
            cp.start(priority=1)

    @pl.when(active)
    def _():
        for cp in weight_copies(step):
            cp.wait()

    uncovered = jnp.logical_or(jnp.logical_not(active), rows <= TM_MOE - SUB_MOE)

    @pl.when(jnp.logical_and(j == 0, jnp.logical_or(i == 0, uncovered)))
    def _():
        acc_ref[...] = jnp.zeros((TM_MOE, D_MODEL), F32)
        ys_ref[...] = jnp.zeros((TM_MOE, D_MODEL), BF16)

    def accumulate(n_rows):
        xb = xs_ref[0:n_rows, :]
        a = _silu(_dot(xb, wg_buf[slot].astype(BF16))) * _dot(xb, wu_buf[slot].astype(BF16))
        part = _dot(a.astype(BF16), wd_buf[slot].astype(BF16))
        total = jnp.where(j == 0, part, acc_ref[0:n_rows, :] + part)
        acc_ref[0:n_rows, :] = total
        ys_ref[0:n_rows, :] = total.astype(BF16)

    for n_sub in range(1, TM_MOE // SUB_MOE + 1):
        lo, hi = (n_sub - 1) * SUB_MOE, n_sub * SUB_MOE

        @pl.when(jnp.logical_and(active, jnp.logical_and(rows > lo, rows <= hi)))
        def _():
            accumulate(hi)


def _grouped_ffn(xs, w_gate, w_up, w_down, tile_expert, tile_rows, n_active):
    nj = D_FF_EXPERT // TF_MOE

    def row_map(i, j, te, tr, na):
        return (jnp.minimum(i, na[0] - 1), 0)

    in_hbm = pl.BlockSpec(memory_space=pl.ANY)
    return pl.pallas_call(
        _grouped_ffn_kernel,
        grid_spec=pltpu.PrefetchScalarGridSpec(
            num_scalar_prefetch=3,
            grid=(N_TILES_MOE, nj),
            in_specs=[
                pl.BlockSpec((TM_MOE, D_MODEL), row_map),
                in_hbm, in_hbm, in_hbm,
            ],
            out_specs=pl.BlockSpec((TM_MOE, D_MODEL), lambda i, j, *_: (i, 0)),
            scratch_shapes=[
                pltpu.VMEM((TM_MOE, D_MODEL), F32),
                pltpu.VMEM((W_BUFFERS, D_MODEL, TF_MOE), F32),
                pltpu.VMEM((W_BUFFERS, D_MODEL, TF_MOE), F32),
                pltpu.VMEM((W_BUFFERS, TF_MOE, D_MODEL), F32),
                pltpu.SemaphoreType.DMA((3, W_BUFFERS)),
            ],
        ),
        out_shape=jax.ShapeDtypeStruct((ROWS_SORTED, D_MODEL), BF16),
        compiler_params=pltpu.CompilerParams(
            dimension_semantics=("arbitrary", "arbitrary"), vmem_limit_bytes=VMEM_LIMIT_BYTES),
        name="moe_grouped_ffn",
    )(tile_expert, tile_rows, n_active, xs, w_gate, w_up, w_down)


def _combine_kernel(seg_ref, x_ref, meta_ref, ys_ref, g_ref, b_ref, o_ref, win_ref, sem):
    i = pl.program_id(0)
    slot = i % 2

    def chunk_copy(which):
        def make(local, sorted_row, size):
            return pltpu.make_async_copy(ys_ref.at[pl.ds(sorted_row, size)],
                                         win_ref.at[which, pl.ds(local, size)], sem.at[which])
        return make

    def fetch(tile, which):
        _for_each_segment_chunk(seg_ref, tile, lambda *chunk: chunk_copy(which)(*chunk).start())

    @pl.when(i == 0)
    def _():
        win_ref[...] = jnp.zeros(win_ref.shape, BF16)
        fetch(0, 0)

    @pl.when(i + 1 < N_TOKEN_TILES)
    def _():
        fetch(i + 1, 1 - slot)

    _for_each_segment_chunk(seg_ref, i, lambda *chunk: chunk_copy(slot)(*chunk).wait())

    meta = meta_ref[...]
    row = lax.broadcasted_iota(jnp.int32, (SLOT_ROWS, TS_MIX), 0).astype(F32)
    gate = (jnp.where(row == meta[META_P1:META_P1 + 1, :], meta[META_W1:META_W1 + 1, :], 0.0)
            + jnp.where(row == meta[META_P2:META_P2 + 1, :], meta[META_W2:META_W2 + 1, :], 0.0))
    moe = lax.dot_general(gate.astype(BF16), win_ref[slot], (((0,), (0,)), ((), ())),
                          preferred_element_type=F32)
    o_ref[...] = _layer_norm(ALPHA * x_ref[...] + moe, g_ref[...], b_ref[...])


def _combine(x, meta, ys, seg, ln_g, ln_b):
    row_spec = pl.BlockSpec((TS_MIX, D_MODEL), lambda i, *_: (i, 0))
    return pl.pallas_call(
        _combine_kernel,
        grid_spec=pltpu.PrefetchScalarGridSpec(
            num_scalar_prefetch=1,
            grid=(N_TOKEN_TILES,),
            in_specs=[
                row_spec,
                pl.BlockSpec((SUBLANES, TS_MIX), lambda i, *_: (0, i)),
                pl.BlockSpec(memory_space=pl.ANY),
                pl.BlockSpec((1, D_MODEL), lambda i, *_: (0, 0)),
                pl.BlockSpec((1, D_MODEL), lambda i, *_: (0, 0)),
            ],
            out_specs=row_spec,
            scratch_shapes=[
                pltpu.VMEM((2, SLOT_ROWS, D_MODEL), BF16),
                pltpu.SemaphoreType.DMA((2,)),
            ],
        ),
        out_shape=jax.ShapeDtypeStruct((TOKENS, D_MODEL), F32),
        compiler_params=pltpu.CompilerParams(
            dimension_semantics=("arbitrary",), vmem_limit_bytes=VMEM_LIMIT_BYTES),
        name="moe_combine",
    )(seg, x, meta, ys, ln_g[None], ln_b[None])


def _routing_tables(tile_counts):
    n = tile_counts[:, :, 0].astype(jnp.int32)
    counts = jnp.sum(n, axis=0)
    tiles = (counts + TM_MOE - 1) // TM_MOE
    tile_end = jnp.cumsum(tiles)
    tile_start = tile_end - tiles
    offset = tile_start * TM_MOE
    local_start = jnp.cumsum(n, axis=1) - n
    sorted_start = offset[None, :] + jnp.cumsum(n, axis=0) - n
    seg = jnp.concatenate([n.reshape(-1), local_start.reshape(-1), sorted_start.reshape(-1)])
    n_active = tile_end[-1:]
    tile_id = jnp.minimum(jnp.arange(N_TILES_MOE, dtype=jnp.int32), n_active[0] - 1)
    tile_expert = jnp.sum(tile_id[:, None] >= tile_end[None, :], axis=1).astype(jnp.int32)
    tile_rows = jnp.clip(counts[tile_expert] - (tile_id - tile_start[tile_expert]) * TM_MOE, 0, TM_MOE)
    pad_tile = jnp.concatenate([jnp.where(tiles > 0, tile_end - 1, -1), n_active]).astype(jnp.int32)
    return (seg.astype(jnp.int32), pad_tile, tile_expert, tile_rows.astype(jnp.int32),
            n_active.astype(jnp.int32))


def _even_layer(x, w_in, conv_a_w, conv_a_b, norm_a_g, norm_a_b, pool_w, pool_scale, w_out,
                ln1_g, ln1_b, ffn_w_gate, ffn_w_up, ffn_w_down, ln2_g, ln2_b):
    x = _even_mixer(x, w_in, conv_a_w, conv_a_b, norm_a_g, norm_a_b, pool_w, pool_scale, w_out,
                    ln1_g, ln1_b)
    x = _dense_ffn(x.reshape(TOKENS, D_MODEL), ffn_w_gate, ffn_w_up, ffn_w_down, ln2_g, ln2_b)
    return x.reshape(BATCH, SEQ, D_MODEL)


def _odd_layer(x, w_in, conv_c_w, sgu_norm_g, sgu_norm_b, sgu_w, sgu_b, w_out, ln1_g, ln1_b,
               router, moe_w_gate, moe_w_up, moe_w_down, ln2_g, ln2_b):
    x, meta, tile_counts = _odd_mixer(x, w_in, conv_c_w, sgu_norm_g, sgu_norm_b, sgu_w, sgu_b, w_out,
                                      ln1_g, ln1_b, router)
    x = x.reshape(TOKENS, D_MODEL)
    seg, pad_tile, tile_expert, tile_rows, n_active = _routing_tables(tile_counts)
    xs = _dispatch(x, meta, seg, pad_tile)
    ys = _grouped_ffn(xs, moe_w_gate, moe_w_up, moe_w_down, tile_expert, tile_rows, n_active)
    x = _combine(x, meta, ys, seg, ln2_g, ln2_b)
    return x.reshape(BATCH, SEQ, D_MODEL)


def kernel(x, even_w_in, even_conv_a_w, even_conv_a_b, even_norm_a_g, even_norm_a_b, even_pool_w, even_pool_scale, even_w_out, even_ln1_g, even_ln1_b, even_ffn_w_gate, even_ffn_w_up, even_ffn_w_down, even_ln2_g, even_ln2_b, odd_w_in, odd_conv_c_w, odd_sgu_norm_g, odd_sgu_norm_b, odd_sgu_w, odd_sgu_b, odd_w_out, odd_ln1_g, odd_ln1_b, odd_router, odd_moe_w_gate, odd_moe_w_up, odd_moe_w_down, odd_ln2_g, odd_ln2_b):
    for layer in range(DEPTH):
        i = layer // 2
        if layer % 2 == 0:
            x = _even_layer(x, even_w_in[i], even_conv_a_w[i], even_conv_a_b[i], even_norm_a_g[i],
                            even_norm_a_b[i], even_pool_w[i], even_pool_scale[i], even_w_out[i],
                            even_ln1_g[i], even_ln1_b[i], even_ffn_w_gate[i], even_ffn_w_up[i],
                            even_ffn_w_down[i], even_ln2_g[i], even_ln2_b[i])
        else:
            x = _odd_layer(x, odd_w_in[i], odd_conv_c_w[i], odd_sgu_norm_g[i], odd_sgu_norm_b[i],
                           odd_sgu_w[i], odd_sgu_b[i], odd_w_out[i], odd_ln1_g[i], odd_ln1_b[i],
                           odd_router[i], odd_moe_w_gate[i], odd_moe_w_up[i], odd_moe_w_down[i],
                           odd_ln2_g[i], odd_ln2_b[i])
    return x
```
